```python
import jax
import jax.numpy as jnp
from jax import lax
import numpy as np

D_MODEL = 1024
BATCH = 8
SEQ = 2048
DEPTH = 1

CTX_LEN = 256
GRID_W = 64
EPS = 1e-6
ATT_HEADS = 8
ATT_KV_HEADS = 2
ATT_GROUP = ATT_HEADS // ATT_KV_HEADS
HEAD_DIM = 64
WINDOW = 128
ATT_BLOCK = 128
ROPE_BASE = 10000.0
AXIS_ROT = HEAD_DIM // 2
GLA_HEADS = 4
GLA_DK = 64
GLA_DV = 128
GLA_RANK = 16
GLA_TAU = 16.0
GLA_CHUNK = 64
N_EXPERTS = 32
TOP_K = 4
D_FF = D_MODEL
SWIGLU_ALPHA = 1.702
SWIGLU_LIMIT = 7.0
MOE_BLOCK = 256

ATT_W = ATT_HEADS * HEAD_DIM
ATT_KV_W = ATT_KV_HEADS * HEAD_DIM
GLA_K_W = GLA_HEADS * GLA_DK
GLA_V_W = GLA_HEADS * GLA_DV
IN_SPLITS = (ATT_W, ATT_KV_W, ATT_KV_W, GLA_K_W, GLA_K_W, GLA_V_W, GLA_V_W, GLA_RANK, GLA_RANK, D_MODEL, D_MODEL)
C_IN = sum(IN_SPLITS)

kernel_name = 'hybrid_swa_gla_moe_dit_layer'


def rms_norm(x, g):
    xf = x.astype(jnp.float32)
    y = xf * lax.rsqrt(jnp.mean(xf * xf, axis=-1, keepdims=True) + EPS)
    return (y * g.astype(jnp.float32)).astype(x.dtype)


def axial_rope_angles(n_tok):
    rows = n_tok // GRID_W
    row = jnp.repeat(jnp.arange(rows, dtype=jnp.float32), GRID_W)
    col = jnp.tile(jnp.arange(GRID_W, dtype=jnp.float32), rows)
    inv = ROPE_BASE ** (-jnp.arange(0, AXIS_ROT, 2, dtype=jnp.float32) / AXIS_ROT)
    return row[:, None] * inv, col[:, None] * inv


def rotate_pairs(x, ang):
    m = x.shape[-1] // 2
    cos = jnp.cos(ang)[:, None, :].astype(x.dtype)
    sin = jnp.sin(ang)[:, None, :].astype(x.dtype)
    x1, x2 = x[..., :m], x[..., m:]
    return jnp.concatenate([x1 * cos - x2 * sin, x2 * cos + x1 * sin], axis=-1)


def axial_rope(x, ang_r, ang_c):
    return jnp.concatenate([rotate_pairs(x[..., :AXIS_ROT], ang_r),
                            rotate_pairs(x[..., AXIS_ROT:], ang_c)], axis=-1)


def mixer_inputs(h, w_in, q_norm, k_norm, w_alpha_f, b_alpha_f, w_alpha_b, b_alpha_b):
    B, T, _ = h.shape
    split_at = np.cumsum(IN_SPLITS)[:-1].tolist()
    aq, ak, av, gq, gk, gv, gr, lr_f, lr_b, ga, gg = jnp.split(h @ w_in, split_at, axis=-1)
    aq = rms_norm(aq.reshape(B, T, ATT_HEADS, HEAD_DIM), q_norm)
    ak = rms_norm(ak.reshape(B, T, ATT_KV_HEADS, HEAD_DIM), k_norm)
    av = av.reshape(B, T, ATT_KV_HEADS, HEAD_DIM)
    gq = gq.reshape(B, T, GLA_HEADS, GLA_DK) * (GLA_DK ** -0.5)
    gk = gk.reshape(B, T, GLA_HEADS, GLA_DK)
    gv = gv.reshape(B, T, GLA_HEADS, GLA_DV)
    la_f = jax.nn.log_sigmoid((lr_f @ w_alpha_f + b_alpha_f).astype(jnp.float32)) / GLA_TAU
    la_b = jax.nn.log_sigmoid((lr_b @ w_alpha_b + b_alpha_b).astype(jnp.float32)) / GLA_TAU
    la_f = la_f.reshape(B, T, GLA_HEADS, GLA_DK)
    la_b = la_b.reshape(B, T, GLA_HEADS, GLA_DK)
    return (aq, ak, av, gq, gk, gv, gr, la_f, la_b, ga, gg)


def windowed_attention(q, k, v, kc, vc, sink):
    B, T = q.shape[:2]
    Lc = kc.shape[1]
    nb = T // ATT_BLOCK
    nl = 3 * ATT_BLOCK
    scale = HEAD_DIM ** -0.5
    qb = q.reshape(B, nb, ATT_BLOCK, ATT_KV_HEADS, ATT_GROUP, HEAD_DIM)
    pad = ((0, 0), (ATT_BLOCK, ATT_BLOCK), (0, 0), (0, 0))
    kp = jnp.pad(k, pad).reshape(B, nb + 2, ATT_BLOCK, ATT_KV_HEADS, HEAD_DIM)
    vp = jnp.pad(v, pad).reshape(B, nb + 2, ATT_BLOCK, ATT_KV_HEADS, HEAD_DIM)
    kw = jnp.concatenate([kp[:, :-2], kp[:, 1:-1], kp[:, 2:]], axis=2)
    vw = jnp.concatenate([vp[:, :-2], vp[:, 1:-1], vp[:, 2:]], axis=2)
    qi = jnp.arange(ATT_BLOCK)[:, None]
    kj = jnp.arange(nl)[None, :]
    rel = kj - ATT_BLOCK - qi
    kpos = jnp.arange(nb)[:, None, None] * ATT_BLOCK - ATT_BLOCK + kj[None]
    mask = (jnp.abs(rel) <= WINDOW)[None] & (kpos >= 0) & (kpos < T)
    s_loc = jnp.einsum('bnqkgd,bnskd->bnkgqs', qb, kw).astype(jnp.float32) * scale
    s_loc = jnp.where(mask[None, :, None, None], s_loc, -jnp.inf)
    s_ctx = jnp.einsum('bnqkgd,bckd->bnkgqc', qb, kc).astype(jnp.float32) * scale
    s_sink = jnp.broadcast_to(sink.reshape(ATT_KV_HEADS, ATT_GROUP)[None, None, :, :, None, None].astype(jnp.float32),
                              s_loc.shape[:-1] + (1,))
    p = jax.nn.softmax(jnp.concatenate([s_loc, s_ctx, s_sink], axis=-1), axis=-1).astype(q.dtype)
    o = (jnp.einsum('bnkgqs,bnskd->bnqkgd', p[..., :nl], vw)
         + jnp.einsum('bnkgqc,bckd->bnqkgd', p[..., nl:nl + Lc], vc))
    return o.reshape(B, T, ATT_W)


def context_attention(qc, kc, vc, sink):
    B, Lc = qc.shape[:2]
    qg = qc.reshape(B, Lc, ATT_KV_HEADS, ATT_GROUP, HEAD_DIM)
    s = jnp.einsum('bqkgd,bckd->bkgqc', qg, kc).astype(jnp.float32) * (HEAD_DIM ** -0.5)
    s_sink = jnp.broadcast_to(sink.reshape(ATT_KV_HEADS, ATT_GROUP)[None, :, :, None, None].astype(jnp.float32),
                              s.shape[:-1] + (1,))
    p = jax.nn.softmax(jnp.concatenate([s, s_sink], axis=-1), axis=-1)[..., :Lc].astype(qc.dtype)
    return jnp.einsum('bkgqc,bckd->bqkgd', p, vc).reshape(B, Lc, ATT_W)


def gla_chunk_scan(q, k, v, log_a, s0):
    B, T, H, dk = q.shape
    dv = v.shape[-1]
    nc = T // GLA_CHUNK

    def chunks(a):
        return a.reshape(B, nc, GLA_CHUNK, H, a.shape[-1]).transpose(1, 0, 3, 2, 4)

    causal = jnp.tril(jnp.ones((GLA_CHUNK, GLA_CHUNK), bool))

    def step(S, inp):
        qc, kc, vc, gc = inp
        b = jnp.cumsum(gc.astype(jnp.float32), axis=2)
        o_inter = jnp.einsum('bhcd,bhde->bhce', qc * jnp.exp(b), S)
        decay = jnp.exp(jnp.where(causal[:, :, None], b[:, :, :, None, :] - b[:, :, None, :, :], -jnp.inf))
        a = jnp.einsum('bhid,bhjd,bhijd->bhij', qc, kc, decay)
        o = o_inter + jnp.einsum('bhij,bhje->bhie', a, vc)
        b_last = b[:, :, -1:, :]
        S = jnp.exp(b_last[:, :, 0, :])[..., None] * S + jnp.einsum('bhjd,bhje->bhde', kc * jnp.exp(b_last - b), vc)
        return S, o

    S, o = lax.scan(step, s0, (chunks(q), chunks(k), chunks(v), chunks(log_a)))
    o = o.transpose(1, 0, 3, 2, 4).reshape(B, T, H, dv)
    return o.astype(v.dtype), S


def bidirectional_gla(q, k, v, la_f, la_b, qc, kc, vc, la_fc, la_bc):
    B = q.shape[0]
    s0 = jnp.zeros((B, GLA_HEADS, GLA_DK, GLA_DV), jnp.float32)
    fl = lambda a: jnp.flip(a, axis=1)
    oc_f, s_f = gla_chunk_scan(qc, kc, vc, la_fc, s0)
    oc_b, s_b = gla_chunk_scan(fl(qc), fl(kc), fl(vc), fl(la_bc), s0)
    o_f, _ = gla_chunk_scan(q, k, v, la_f, s_f)
    o_b, _ = gla_chunk_scan(fl(q), fl(k), fl(v), fl(la_b), s_b)
    return o_f + fl(o_b), oc_f + fl(oc_b)


def merge_branches(attn_o, gla_o, gr, ga, gg, gla_norm, w_branch_attn, w_branch_gla, w_out):
    B, T = gr.shape[:2]
    o = rms_norm(gla_o, gla_norm) * jax.nn.silu(gr.reshape(B, T, GLA_HEADS, GLA_DV))
    y = (jax.nn.sigmoid(ga) * (attn_o @ w_branch_attn)
         + jax.nn.sigmoid(gg) * (o.reshape(B, T, GLA_V_W) @ w_branch_gla))
    return y @ w_out


def moe_ffn(h, w_router, b_router, w1, b1, w2, b2):
    shp = h.shape
    xt = h.reshape(-1, D_MODEL)
    n = xt.shape[0]
    logits = (xt @ w_router + b_router).astype(jnp.float32)
    top_v, top_i = lax.top_k(logits, TOP_K)
    wts = jax.nn.softmax(top_v, axis=-1).astype(h.dtype)
    e_flat = top_i.reshape(-1)
    order = jnp.argsort(e_flat)
    e_sorted = e_flat[order]
    tok_sorted = order // TOP_K
    w_sorted = wts.reshape(-1)[order]
    counts = jnp.bincount(e_flat, length=N_EXPERTS)
    padded = (counts + MOE_BLOCK - 1) // MOE_BLOCK * MOE_BLOCK
    pad_end = jnp.cumsum(padded)
    pad_start = pad_end - padded
    start = jnp.cumsum(counts) - counts
    dest = pad_start[e_sorted] + jnp.arange(n * TOP_K) - start[e_sorted]
    cap = (n * TOP_K + N_EXPERTS * (MOE_BLOCK - 1)) // MOE_BLOCK * MOE_BLOCK
    n_blk = cap // MOE_BLOCK
    buf_tok = jnp.zeros((cap,), jnp.int32).at[dest].set(tok_sorted.astype(jnp.int32))
    blk_e = jnp.minimum(jnp.searchsorted(pad_end, jnp.arange(n_blk) * MOE_BLOCK, side='right'), N_EXPERTS - 1)
    xs = xt[buf_tok].reshape(n_blk, MOE_BLOCK, D_MODEL)

    def expert_block(args):
        xb, e = args
        gu = xb @ w1[e] + b1[e]
        gate = jnp.minimum(gu[:, :D_FF], SWIGLU_LIMIT)
        up = jnp.clip(gu[:, D_FF:], -SWIGLU_LIMIT, SWIGLU_LIMIT)
        act = gate * jax.nn.sigmoid(SWIGLU_ALPHA * gate) * (up + 1)
        return act @ w2[e] + b2[e]

    ys = lax.map(expert_block, (xs, blk_e)).reshape(cap, D_MODEL)
    y = jnp.zeros_like(xt).at[tok_sorted].add(ys[dest] * w_sorted[:, None])
    return y.reshape(shp)


def setup_inputs(seed: int = 0) -> dict:
    key = jax.random.key(seed)
    ks = jax.random.split(key, 26)
    D = D_MODEL

    def nrm(k, shape, scale):
        return jax.random.normal(k, shape, jnp.float32) * scale

    return {
        'x': nrm(ks[0], (BATCH, SEQ, D), 1.0),
        'c': nrm(ks[1], (BATCH, D), 1.0),
        'ctx': nrm(ks[2], (BATCH, CTX_LEN, D), 1.0),
        'c_ctx': nrm(ks[3], (D,), 1.0),
        'w_mod': nrm(ks[4], (DEPTH, D, 6 * D), 0.5 * D ** -0.5),
        'b_mod': nrm(ks[5], (DEPTH, 6 * D), 0.02),
        'norm1': 1.0 + nrm(ks[6], (DEPTH, D), 0.02),
        'norm2': 1.0 + nrm(ks[7], (DEPTH, D), 0.02),
        'w_in': nrm(ks[8], (DEPTH, D, C_IN), D ** -0.5),
        'q_norm': 1.0 + nrm(ks[9], (DEPTH, HEAD_DIM), 0.02),
        'k_norm': 1.0 + nrm(ks[10], (DEPTH, HEAD_DIM), 0.02),
        'attn_sink': nrm(ks[11], (DEPTH, ATT_HEADS), 0.5),
        'w_alpha_f': nrm(ks[12], (DEPTH, GLA_RANK, GLA_K_W), GLA_RANK ** -0.5),
        'b_alpha_f': nrm(ks[13], (DEPTH, GLA_K_W), 0.1),
        'w_alpha_b': nrm(ks[14], (DEPTH, GLA_RANK, GLA_K_W), GLA_RANK ** -0.5),
        'b_alpha_b': nrm(ks[15], (DEPTH, GLA_K_W), 0.1),
        'gla_norm': 1.0 + nrm(ks[16], (DEPTH, GLA_DV), 0.02),
        'w_branch_attn': nrm(ks[17], (DEPTH, ATT_W, D), ATT_W ** -0.5),
        'w_branch_gla': nrm(ks[18], (DEPTH, GLA_V_W, D), GLA_V_W ** -0.5),
        'w_out': nrm(ks[19], (DEPTH, D, D), D ** -0.5),
        'w_router': nrm(ks[20], (DEPTH, D, N_EXPERTS), D ** -0.5),
        'b_router': nrm(ks[21], (DEPTH, N_EXPERTS), 0.01),
        'w_exp_in': nrm(ks[22], (DEPTH, N_EXPERTS, D, 2 * D_FF), D ** -0.5),
        'b_exp_in': nrm(ks[23], (DEPTH, N_EXPERTS, 2 * D_FF), 0.02),
        'w_exp_out': nrm(ks[24], (DEPTH, N_EXPERTS, D_FF, D), D_FF ** -0.5),
        'b_exp_out': nrm(ks[25], (DEPTH, N_EXPERTS, D), 0.02),
    }


def reference(x, c, ctx, c_ctx, w_mod, b_mod, norm1, norm2, w_in, q_norm, k_norm, attn_sink,
              w_alpha_f, b_alpha_f, w_alpha_b, b_alpha_b, gla_norm, w_branch_attn, w_branch_gla,
              w_out, w_router, b_router, w_exp_in, b_exp_in, w_exp_out, b_exp_out):
    ang_r, ang_c = axial_rope_angles(x.shape[1])
    for l in range(DEPTH):
        mod = (jax.nn.silu(c) @ w_mod[l] + b_mod[l])[:, None, :]
        mod_c = jax.nn.silu(c_ctx) @ w_mod[l] + b_mod[l]
        sh1, sc1, gt1, sh2, sc2, gt2 = jnp.split(mod, 6, axis=-1)
        csh1, csc1, cgt1, csh2, csc2, cgt2 = jnp.split(mod_c, 6, axis=-1)
        h = rms_norm(x, norm1[l]) * (1 + sc1) + sh1
        hc = rms_norm(ctx, norm1[l]) * (1 + csc1) + csh1
        proj_w = (w_in[l], q_norm[l], k_norm[l], w_alpha_f[l], b_alpha_f[l], w_alpha_b[l], b_alpha_b[l])
        aq, ak, av, gq, gk, gv, gr, laf, lab, ga, gg = mixer_inputs(h, *proj_w)
        caq, cak, cav, cgq, cgk, cgv, cgr, claf, clab, cga, cgg = mixer_inputs(hc, *proj_w)
        aq = axial_rope(aq, ang_r, ang_c)
        ak = axial_rope(ak, ang_r, ang_c)
        attn_o = windowed_attention(aq, ak, av, cak, cav, attn_sink[l])
        gla_o, gla_oc = bidirectional_gla(gq, gk, gv, laf, lab, cgq, cgk, cgv, claf, clab)
        merge_w = (gla_norm[l], w_branch_attn[l], w_branch_gla[l], w_out[l])
        moe_w = (w_router[l], b_router[l], w_exp_in[l], b_exp_in[l], w_exp_out[l], b_exp_out[l])
        x_next = x + gt1 * merge_branches(attn_o, gla_o, gr, ga, gg, *merge_w)
        h2 = rms_norm(x_next, norm2[l]) * (1 + sc2) + sh2
        x_next = x_next + gt2 * moe_ffn(h2, *moe_w)
        if l + 1 < DEPTH:
            attn_oc = context_attention(caq, cak, cav, attn_sink[l])
            ctx = ctx + cgt1 * merge_branches(attn_oc, gla_oc, cgr, cga, cgg, *merge_w)
            hc2 = rms_norm(ctx, norm2[l]) * (1 + csc2) + csh2
            ctx = ctx + cgt2 * moe_ffn(hc2, *moe_w)
        x = x_next
    return x
```

```python
import functools

import numpy as np
import jax
import jax.numpy as jnp
from jax import lax
from jax.experimental import pallas as pl
from jax.experimental.pallas import tpu as pltpu

F32 = jnp.float32
BF16 = jnp.bfloat16

D_MODEL = 1024
GRID_W = 64
EPS = 1e-6
ATT_HEADS = 8
ATT_KV_HEADS = 2
ATT_GROUP = ATT_HEADS // ATT_KV_HEADS
HEAD_DIM = 64
WINDOW = 128
ATT_BLOCK = 128
ROPE_BASE = 10000.0
AXIS_ROT = HEAD_DIM // 2
GLA_HEADS = 4
GLA_DK = 64
GLA_DV = 128
GLA_RANK = 16
GLA_TAU = 16.0
N_EXPERTS = 32
TOP_K = 4
D_FF = D_MODEL
SWIGLU_ALPHA = 1.702
SWIGLU_LIMIT = 7.0
MOE_BLOCK = 256

ATT_W = ATT_HEADS * HEAD_DIM
ATT_KV_W = ATT_KV_HEADS * HEAD_DIM
GLA_K_W = GLA_HEADS * GLA_DK
GLA_V_W = GLA_HEADS * GLA_DV
IN_SPLITS = (ATT_W, ATT_KV_W, ATT_KV_W, GLA_K_W, GLA_K_W, GLA_V_W, GLA_V_W, GLA_RANK, GLA_RANK, D_MODEL, D_MODEL)

LANES = 128
VMEM_LIMIT = 56 * 1024 * 1024
NEG = -1e30

GLA_C = 64
GLA_SUB = 4
GLA_LEVELS = 4


def _cparams(sem):
    return pltpu.CompilerParams(dimension_semantics=sem, vmem_limit_bytes=VMEM_LIMIT)


def _full(shape):
    n = len(shape)
    return pl.BlockSpec(shape, lambda *_: (0,) * n)


def _mod_kernel(c_ref, w_ref, b_ref, o_ref):
    c = c_ref[...]
    s = c * (1.0 / (1.0 + jnp.exp(-c)))
    o_ref[...] = jnp.dot(s, w_ref[...], preferred_element_type=F32,
                         precision=lax.Precision.HIGHEST) + b_ref[...]


def _modulation(c_all, w_mod, b_mod):
    rows = c_all.shape[0]
    n = w_mod.shape[1]
    tn = 1536
    return pl.pallas_call(
        _mod_kernel,
        out_shape=jax.ShapeDtypeStruct((rows, n), F32),
        grid=(n // tn,),
        in_specs=[pl.BlockSpec((rows, D_MODEL), lambda j: (0, 0)),
                  pl.BlockSpec((D_MODEL, tn), lambda j: (0, j)),
                  pl.BlockSpec((1, tn), lambda j: (0, j))],
        out_specs=pl.BlockSpec((rows, tn), lambda j: (0, j)),
        compiler_params=_cparams(("arbitrary",)),
        name="mod",
    )(c_all, w_mod, b_mod.reshape(1, n))


def _pair_norm(a, g, lo):
    s = a * a
    tot = jnp.sum(s, axis=-1, keepdims=True)
    slo = jnp.sum(jnp.where(lo, s, 0.0), axis=-1, keepdims=True)
    ms = jnp.where(lo, slo, tot - slo) * (1.0 / HEAD_DIM)
    return a * lax.rsqrt(ms + EPS) * g


def _rope(y, cos, sin, first):
    up = pltpu.roll(y, LANES - AXIS_ROT // 2, 1)
    dn = pltpu.roll(y, AXIS_ROT // 2, 1)
    return y * cos + jnp.where(first, up, dn) * sin


def _inproj_kernel(*refs, rope, full):
    if full:
        (x_ref, sh_ref, sc_ref, n1_ref, cos_ref, sin_ref, qn_ref, kn_ref, wal_ref, bal_ref,
         wq, wk, wv, wgq, wgk, wgv, wgr, wga, wgg, wlr,
         oq, ok, ov, ogq, ogk, ogv, ogr, oga, ogg, ola) = refs
    else:
        (x_ref, sh_ref, sc_ref, n1_ref, kn_ref, wal_ref, bal_ref,
         wk, wv, wgk, wgv, wlr,
         ok, ov, ogk, ogv, ola) = refs
    x = x_ref[...]
    tm = x.shape[0]
    ms = jnp.mean(x * x, axis=-1, keepdims=True)
    h = (x * lax.rsqrt(ms + EPS) * n1_ref[...]) * (1.0 + sc_ref[...]) + sh_ref[...]
    hb = h.astype(BF16)

    def proj(w_ref):
        return jnp.dot(hb, w_ref[...], preferred_element_type=F32)

    lane = lax.broadcasted_iota(jnp.int32, (tm, LANES), 1)
    lo = lane < HEAD_DIM
    first = (lane % AXIS_ROT) < (AXIS_ROT // 2)
    if rope:
        cos = cos_ref[...]
        sin = sin_ref[...]

    k = _pair_norm(proj(wk), kn_ref[...], lo)
    if rope:
        k = _rope(k, cos, sin, first)
    ok[...] = k.astype(BF16)
    ov[...] = proj(wv).astype(BF16)
    ogk[...] = proj(wgk).astype(BF16)
    ogv[...] = proj(wgv).astype(BF16)
    lr = proj(wlr).astype(BF16)
    z = jnp.dot(lr, wal_ref[...], preferred_element_type=F32) + bal_ref[...]
    ola[...] = (jnp.minimum(z, 0.0) - jnp.log(1.0 + jnp.exp(-jnp.abs(z)))) * (1.0 / GLA_TAU)
    if full:
        q = proj(wq)
        for p in range(ATT_W // LANES):
            y = _pair_norm(q[:, p * LANES:(p + 1) * LANES], qn_ref[...], lo)
            if rope:
                y = _rope(y, cos, sin, first)
            oq[:, p * LANES:(p + 1) * LANES] = (y * HEAD_DIM ** -0.5).astype(BF16)
        ogq[...] = (proj(wgq) * GLA_DK ** -0.5).astype(BF16)
        g = proj(wgr)
        ogr[...] = (g * (1.0 / (1.0 + jnp.exp(-g)))).astype(BF16)
        oga[...] = (1.0 / (1.0 + jnp.exp(-proj(wga)))).astype(BF16)
        ogg[...] = (1.0 / (1.0 + jnp.exp(-proj(wgg)))).astype(BF16)


def _inproj(x, sh, sc, norm1, tabs, wts, *, rope, full, tm):
    B, T, D = x.shape
    grid = (B, T // tm)
    row = lambda w: pl.BlockSpec((None, tm, w), lambda b, t: (b, t, 0))
    vec = pl.BlockSpec((None, 1, D), lambda b, t: (b, 0, 0))
    tab = pl.BlockSpec((tm, LANES), lambda b, t: (t, 0))
    if full:
        names = ("wq", "wk", "wv", "wgq", "wgk", "wgv", "wgr", "wga", "wgg", "wlr")
        ins = [x, sh, sc, norm1, tabs["cos"], tabs["sin"], wts["qn"], wts["kn"], wts["wal"], wts["bal"]]
        specs = [row(D), vec, vec, _full((1, D)), tab, tab, _full((1, LANES)), _full((1, LANES)),
                 _full(wts["wal"].shape), _full(wts["bal"].shape)]
        out_w = (ATT_W, ATT_KV_W, ATT_KV_W, GLA_K_W, GLA_K_W, GLA_V_W, GLA_V_W, D, D)
    else:
        names = ("wk", "wv", "wgk", "wgv", "wlr")
        ins = [x, sh, sc, norm1, wts["kn"], wts["wal"], wts["bal"]]
        specs = [row(D), vec, vec, _full((1, D)), _full((1, LANES)),
                 _full(wts["wal"].shape), _full(wts["bal"].shape)]
        out_w = (ATT_KV_W, ATT_KV_W, GLA_K_W, GLA_V_W)
    ins += [wts[n] for n in names]
    specs += [_full(wts[n].shape) for n in names]
    out_shape = [jax.ShapeDtypeStruct((B, T, w), BF16) for w in out_w]
    out_shape.append(jax.ShapeDtypeStruct((B, T, 2 * GLA_K_W), F32))
    out_specs = [row(w) for w in out_w] + [row(2 * GLA_K_W)]
    return pl.pallas_call(
        functools.partial(_inproj_kernel, rope=rope, full=full),
        out_shape=out_shape, grid=grid, in_specs=specs, out_specs=out_specs,
        compiler_params=_cparams(("parallel", "arbitrary")),
        name="inproj_full" if full else "inproj_ctx",
    )(*ins)


def _attn_kernel(sink_ref, q_ref, kp_ref, kc_ref, kn_ref, kx_ref, vp_ref, vc_ref, vn_ref, vx_ref,
                 o_ref, *, seq):
    n = pl.program_id(1)
    blk = ATT_BLOCK
    lc = kx_ref.shape[0]
    kcat = jnp.concatenate([kp_ref[...], kc_ref[...], kn_ref[...], kx_ref[...]], axis=0)
    vcat = jnp.concatenate([vp_ref[...], vc_ref[...], vn_ref[...], vx_ref[...]], axis=0)
    nk = 3 * blk + lc
    qi = lax.broadcasted_iota(jnp.int32, (blk, nk), 0)
    kj = lax.broadcasted_iota(jnp.int32, (blk, nk), 1)
    kpos = (n - 1) * blk + kj
    valid = (jnp.abs(kj - blk - qi) <= WINDOW) & (kpos >= 0) & (kpos < seq)
    valid = valid | (kj >= 3 * blk)
    lane = lax.broadcasted_iota(jnp.int32, (blk, LANES), 1)
    lo = lane < HEAD_DIM
    for m in range(ATT_W // LANES):
        q2 = q_ref[:, m * LANES:(m + 1) * LANES]
        halves = []
        for kv in range(ATT_KV_HEADS):
            qm = jnp.where(lo if kv == 0 else jnp.logical_not(lo), q2, jnp.zeros_like(q2))
            s = lax.dot_general(qm, kcat, (((1,), (1,)), ((), ())), preferred_element_type=F32)
            s = jnp.where(valid, s, NEG)
            snk = sink_ref[kv * ATT_GROUP + m]
            mx = jnp.maximum(jnp.max(s, axis=-1, keepdims=True), snk)
            p = jnp.exp(s - mx)
            den = jnp.sum(p, axis=-1, keepdims=True) + jnp.exp(snk - mx)
            o = jnp.dot(p.astype(BF16), vcat, preferred_element_type=F32)
            halves.append(o / den)
        o_ref[:, m * LANES:(m + 1) * LANES] = jnp.where(lo, halves[0], halves[1]).astype(BF16)


def _attention(sink, aq, ak, av, cak, cav):
    B, T, _ = aq.shape
    lc = cak.shape[1]
    nb = T // ATT_BLOCK
    blk = ATT_BLOCK
    prev = lambda b, n: (b, jnp.maximum(n - 1, 0), 0)
    cur = lambda b, n: (b, n, 0)
    nxt = lambda b, n: (b, jnp.minimum(n + 1, nb - 1), 0)
    ctx = lambda b, n: (b, 0, 0)
    kvspec = lambda f: pl.BlockSpec((None, blk, ATT_KV_W), f)
    cspec = pl.BlockSpec((None, lc, ATT_KV_W), ctx)
    return pl.pallas_call(
        functools.partial(_attn_kernel, seq=T),
        out_shape=jax.ShapeDtypeStruct((B, T, ATT_W), BF16),
        grid=(B, nb),
        in_specs=[pl.BlockSpec(memory_space=pltpu.SMEM),
                  pl.BlockSpec((None, blk, ATT_W), cur),
                  kvspec(prev), kvspec(cur), kvspec(nxt), cspec,
                  kvspec(prev), kvspec(cur), kvspec(nxt), cspec],
        out_specs=pl.BlockSpec((None, blk, ATT_W), cur),
        compiler_params=_cparams(("parallel", "arbitrary")),
        name="attn",
    )(sink, aq, ak, ak, ak, cak, av, av, av, cav)


def _gla_constants():
    C, sub, L = GLA_C, GLA_SUB, GLA_LEVELS
    i = np.arange(C)[:, None]
    t = np.arange(C)[None, :]
    tabs = [t <= i, t > i]
    rowq, same = [], []
    for l in range(L):
        s = C >> l
        mid = (i // s) * s + s // 2
        isq = i >= mid
        tabs.append(np.where(isq, (t >= mid) & (t <= i), (t > i) & (t < mid)))
        rowq.append(np.broadcast_to(isq, (C, C)))
        same.append((i // s) == (t // s))
    shifts, dmask = [], [t == i]
    for d in range(1, sub):
        ok = (i % sub) >= d
        tabs.append(ok & (t > i - d) & (t <= i))
        shifts.append(ok & (t == i - d))
        dmask.append(ok & (t == i - d))
    tabs.append(np.ones((8, C), bool))
    flip = lambda a: a[::-1, ::-1]
    tile = lambda a: np.tile(a, (1, GLA_HEADS))

    def both(xs, lanes):
        f = (lambda a: tile(a)) if lanes else (lambda a: a)
        return np.stack([np.concatenate([f(a) for a in xs], 0),
                         np.concatenate([f(flip(a)) for a in xs], 0)]).astype(np.float32)

    hk = np.arange(GLA_K_W) // GLA_DK
    hv = np.arange(GLA_V_W) // GLA_DV
    ind = (hk[:, None] == hk[None, :]).astype(np.float32)
    bdv = (hk[:, None] == hv[None, :]).astype(np.float32)
    return (both(tabs, False), both(shifts, False), both(rowq, True), both(same, True),
            both(dmask, True), ind, bdv, np.ascontiguousarray(bdv.T))


def _gla_kernel(q_ref, k_ref, v_ref, la_ref, ck_ref, cv_ref, cla_ref,
                mtab_ref, shm_ref, rq_ref, sm_ref, dm_ref, ind_ref, bdv_ref, bds_ref,
                o_ref, st_ref, *, n_ctx):
    s = pl.program_id(2)
    C = GLA_C
    kw = GLA_K_W

    @pl.when(s == 0)
    def _():
        st_ref[...] = jnp.zeros_like(st_ref)

    is_ctx = s < n_ctx
    q = q_ref[...].astype(F32)
    k_b = jnp.where(is_ctx, ck_ref[...], k_ref[...])
    v_b = jnp.where(is_ctx, cv_ref[...], v_ref[...])
    la = jnp.where(is_ctx, cla_ref[...], la_ref[...])
    k = k_b.astype(F32)

    hi = la.astype(BF16)
    r1 = la - hi.astype(F32)
    mid = r1.astype(BF16)
    lo = (r1 - mid.astype(F32)).astype(BF16)
    g3 = jnp.dot(mtab_ref[...], jnp.concatenate([hi, mid, lo], axis=1), preferred_element_type=F32)
    e = jnp.exp(g3[:, :kw] + g3[:, kw:2 * kw] + g3[:, 2 * kw:])

    qt = (q * e[0:C]).astype(BF16)
    kt = (k * e[C:2 * C]).astype(BF16)
    nt = 2 + GLA_LEVELS + GLA_SUB - 1
    gamma = e[nt * C:nt * C + 1]

    ind = ind_ref[...] > 0.5
    zero_k = jnp.zeros((GLA_HEADS * C, kw), BF16)
    a = jnp.zeros((C, kw), F32)
    for l in range(GLA_LEVELS):
        el = e[(2 + l) * C:(3 + l) * C]
        rq = rq_ref[l * C:(l + 1) * C, :]
        qh = (q * (el * rq)).astype(BF16)
        kh = (k * (el * (1.0 - rq))).astype(BF16)
        bdk = jnp.where(ind, jnp.concatenate([kh] * GLA_HEADS, axis=0), zero_k)
        al = lax.dot_general(qh, bdk, (((1,), (1,)), ((), ())), preferred_element_type=F32)
        a = a + al * sm_ref[l * C:(l + 1) * C, :]
    ksh = jnp.dot(shm_ref[...], k_b, preferred_element_type=F32)
    indb = ind_ref[...].astype(BF16)
    for d in range(GLA_SUB):
        if d == 0:
            p = q * k
        else:
            p = q * ksh[(d - 1) * C:d * C] * e[(2 + GLA_LEVELS + d - 1) * C:(2 + GLA_LEVELS + d) * C]
        w = jnp.dot(p.astype(BF16), indb, preferred_element_type=F32)
        a = a + w * dm_ref[d * C:(d + 1) * C, :]

    st = st_ref[...]
    bdv = jnp.where(bdv_ref[...] > 0.5, jnp.concatenate([v_b] * GLA_HEADS, axis=0),
                    jnp.zeros((GLA_HEADS * C, GLA_V_W), BF16))
    o = jnp.dot(a.astype(BF16), bdv, preferred_element_type=F32)
    o = o + lax.dot_general(qt, st.astype(BF16), (((1,), (1,)), ((), ())), preferred_element_type=F32)
    o_ref[...] = o
    upd = lax.dot_general(v_b, kt, (((0,), (0,)), ((), ())), preferred_element_type=F32)
    st_ref[...] = st * gamma + jnp.where(bds_ref[...] > 0.5, upd, 0.0)


def _gla(gq, gk, gv, la, cgk, cgv, cla):
    B, T, _ = gq.shape
    lc = cgk.shape[1]
    C = GLA_C
    n_ctx, n_lat = lc // C, T // C
    consts = [jnp.asarray(c) for c in _gla_constants()]
    mtab, shm = consts[0].astype(BF16), consts[1].astype(BF16)
    rq, sm, dm, ind, bdv, bds = consts[2:]

    def lat(b, d, s):
        j = jnp.maximum(s - n_ctx, 0)
        return jnp.where(d == 0, j, n_lat - 1 - j)

    def ctx(b, d, s):
        j = jnp.minimum(s, n_ctx - 1)
        return jnp.where(d == 0, j, n_ctx - 1 - j)

    lspec = lambda w: pl.BlockSpec((None, C, w), lambda b, d, s: (b, lat(b, d, s), 0))
    cspec = lambda w: pl.BlockSpec((None, C, w), lambda b, d, s: (b, ctx(b, d, s), 0))
    dirc = lambda a: pl.BlockSpec((None,) + a.shape[1:], lambda b, d, s: (d,) + (0,) * (a.ndim - 1))
    return pl.pallas_call(
        functools.partial(_gla_kernel, n_ctx=n_ctx),
        out_shape=jax.ShapeDtypeStruct((2, B, T, GLA_V_W), F32),
        grid=(B, 2, n_ctx + n_lat),
        in_specs=[lspec(GLA_K_W), lspec(GLA_K_W), lspec(GLA_V_W),
                  pl.BlockSpec((None, C, GLA_K_W), lambda b, d, s: (b, lat(b, d, s), d)),
                  cspec(GLA_K_W), cspec(GLA_V_W),
                  pl.BlockSpec((None, C, GLA_K_W), lambda b, d, s: (b, ctx(b, d, s), d)),
                  dirc(mtab), dirc(shm), dirc(rq), dirc(sm), dirc(dm), _full(ind.shape), _full(bdv.shape),
                  _full(bds.shape)],
        out_specs=pl.BlockSpec((None, None, C, GLA_V_W), lambda b, d, s: (d, b, lat(b, d, s), 0)),
        scratch_shapes=[pltpu.VMEM((GLA_V_W, GLA_K_W), F32)],
        compiler_params=_cparams(("parallel", "arbitrary", "arbitrary")),
        name="gla",
    )(gq, gk, gv, la, cgk, cgv, cla, mtab, shm, rq, sm, dm, ind, bdv, bds)


def _merge_kernel(x_ref, at_ref, of_ref, ob_ref, gr_ref, ga_ref, gg_ref, gt1_ref, sc2_ref, sh2_ref,
                  n2_ref, gn_ref, wba_ref, wbg_ref, wo_ref, wrh_ref, wrl_ref, br_ref,
                  xn_ref, h2_ref, ti_ref, tw_ref):
    tm = x_ref.shape[0]
    go = of_ref[...] + ob_ref[...]
    parts = []
    for h in range(GLA_HEADS):
        gh = go[:, h * GLA_DV:(h + 1) * GLA_DV]
        ms = jnp.mean(gh * gh, axis=-1, keepdims=True)
        parts.append(gh * lax.rsqrt(ms + EPS))
    o = jnp.concatenate(parts, axis=1) * gn_ref[...] * gr_ref[...].astype(F32)
    ya = jnp.dot(at_ref[...], wba_ref[...], preferred_element_type=F32)
    yg = jnp.dot(o.astype(BF16), wbg_ref[...], preferred_element_type=F32)
    y = ga_ref[...].astype(F32) * ya + gg_ref[...].astype(F32) * yg
    z = jnp.dot(y.astype(BF16), wo_ref[...], preferred_element_type=F32)
    xn = x_ref[...] + gt1_ref[...] * z
    xn_ref[...] = xn
    ms = jnp.mean(xn * xn, axis=-1, keepdims=True)
    h2 = (xn * lax.rsqrt(ms + EPS) * n2_ref[...]) * (1.0 + sc2_ref[...]) + sh2_ref[...]
    hh = h2.astype(BF16)
    hl = (h2 - hh.astype(F32)).astype(BF16)
    h2_ref[...] = hh
    logits = (jnp.dot(hh, wrh_ref[...], preferred_element_type=F32)
              + jnp.dot(hl, wrh_ref[...], preferred_element_type=F32)
              + jnp.dot(hh, wrl_ref[...], preferred_element_type=F32)) + br_ref[...]
    lane = lax.broadcasted_iota(jnp.int32, (tm, LANES), 1).astype(F32)
    vals, idxs = [], []
    l = logits
    for _ in range(TOP_K):
        m = jnp.max(l, axis=-1, keepdims=True)
        ix = jnp.min(jnp.where(l == m, lane, float(LANES)), axis=-1, keepdims=True)
        vals.append(m)
        idxs.append(ix)
        l = jnp.where(lane == ix, -3.0e38, l)
    ex = [jnp.exp(v - vals[0]) for v in vals]
    den = ex[0] + ex[1] + ex[2] + ex[3]
    ti = jnp.zeros((tm, LANES), F32)
    tw = jnp.zeros((tm, LANES), F32)
    for j in range(TOP_K):
        ti = jnp.where(lane == float(j), idxs[j], ti)
        tw = jnp.where(lane == float(j), ex[j] / den, tw)
    ti_ref[...] = ti.astype(jnp.int32)
    tw_ref[...] = tw


def _merge(x, attn_o, gla_o, gr, ga, gg, gt1, sc2, sh2, norm2, wts, *, tm):
    B, T, D = x.shape
    row = lambda w: pl.BlockSpec((None, tm, w), lambda b, t: (b, t, 0))
    vec = pl.BlockSpec((None, 1, D), lambda b, t: (b, 0, 0))
    names = ("wba", "wbg", "wo", "wrh", "wrl", "br")
    return pl.pallas_call(
        _merge_kernel,
        out_shape=[jax.ShapeDtypeStruct((B, T, D), F32), jax.ShapeDtypeStruct((B, T, D), BF16),
                   jax.ShapeDtypeStruct((B, T, LANES), jnp.int32), jax.ShapeDtypeStruct((B, T, LANES), F32)],
        grid=(B, T // tm),
        in_specs=[row(D), row(ATT_W),
                  pl.BlockSpec((None, None, tm, GLA_V_W), lambda b, t: (0, b, t, 0)),
                  pl.BlockSpec((None, None, tm, GLA_V_W), lambda b, t: (1, b, t, 0)),
                  row(GLA_V_W), row(D), row(D), vec, vec, vec,
                  _full((1, D)), _full((1, GLA_V_W))] + [_full(wts[n].shape) for n in names],
        out_specs=[row(D), row(D), row(LANES), row(LANES)],
        compiler_params=_cparams(("parallel", "arbitrary")),
        name="merge",
    )(x, attn_o, gla_o, gla_o, gr, ga, gg, gt1, sc2, sh2, norm2, wts["gn"], *[wts[n] for n in names])


def _expert_kernel(be_ref, nu_ref, xs_ref, w1_ref, b1_ref, w2_ref, b2_ref, o_ref, w1b, w2b):
    i = pl.program_id(0)
    e = be_ref[i]
    prev = be_ref[jnp.maximum(i - 1, 0)]

    @pl.when((i == 0) | (e != prev))
    def _():
        w1b[...] = w1_ref[...].astype(BF16)
        w2b[...] = w2_ref[...].astype(BF16)

    @pl.when(i < nu_ref[0])
    def _():
        gu = jnp.dot(xs_ref[...], w1b[...], preferred_element_type=F32) + b1_ref[...]
        gate = jnp.minimum(gu[:, :D_FF], SWIGLU_LIMIT)
        up = jnp.clip(gu[:, D_FF:], -SWIGLU_LIMIT, SWIGLU_LIMIT)
        act = gate * (1.0 / (1.0 + jnp.exp(-SWIGLU_ALPHA * gate))) * (up + 1.0)
        o_ref[...] = jnp.dot(act.astype(BF16), w2b[...], preferred_element_type=F32) + b2_ref[...]


def _experts(blk_e, n_used, xs, w1, b1, w2, b2):
    cap = xs.shape[0]
    n_blk = cap // MOE_BLOCK
    ne = w1.shape[0]
    gs = pltpu.PrefetchScalarGridSpec(
        num_scalar_prefetch=2, grid=(n_blk,),
        in_specs=[pl.BlockSpec((MOE_BLOCK, D_MODEL), lambda i, be, nu: (i, 0)),
                  pl.BlockSpec((None, D_MODEL, 2 * D_FF), lambda i, be, nu: (be[i], 0, 0)),
                  pl.BlockSpec((None, 1, 2 * D_FF), lambda i, be, nu: (be[i], 0, 0)),
                  pl.BlockSpec((None, D_FF, D_MODEL), lambda i, be, nu: (be[i], 0, 0)),
                  pl.BlockSpec((None, 1, D_MODEL), lambda i, be, nu: (be[i], 0, 0))],
        out_specs=pl.BlockSpec((MOE_BLOCK, D_MODEL), lambda i, be, nu: (i, 0)),
        scratch_shapes=[pltpu.VMEM((D_MODEL, 2 * D_FF), BF16), pltpu.VMEM((D_FF, D_MODEL), BF16)])
    return pl.pallas_call(
        _expert_kernel, grid_spec=gs,
        out_shape=jax.ShapeDtypeStruct((cap, D_MODEL), F32),
        compiler_params=_cparams(("arbitrary",)),
        name="experts",
    )(blk_e, n_used, xs, w1, b1.reshape(ne, 1, 2 * D_FF), w2, b2.reshape(ne, 1, D_MODEL))


def _combine_kernel(xn_ref, g0, g1, g2, g3, tw_ref, gt2_ref, o_ref):
    tw = tw_ref[...]
    y = (g0[...] * tw[:, 0:1] + g1[...] * tw[:, 1:2]) + (g2[...] * tw[:, 2:3] + g3[...] * tw[:, 3:4])
    o_ref[...] = xn_ref[...] + gt2_ref[...] * y


def _combine(xn, gs, tw, gt2, *, tm):
    B, T, D = xn.shape
    row = lambda w: pl.BlockSpec((None, tm, w), lambda b, t: (b, t, 0))
    return pl.pallas_call(
        _combine_kernel,
        out_shape=jax.ShapeDtypeStruct((B, T, D), F32),
        grid=(B, T // tm),
        in_specs=[row(D)] + [row(D)] * TOP_K + [row(LANES), pl.BlockSpec((None, 1, D), lambda b, t: (b, 0, 0))],
        out_specs=row(D),
        compiler_params=_cparams(("parallel", "arbitrary")),
        name="combine",
    )(xn, *gs, tw, gt2)


def _rope_tables(T):
    rows = T // GRID_W
    row = jnp.repeat(jnp.arange(rows, dtype=F32), GRID_W)
    col = jnp.tile(jnp.arange(GRID_W, dtype=F32), rows)
    inv = ROPE_BASE ** (-jnp.arange(0, AXIS_ROT, 2, dtype=F32) / AXIS_ROT)
    ang_r, ang_c = row[:, None] * inv, col[:, None] * inv
    m = AXIS_ROT // 2
    ang = jnp.concatenate([ang_r, ang_r, ang_c, ang_c], axis=1)
    sign = jnp.tile(jnp.concatenate([-jnp.ones((m,), F32), jnp.ones((m,), F32)]), 2)
    cos = jnp.tile(jnp.cos(ang), (1, LANES // HEAD_DIM))
    sin = jnp.tile(jnp.sin(ang) * sign, (1, LANES // HEAD_DIM))
    return cos, sin


def _head_perm():
    order = []
    for m in range(ATT_GROUP):
        for kv in range(ATT_KV_HEADS):
            h = kv * ATT_GROUP + m
            order.extend(range(h * HEAD_DIM, (h + 1) * HEAD_DIM))
    return np.asarray(order)


def kernel(x, c, ctx, c_ctx, w_mod, b_mod, norm1, norm2, w_in, q_norm, k_norm, attn_sink,
           w_alpha_f, b_alpha_f, w_alpha_b, b_alpha_b, gla_norm, w_branch_attn, w_branch_gla,
           w_out, w_router, b_router, w_exp_in, b_exp_in, w_exp_out, b_exp_out):
    B, T, D = x.shape
    depth = w_mod.shape[0]
    assert depth == 1, "single-layer kernel: the context stream update only feeds later layers"
    l = 0
    perm = _head_perm()

    rows = ((B + 1 + 7) // 8) * 8
    c_all = jnp.zeros((rows, D), F32).at[:B].set(c).at[B].set(c_ctx)
    mod = _modulation(c_all, w_mod[l], b_mod[l])
    sh1, sc1, gt1, sh2, sc2, gt2 = [mod[:B, j * D:(j + 1) * D].reshape(B, 1, D) for j in range(6)]
    csh1, csc1 = [jnp.broadcast_to(mod[B, j * D:(j + 1) * D].reshape(1, 1, D), (B, 1, D)) for j in range(2)]

    offs = np.concatenate([[0], np.cumsum(IN_SPLITS)])
    cols = lambda j: w_in[l][:, offs[j]:offs[j + 1]]
    wal = jnp.zeros((2 * GLA_RANK, 2 * GLA_K_W), F32)
    wal = wal.at[:GLA_RANK, :GLA_K_W].set(w_alpha_f[l]).at[GLA_RANK:, GLA_K_W:].set(w_alpha_b[l])
    pw = {
        "wq": cols(0)[:, perm].astype(BF16), "wk": cols(1).astype(BF16), "wv": cols(2).astype(BF16),
        "wgq": cols(3).astype(BF16), "wgk": cols(4).astype(BF16), "wgv": cols(5).astype(BF16),
        "wgr": cols(6).astype(BF16), "wga": cols(9).astype(BF16), "wgg": cols(10).astype(BF16),
        "wlr": jnp.concatenate([cols(7), cols(8)], axis=1).astype(BF16),
        "qn": jnp.tile(q_norm[l], LANES // HEAD_DIM).reshape(1, LANES),
        "kn": jnp.tile(k_norm[l], LANES // HEAD_DIM).reshape(1, LANES),
        "wal": wal.astype(BF16),
        "bal": jnp.concatenate([b_alpha_f[l], b_alpha_b[l]]).reshape(1, 2 * GLA_K_W),
    }
    cos, sin = _rope_tables(T)
    n1 = norm1[l].reshape(1, D)
    tm = min(256, T)
    aq, ak, av, gq, gk, gv, gr, ga, gg, la = _inproj(
        x, sh1, sc1, n1, {"cos": cos, "sin": sin}, pw, rope=True, full=True, tm=tm)
    cak, cav, cgk, cgv, cla = _inproj(
        ctx, csh1, csc1, n1, None, pw, rope=False, full=False, tm=min(256, ctx.shape[1]))

    attn_o = _attention(attn_sink[l], aq, ak, av, cak, cav)
    gla_o = _gla(gq, gk, gv, la, cgk, cgv, cla)

    wr = jnp.zeros((D, LANES), F32).at[:, :N_EXPERTS].set(w_router[l])
    wrh = wr.astype(BF16)
    mw = {
        "gn": jnp.tile(gla_norm[l], GLA_HEADS).reshape(1, GLA_V_W),
        "wba": w_branch_attn[l][perm, :].astype(BF16), "wbg": w_branch_gla[l].astype(BF16),
        "wo": w_out[l].astype(BF16), "wrh": wrh, "wrl": (wr - wrh.astype(F32)).astype(BF16),
        "br": jnp.full((1, LANES), NEG, F32).at[0, :N_EXPERTS].set(b_router[l]),
    }
    xn, h2, ti, tw = _merge(x, attn_o, gla_o, gr, ga, gg, gt1, sc2, sh2, norm2[l].reshape(1, D), mw, tm=tm)

    n = B * T
    e_flat = ti[:, :, :TOP_K].reshape(n * TOP_K)
    onehot = (e_flat[:, None] == jnp.arange(N_EXPERTS, dtype=jnp.int32)[None, :]).astype(jnp.int32)
    csum = jnp.cumsum(onehot, axis=0)
    counts = csum[-1]
    rank = jnp.take_along_axis(csum, e_flat[:, None], axis=1)[:, 0] - 1
    padded = (counts + MOE_BLOCK - 1) // MOE_BLOCK * MOE_BLOCK
    pad_end = jnp.cumsum(padded)
    pad_start = pad_end - padded
    dest = pad_start[e_flat] + rank
    cap = (n * TOP_K + N_EXPERTS * (MOE_BLOCK - 1)) // MOE_BLOCK * MOE_BLOCK
    n_blk = cap // MOE_BLOCK
    buf_tok = jnp.zeros((cap,), jnp.int32).at[dest].set(jnp.arange(n * TOP_K, dtype=jnp.int32) // TOP_K)
    blk_e = jnp.minimum(jnp.searchsorted(pad_end, jnp.arange(n_blk, dtype=jnp.int32) * MOE_BLOCK, side='right'),
                        N_EXPERTS - 1).astype(jnp.int32)
    n_used = (pad_end[-1] // MOE_BLOCK).astype(jnp.int32).reshape(1)

    xs = jnp.take(h2.reshape(n, D), buf_tok, axis=0)
    ys = _experts(blk_e, n_used, xs, w_exp_in[l], b_exp_in[l], w_exp_out[l], b_exp_out[l])
    dest2 = dest.reshape(n, TOP_K)
    gs = [jnp.take(ys, dest2[:, j], axis=0).reshape(B, T, D) for j in range(TOP_K)]
    return _combine(xn, gs, tw, gt2, tm=tm)
```

```python
import functools

import numpy as np
import jax
import jax.numpy as jnp
from jax import lax
from jax.experimental import pallas as pl
from jax.experimental.pallas import tpu as pltpu

F32 = jnp.float32
BF16 = jnp.bfloat16

D_MODEL = 1024
GRID_W = 64
EPS = 1e-6
ATT_HEADS = 8
ATT_KV_HEADS = 2
ATT_GROUP = ATT_HEADS // ATT_KV_HEADS
HEAD_DIM = 64
WINDOW = 128
ATT_BLOCK = 128
ROPE_BASE = 10000.0
AXIS_ROT = HEAD_DIM // 2
GLA_HEADS = 4
GLA_DK = 64
GLA_DV = 128
GLA_RANK = 16
GLA_TAU = 16.0
N_EXPERTS = 32
TOP_K = 4
D_FF = D_MODEL
SWIGLU_ALPHA = 1.702
SWIGLU_LIMIT = 7.0
MOE_BLOCK = 256

ATT_W = ATT_HEADS * HEAD_DIM
ATT_KV_W = ATT_KV_HEADS * HEAD_DIM
GLA_K_W = GLA_HEADS * GLA_DK
GLA_V_W = GLA_HEADS * GLA_DV
IN_SPLITS = (ATT_W, ATT_KV_W, ATT_KV_W, GLA_K_W, GLA_K_W, GLA_V_W, GLA_V_W, GLA_RANK, GLA_RANK, D_MODEL, D_MODEL)

LANES = 128
VMEM_LIMIT = 56 * 1024 * 1024
NEG = -1e30

GLA_C = 64
GLA_SUB = 4
GLA_LEVELS = 4


def _cparams(sem):
    return pltpu.CompilerParams(dimension_semantics=sem, vmem_limit_bytes=VMEM_LIMIT)


def _full(shape):
    n = len(shape)
    return pl.BlockSpec(shape, lambda *_: (0,) * n)


def _mod_kernel(c_ref, w_ref, b_ref, o_ref):
    c = c_ref[...]
    s = c * (1.0 / (1.0 + jnp.exp(-c)))
    o_ref[...] = jnp.dot(s, w_ref[...], preferred_element_type=F32,
                         precision=lax.Precision.HIGHEST) + b_ref[...]


def _modulation(c_all, w_mod, b_mod):
    rows = c_all.shape[0]
    n = w_mod.shape[1]
    tn = 1536
    return pl.pallas_call(
        _mod_kernel,
        out_shape=jax.ShapeDtypeStruct((rows, n), F32),
        grid=(n // tn,),
        in_specs=[pl.BlockSpec((rows, D_MODEL), lambda j: (0, 0)),
                  pl.BlockSpec((D_MODEL, tn), lambda j: (0, j)),
                  pl.BlockSpec((1, tn), lambda j: (0, j))],
        out_specs=pl.BlockSpec((rows, tn), lambda j: (0, j)),
        compiler_params=_cparams(("arbitrary",)),
        name="mod",
    )(c_all, w_mod, b_mod.reshape(1, n))


def _pair_norm(a, g, lo):
    s = a * a
    tot = jnp.sum(s, axis=-1, keepdims=True)
    slo = jnp.sum(jnp.where(lo, s, 0.0), axis=-1, keepdims=True)
    ms = jnp.where(lo, slo, tot - slo) * (1.0 / HEAD_DIM)
    return a * lax.rsqrt(ms + EPS) * g


def _rope(y, cos, sin, first):
    up = pltpu.roll(y, LANES - AXIS_ROT // 2, 1)
    dn = pltpu.roll(y, AXIS_ROT // 2, 1)
    return y * cos + jnp.where(first, up, dn) * sin


def _inproj_kernel(*refs, rope, full):
    if full:
        (x_ref, sh_ref, sc_ref, n1_ref, cos_ref, sin_ref, qn_ref, kn_ref, wal_ref, bal_ref,
         wq, wk, wv, wgq, wgk, wgv, wgr, wga, wgg, wlr,
         oq, ok, ov, ogq, ogk, ogv, ogr, oga, ogg, ola) = refs
    else:
        (x_ref, sh_ref, sc_ref, n1_ref, kn_ref, wal_ref, bal_ref,
         wk, wv, wgk, wgv, wlr,
         ok, ov, ogk, ogv, ola) = refs
    x = x_ref[...]
    tm = x.shape[0]
    ms = jnp.mean(x * x, axis=-1, keepdims=True)
    h = (x * lax.rsqrt(ms + EPS) * n1_ref[...]) * (1.0 + sc_ref[...]) + sh_ref[...]
    hb = h.astype(BF16)

    def proj(w_ref):
        return jnp.dot(hb, w_ref[...], preferred_element_type=F32)

    lane = lax.broadcasted_iota(jnp.int32, (tm, LANES), 1)
    lo = lane < HEAD_DIM
    first = (lane % AXIS_ROT) < (AXIS_ROT // 2)
    if rope:
        cos = cos_ref[...]
        sin = sin_ref[...]

    k = _pair_norm(proj(wk), kn_ref[...], lo)
    if rope:
        k = _rope(k, cos, sin, first)
    ok[...] = k.astype(BF16)
    ov[...] = proj(wv).astype(BF16)
    ogk[...] = proj(wgk).astype(BF16)
    ogv[...] = proj(wgv).astype(BF16)
    lr = proj(wlr).astype(BF16)
    z = jnp.dot(lr, wal_ref[...], preferred_element_type=F32) + bal_ref[...]
    ola[...] = (jnp.minimum(z, 0.0) - jnp.log(1.0 + jnp.exp(-jnp.abs(z)))) * (1.0 / GLA_TAU)
    if full:
        q = proj(wq)
        for p in range(ATT_W // LANES):
            y = _pair_norm(q[:, p * LANES:(p + 1) * LANES], qn_ref[...], lo)
            if rope:
                y = _rope(y, cos, sin, first)
            oq[:, p * LANES:(p + 1) * LANES] = (y * HEAD_DIM ** -0.5).astype(BF16)
        ogq[...] = (proj(wgq) * GLA_DK ** -0.5).astype(BF16)
        g = proj(wgr)
        ogr[...] = (g * (1.0 / (1.0 + jnp.exp(-g)))).astype(BF16)
        oga[...] = (1.0 / (1.0 + jnp.exp(-proj(wga)))).astype(BF16)
        ogg[...] = (1.0 / (1.0 + jnp.exp(-proj(wgg)))).astype(BF16)


def _inproj(x, sh, sc, norm1, tabs, wts, *, rope, full, tm):
    B, T, D = x.shape
    grid = (B, T // tm)
    row = lambda w: pl.BlockSpec((None, tm, w), lambda b, t: (b, t, 0))
    vec = pl.BlockSpec((None, 1, D), lambda b, t: (b, 0, 0))
    tab = pl.BlockSpec((tm, LANES), lambda b, t: (t, 0))
    if full:
        names = ("wq", "wk", "wv", "wgq", "wgk", "wgv", "wgr", "wga", "wgg", "wlr")
        ins = [x, sh, sc, norm1, tabs["cos"], tabs["sin"], wts["qn"], wts["kn"], wts["wal"], wts["bal"]]
        specs = [row(D), vec, vec, _full((1, D)), tab, tab, _full((1, LANES)), _full((1, LANES)),
                 _full(wts["wal"].shape), _full(wts["bal"].shape)]
        out_w = (ATT_W, ATT_KV_W, ATT_KV_W, GLA_K_W, GLA_K_W, GLA_V_W, GLA_V_W, D, D)
    else:
        names = ("wk", "wv", "wgk", "wgv", "wlr")
        ins = [x, sh, sc, norm1, wts["kn"], wts["wal"], wts["bal"]]
        specs = [row(D), vec, vec, _full((1, D)), _full((1, LANES)),
                 _full(wts["wal"].shape), _full(wts["bal"].shape)]
        out_w = (ATT_KV_W, ATT_KV_W, GLA_K_W, GLA_V_W)
    ins += [wts[n] for n in names]
    specs += [_full(wts[n].shape) for n in names]
    out_shape = [jax.ShapeDtypeStruct((B, T, w), BF16) for w in out_w]
    out_shape.append(jax.ShapeDtypeStruct((B, T, 2 * GLA_K_W), F32))
    out_specs = [row(w) for w in out_w] + [row(2 * GLA_K_W)]
    return pl.pallas_call(
        functools.partial(_inproj_kernel, rope=rope, full=full),
        out_shape=out_shape, grid=grid, in_specs=specs, out_specs=out_specs,
        compiler_params=_cparams(("parallel", "arbitrary")),
        name="inproj_full" if full else "inproj_ctx",
    )(*ins)


def _attn_kernel(sink_ref, q_ref, kp_ref, kc_ref, kn_ref, kx_ref, vp_ref, vc_ref, vn_ref, vx_ref,
                 o_ref, *, seq):
    n = pl.program_id(1)
    blk = ATT_BLOCK
    lc = kx_ref.shape[0]
    kcat = jnp.concatenate([kp_ref[...], kc_ref[...], kn_ref[...], kx_ref[...]], axis=0)
    vcat = jnp.concatenate([vp_ref[...], vc_ref[...], vn_ref[...], vx_ref[...]], axis=0)
    nk = 3 * blk + lc
    qi = lax.broadcasted_iota(jnp.int32, (blk, nk), 0)
    kj = lax.broadcasted_iota(jnp.int32, (blk, nk), 1)
    kpos = (n - 1) * blk + kj
    valid = (jnp.abs(kj - blk - qi) <= WINDOW) & (kpos >= 0) & (kpos < seq)
    valid = valid | (kj >= 3 * blk)
    lane = lax.broadcasted_iota(jnp.int32, (blk, LANES), 1)
    lo = lane < HEAD_DIM
    for m in range(ATT_W // LANES):
        q2 = q_ref[:, m * LANES:(m + 1) * LANES]
        halves = []
        for kv in range(ATT_KV_HEADS):
            qm = jnp.where(lo if kv == 0 else jnp.logical_not(lo), q2, jnp.zeros_like(q2))
            s = lax.dot_general(qm, kcat, (((1,), (1,)), ((), ())), preferred_element_type=F32)
            s = jnp.where(valid, s, NEG)
            snk = sink_ref[kv * ATT_GROUP + m]
            mx = jnp.maximum(jnp.max(s, axis=-1, keepdims=True), snk)
            p = jnp.exp(s - mx)
            den = jnp.sum(p, axis=-1, keepdims=True) + jnp.exp(snk - mx)
            o = jnp.dot(p.astype(BF16), vcat, preferred_element_type=F32)
            halves.append(o / den)
        o_ref[:, m * LANES:(m + 1) * LANES] = jnp.where(lo, halves[0], halves[1]).astype(BF16)


def _attention(sink, aq, ak, av, cak, cav):
    B, T, _ = aq.shape
    lc = cak.shape[1]
    nb = T // ATT_BLOCK
    blk = ATT_BLOCK
    prev = lambda b, n: (b, jnp.maximum(n - 1, 0), 0)
    cur = lambda b, n: (b, n, 0)
    nxt = lambda b, n: (b, jnp.minimum(n + 1, nb - 1), 0)
    ctx = lambda b, n: (b, 0, 0)
    kvspec = lambda f: pl.BlockSpec((None, blk, ATT_KV_W), f)
    cspec = pl.BlockSpec((None, lc, ATT_KV_W), ctx)
    return pl.pallas_call(
        functools.partial(_attn_kernel, seq=T),
        out_shape=jax.ShapeDtypeStruct((B, T, ATT_W), BF16),
        grid=(B, nb),
        in_specs=[pl.BlockSpec(memory_space=pltpu.SMEM),
                  pl.BlockSpec((None, blk, ATT_W), cur),
                  kvspec(prev), kvspec(cur), kvspec(nxt), cspec,
                  kvspec(prev), kvspec(cur), kvspec(nxt), cspec],
        out_specs=pl.BlockSpec((None, blk, ATT_W), cur),
        compiler_params=_cparams(("parallel", "arbitrary")),
        name="attn",
    )(sink, aq, ak, ak, ak, cak, av, av, av, cav)


def _gla_constants():
    C, sub, L = GLA_C, GLA_SUB, GLA_LEVELS
    i = np.arange(C)[:, None]
    t = np.arange(C)[None, :]
    tabs = [t <= i, t > i]
    rowq, same = [], []
    for l in range(L):
        s = C >> l
        mid = (i // s) * s + s // 2
        isq = i >= mid
        tabs.append(np.where(isq, (t >= mid) & (t <= i), (t > i) & (t < mid)))
        rowq.append(np.broadcast_to(isq, (C, C)))
        same.append((i // s) == (t // s))
    shifts, dmask = [], [t == i]
    for d in range(1, sub):
        ok = (i % sub) >= d
        tabs.append(ok & (t > i - d) & (t <= i))
        shifts.append(ok & (t == i - d))
        dmask.append(ok & (t == i - d))
    tabs.append(np.ones((8, C), bool))
    flip = lambda a: a[::-1, ::-1]
    tile = lambda a: np.tile(a, (1, GLA_HEADS))

    def both(xs, lanes):
        f = (lambda a: tile(a)) if lanes else (lambda a: a)
        return np.stack([np.concatenate([f(a) for a in xs], 0),
                         np.concatenate([f(flip(a)) for a in xs], 0)]).astype(np.float32)

    hk = np.arange(GLA_K_W) // GLA_DK
    hv = np.arange(GLA_V_W) // GLA_DV
    ind = (hk[:, None] == hk[None, :]).astype(np.float32)
    bdv = (hk[:, None] == hv[None, :]).astype(np.float32)
    return (both(tabs, False), both(shifts, False), both(rowq, True), both(same, True),
            both(dmask, True), ind, bdv, np.ascontiguousarray(bdv.T))


def _gla_chunk(q_b, k_b, v_b, la, cst, d):
    mtab_ref, shm_ref, rq_ref, sm_ref, dm_ref, ind_ref, bdv_ref, bds_ref = cst
    C = GLA_C
    kw = GLA_K_W
    q = q_b.astype(F32)
    k = k_b.astype(F32)
    hi = la.astype(BF16)
    r1 = la - hi.astype(F32)
    mid = r1.astype(BF16)
    lo = (r1 - mid.astype(F32)).astype(BF16)
    g3 = jnp.dot(mtab_ref[d], jnp.concatenate([hi, mid, lo], axis=1), preferred_element_type=F32)
    e = jnp.exp(g3[:, :kw] + g3[:, kw:2 * kw] + g3[:, 2 * kw:])

    qt = (q * e[0:C]).astype(BF16)
    kt = (k * e[C:2 * C]).astype(BF16)
    nt = 2 + GLA_LEVELS + GLA_SUB - 1
    gamma = e[nt * C:nt * C + 1]

    ind = ind_ref[...]
    a = jnp.zeros((C, kw), F32)
    for l in range(GLA_LEVELS):
        el = e[(2 + l) * C:(3 + l) * C]
        rq = rq_ref[d, l * C:(l + 1) * C, :]
        qh = (q * (el * rq)).astype(BF16)
        kh = (k * (el * (1.0 - rq))).astype(BF16)
        bdk = jnp.concatenate([kh] * GLA_HEADS, axis=0) * ind
        al = lax.dot_general(qh, bdk, (((1,), (1,)), ((), ())), preferred_element_type=F32)
        a = a + al * sm_ref[d, l * C:(l + 1) * C, :]
    ksh = jnp.dot(shm_ref[d], k_b, preferred_element_type=F32)
    for j in range(GLA_SUB):
        if j == 0:
            p = q * k
        else:
            p = q * ksh[(j - 1) * C:j * C] * e[(2 + GLA_LEVELS + j - 1) * C:(2 + GLA_LEVELS + j) * C]
        w = jnp.dot(p.astype(BF16), ind, preferred_element_type=F32)
        a = a + w * dm_ref[d, j * C:(j + 1) * C, :]

    bdv = jnp.concatenate([v_b] * GLA_HEADS, axis=0) * bdv_ref[...]
    o_intra = jnp.dot(a.astype(BF16), bdv, preferred_element_type=F32)
    upd = lax.dot_general(v_b, kt, (((0,), (0,)), ((), ())), preferred_element_type=F32) * bds_ref[...]
    return o_intra, qt, upd, gamma


def _gla_kernel(qf_ref, kf_ref, vf_ref, laf_ref, qb_ref, kb_ref, vb_ref, lab_ref,
                ckf_ref, cvf_ref, claf_ref, ckb_ref, cvb_ref, clab_ref,
                mtab_ref, shm_ref, rq_ref, sm_ref, dm_ref, ind_ref, bdv_ref, bds_ref,
                of_ref, ob_ref, st_ref, *, n_ctx_steps):
    s = pl.program_id(1)
    C = GLA_C
    cst = (mtab_ref, shm_ref, rq_ref, sm_ref, dm_ref, ind_ref, bdv_ref, bds_ref)

    @pl.when(s == 0)
    def _():
        st_ref[...] = jnp.zeros_like(st_ref)

    is_ctx = s < n_ctx_steps
    dirs = ((0, qf_ref, kf_ref, vf_ref, laf_ref, ckf_ref, cvf_ref, claf_ref, of_ref),
            (1, qb_ref, kb_ref, vb_ref, lab_ref, ckb_ref, cvb_ref, clab_ref, ob_ref))
    work = []
    for d, q_ref, k_ref, v_ref, la_ref, ck_ref, cv_ref, cla_ref, o_ref in dirs:
        order = (0, 1) if d == 0 else (1, 0)
        for c in order:
            rows = slice(c * C, (c + 1) * C)
            k_b = jnp.where(is_ctx, ck_ref[rows, :], k_ref[rows, :])
            v_b = jnp.where(is_ctx, cv_ref[rows, :], v_ref[rows, :])
            la = jnp.where(is_ctx, cla_ref[rows, :], la_ref[rows, :])
            work.append((d, rows, o_ref, _gla_chunk(q_ref[rows, :], k_b, v_b, la, cst, d)))
    states = [st_ref[0], st_ref[1]]
    for d, rows, o_ref, (o_intra, qt, upd, gamma) in work:
        st = states[d]
        o_ref[rows, :] = o_intra + lax.dot_general(qt, st.astype(BF16), (((1,), (1,)), ((), ())),
                                                   preferred_element_type=F32)
        states[d] = st * gamma + upd
    st_ref[0] = states[0]
    st_ref[1] = states[1]


def _gla(gq, gk, gv, la, cgk, cgv, cla):
    B, T, _ = gq.shape
    lc = cgk.shape[1]
    R = 2 * GLA_C
    assert lc % R == 0 and T % R == 0
    n_ctx, n_lat = lc // R, T // R
    consts = _gla_constants()
    mtab, shm = jnp.asarray(consts[0], BF16), jnp.asarray(consts[1], BF16)
    rq, sm, dm = [jnp.asarray(c) for c in consts[2:5]]
    ind, bdv = jnp.asarray(consts[5], BF16), jnp.asarray(consts[6], BF16)
    bds = jnp.asarray(consts[7])

    def lat(s, d):
        j = jnp.maximum(s - n_ctx, 0)
        return j if d == 0 else n_lat - 1 - j

    def ctx(s, d):
        j = jnp.minimum(s, n_ctx - 1)
        return j if d == 0 else n_ctx - 1 - j

    lspec = lambda w, d, c=0: pl.BlockSpec((None, R, w), lambda b, s: (b, lat(s, d), c))
    cspec = lambda w, d, c=0: pl.BlockSpec((None, R, w), lambda b, s: (b, ctx(s, d), c))
    lat_specs = lambda d: [lspec(GLA_K_W, d), lspec(GLA_K_W, d), lspec(GLA_V_W, d), lspec(GLA_K_W, d, d)]
    ctx_specs = lambda d: [cspec(GLA_K_W, d), cspec(GLA_V_W, d), cspec(GLA_K_W, d, d)]
    cs = [mtab, shm, rq, sm, dm, ind, bdv, bds]
    return pl.pallas_call(
        functools.partial(_gla_kernel, n_ctx_steps=n_ctx),
        out_shape=[jax.ShapeDtypeStruct((B, T, GLA_V_W), F32)] * 2,
        grid=(B, n_ctx + n_lat),
        in_specs=lat_specs(0) + lat_specs(1) + ctx_specs(0) + ctx_specs(1) + [_full(c.shape) for c in cs],
        out_specs=[lspec(GLA_V_W, 0), lspec(GLA_V_W, 1)],
        scratch_shapes=[pltpu.VMEM((2, GLA_V_W, GLA_K_W), F32)],
        compiler_params=_cparams(("parallel", "arbitrary")),
        name="gla",
    )(gq, gk, gv, la, gq, gk, gv, la, cgk, cgv, cla, cgk, cgv, cla, *cs)


def _merge_kernel(x_ref, at_ref, of_ref, ob_ref, gr_ref, ga_ref, gg_ref, gt1_ref, sc2_ref, sh2_ref,
                  n2_ref, gn_ref, ltri_ref, wba_ref, wbg_ref, wo_ref, wrh_ref, wrl_ref, br_ref,
                  xn_ref, h2_ref, ti_ref, rk_ref, tw_ref, cnt_ref, carry_ref):
    tm = x_ref.shape[0]

    @pl.when((pl.program_id(0) == 0) & (pl.program_id(1) == 0))
    def _():
        carry_ref[...] = jnp.zeros_like(carry_ref)

    go = of_ref[...] + ob_ref[...]
    parts = []
    for h in range(GLA_HEADS):
        gh = go[:, h * GLA_DV:(h + 1) * GLA_DV]
        ms = jnp.mean(gh * gh, axis=-1, keepdims=True)
        parts.append(gh * lax.rsqrt(ms + EPS))
    o = jnp.concatenate(parts, axis=1) * gn_ref[...] * gr_ref[...].astype(F32)
    ya = jnp.dot(at_ref[...], wba_ref[...], preferred_element_type=F32)
    yg = jnp.dot(o.astype(BF16), wbg_ref[...], preferred_element_type=F32)
    y = ga_ref[...].astype(F32) * ya + gg_ref[...].astype(F32) * yg
    z = jnp.dot(y.astype(BF16), wo_ref[...], preferred_element_type=F32)
    xn = x_ref[...] + gt1_ref[...] * z
    xn_ref[...] = xn
    ms = jnp.mean(xn * xn, axis=-1, keepdims=True)
    h2 = (xn * lax.rsqrt(ms + EPS) * n2_ref[...]) * (1.0 + sc2_ref[...]) + sh2_ref[...]
    hh = h2.astype(BF16)
    hl = (h2 - hh.astype(F32)).astype(BF16)
    h2_ref[...] = h2
    logits = (jnp.dot(hh, wrh_ref[...], preferred_element_type=F32)
              + jnp.dot(hl, wrh_ref[...], preferred_element_type=F32)
              + jnp.dot(hh, wrl_ref[...], preferred_element_type=F32)) + br_ref[...]
    lane = lax.broadcasted_iota(jnp.int32, (tm, LANES), 1).astype(F32)
    vals, idxs = [], []
    l = logits
    for _ in range(TOP_K):
        m = jnp.max(l, axis=-1, keepdims=True)
        ix = jnp.min(jnp.where(l == m, lane, float(LANES)), axis=-1, keepdims=True)
        vals.append(m)
        idxs.append(ix)
        l = jnp.where(lane == ix, -3.0e38, l)
    ex = [jnp.exp(v - vals[0]) for v in vals]
    den = ex[0] + ex[1] + ex[2] + ex[3]
    mh = jnp.zeros((tm, LANES), F32)
    for j in range(TOP_K):
        mh = mh + jnp.where(lane == idxs[j], 1.0, 0.0)
    pc = jnp.dot(ltri_ref[...], mh.astype(BF16), preferred_element_type=F32) + carry_ref[...]
    ti = jnp.zeros((tm, LANES), F32)
    rk = jnp.zeros((tm, LANES), F32)
    tw = jnp.zeros((tm, LANES), F32)
    for j in range(TOP_K):
        rj = jnp.sum(jnp.where(lane == idxs[j], pc, 0.0), axis=-1, keepdims=True)
        ti = jnp.where(lane == float(j), idxs[j], ti)
        rk = jnp.where(lane == float(j), rj, rk)
        tw = jnp.where(lane == float(j), ex[j] / den, tw)
    ti_ref[...] = ti.astype(jnp.int32)
    rk_ref[...] = rk.astype(jnp.int32)
    tw_ref[...] = tw
    total = carry_ref[...] + jnp.sum(mh, axis=0, keepdims=True)
    carry_ref[...] = total
    cnt_ref[...] = total.astype(jnp.int32)


def _merge(x, attn_o, gla_o, gr, ga, gg, gt1, sc2, sh2, norm2, wts, *, tm):
    B, T, D = x.shape
    row = lambda w: pl.BlockSpec((None, tm, w), lambda b, t: (b, t, 0))
    vec = pl.BlockSpec((None, 1, D), lambda b, t: (b, 0, 0))
    names = ("wba", "wbg", "wo", "wrh", "wrl", "br")
    ltri = jnp.asarray(np.tril(np.ones((tm, tm), np.float32), -1), BF16)
    return pl.pallas_call(
        _merge_kernel,
        out_shape=[jax.ShapeDtypeStruct((B, T, D), F32), jax.ShapeDtypeStruct((B, T, D), F32),
                   jax.ShapeDtypeStruct((B, T, LANES), jnp.int32), jax.ShapeDtypeStruct((B, T, LANES), jnp.int32),
                   jax.ShapeDtypeStruct((B, T, LANES), F32), jax.ShapeDtypeStruct((1, LANES), jnp.int32)],
        grid=(B, T // tm),
        in_specs=[row(D), row(ATT_W), row(GLA_V_W), row(GLA_V_W),
                  row(GLA_V_W), row(D), row(D), vec, vec, vec,
                  _full((1, D)), _full((1, GLA_V_W)), _full((tm, tm))] + [_full(wts[n].shape) for n in names],
        out_specs=[row(D), row(D), row(LANES), row(LANES), row(LANES), _full((1, LANES))],
        scratch_shapes=[pltpu.VMEM((1, LANES), F32)],
        compiler_params=_cparams(("arbitrary", "arbitrary")),
        name="merge",
    )(x, attn_o, gla_o[0], gla_o[1], gr, ga, gg, gt1, sc2, sh2, norm2, wts["gn"], ltri, *[wts[n] for n in names])


def _dispatch_kernel(ti_ref, rk_ref, ps_ref, zs_ref, nu_ref, h_ref, xs_ref, buf, zbuf, sem, zsem, *, tb, nsteps):
    s = pl.program_id(0)
    nblk = xs_ref.shape[0] // MOE_BLOCK
    slot = s % 2

    def wait_rows(sl):
        for _ in range(TOP_K):
            pltpu.make_async_copy(buf.at[sl], xs_ref.at[pl.ds(0, tb)], sem.at[sl]).wait()

    @pl.when(s == 0)
    def _():
        zbuf[...] = jnp.zeros_like(zbuf)

        def zstart(e, c):
            z0 = pl.multiple_of(zs_ref[e], MOE_BLOCK)
            pltpu.make_async_copy(zbuf, xs_ref.at[pl.ds(z0, MOE_BLOCK)], zsem).start()
            return c

        def zwait(e, c):
            pltpu.make_async_copy(zbuf, xs_ref.at[pl.ds(0, MOE_BLOCK)], zsem).wait()
            return c

        lax.fori_loop(0, N_EXPERTS, zstart, 0)
        lax.fori_loop(0, N_EXPERTS, zwait, 0)

        def tstart(j, c):
            pltpu.make_async_copy(zbuf, xs_ref.at[pl.ds(pl.multiple_of(j * MOE_BLOCK, MOE_BLOCK), MOE_BLOCK)],
                                  zsem).start()
            return c

        lax.fori_loop(nu_ref[0], nblk, tstart, 0)
        lax.fori_loop(nu_ref[0], nblk, zwait, 0)

    @pl.when(s >= 2)
    def _():
        wait_rows(slot)

    buf[slot] = h_ref[...]

    def issue(r, c):
        for k in range(TOP_K):
            d = ps_ref[ti_ref[0, r * TOP_K + k]] + rk_ref[0, r * TOP_K + k]
            pltpu.make_async_copy(buf.at[slot, pl.ds(r, 1)], xs_ref.at[pl.ds(d, 1)], sem.at[slot]).start()
        return c

    lax.fori_loop(0, tb, issue, 0)

    @pl.when(s == nsteps - 1)
    def _():
        wait_rows(slot)
        if nsteps >= 2:
            wait_rows(1 - slot)


def _dispatch(ti4, rk4, pad_start, zstart, n_used, h2, cap, *, tb):
    n, D = h2.shape
    nsteps = n // tb
    idx = pl.BlockSpec((None, 1, tb * TOP_K), lambda s: (s, 0, 0), memory_space=pltpu.SMEM)
    smem = pl.BlockSpec(memory_space=pltpu.SMEM)
    return pl.pallas_call(
        functools.partial(_dispatch_kernel, tb=tb, nsteps=nsteps),
        out_shape=jax.ShapeDtypeStruct((cap, D), F32),
        grid=(nsteps,),
        in_specs=[idx, idx, smem, smem, smem, pl.BlockSpec((tb, D), lambda s: (s, 0))],
        out_specs=pl.BlockSpec(memory_space=pl.ANY),
        scratch_shapes=[pltpu.VMEM((2, tb, D), F32), pltpu.VMEM((MOE_BLOCK, D), F32),
                        pltpu.SemaphoreType.DMA((2,)), pltpu.SemaphoreType.DMA(())],
        compiler_params=_cparams(("arbitrary",)),
        name="dispatch",
    )(ti4, rk4, pad_start, zstart, n_used, h2)


def _expert_kernel(be_ref, nu_ref, xs_ref, w1_ref, b1_ref, w2_ref, b2_ref, o_ref, w1b, w2b):
    i = pl.program_id(0)
    e = be_ref[i]
    prev = be_ref[jnp.maximum(i - 1, 0)]

    @pl.when((i == 0) | (e != prev))
    def _():
        w1b[...] = w1_ref[...].astype(BF16)
        w2b[...] = w2_ref[...].astype(BF16)

    @pl.when(i < nu_ref[0])
    def _():
        gu = jnp.dot(xs_ref[...].astype(BF16), w1b[...], preferred_element_type=F32) + b1_ref[...]
        gate = jnp.minimum(gu[:, :D_FF], SWIGLU_LIMIT)
        up = jnp.clip(gu[:, D_FF:], -SWIGLU_LIMIT, SWIGLU_LIMIT)
        act = gate * (1.0 / (1.0 + jnp.exp(-SWIGLU_ALPHA * gate))) * (up + 1.0)
        o_ref[...] = jnp.dot(act.astype(BF16), w2b[...], preferred_element_type=F32) + b2_ref[...]

    @pl.when(i >= nu_ref[0])
    def _():
        o_ref[...] = jnp.zeros_like(o_ref)


def _experts(blk_e, n_used, xs, w1, b1, w2, b2):
    cap = xs.shape[0]
    n_blk = cap // MOE_BLOCK
    ne = w1.shape[0]
    gs = pltpu.PrefetchScalarGridSpec(
        num_scalar_prefetch=2, grid=(n_blk,),
        in_specs=[pl.BlockSpec((MOE_BLOCK, D_MODEL), lambda i, be, nu: (i, 0)),
                  pl.BlockSpec((None, D_MODEL, 2 * D_FF), lambda i, be, nu: (be[i], 0, 0)),
                  pl.BlockSpec((None, 1, 2 * D_FF), lambda i, be, nu: (be[i], 0, 0)),
                  pl.BlockSpec((None, D_FF, D_MODEL), lambda i, be, nu: (be[i], 0, 0)),
                  pl.BlockSpec((None, 1, D_MODEL), lambda i, be, nu: (be[i], 0, 0))],
        out_specs=pl.BlockSpec((MOE_BLOCK, D_MODEL), lambda i, be, nu: (i, 0)),
        scratch_shapes=[pltpu.VMEM((D_MODEL, 2 * D_FF), BF16), pltpu.VMEM((D_FF, D_MODEL), BF16)])
    return pl.pallas_call(
        _expert_kernel, grid_spec=gs,
        out_shape=jax.ShapeDtypeStruct((cap, D_MODEL), F32),
        compiler_params=_cparams(("arbitrary",)),
        name="experts",
    )(blk_e, n_used, xs, w1, b1.reshape(ne, 1, 2 * D_FF), w2, b2.reshape(ne, 1, D_MODEL))


def _combine_kernel(ti_ref, rk_ref, ps_ref, xn_ref, tw_ref, gt2_ref, ys_ref, o_ref, gbuf, sem, *, tb):
    def issue(r, c):
        for k in range(TOP_K):
            d = ps_ref[ti_ref[0, r * TOP_K + k]] + rk_ref[0, r * TOP_K + k]
            pltpu.make_async_copy(ys_ref.at[pl.ds(d, 1)], gbuf.at[k, pl.ds(r, 1)], sem).start()
        return c

    lax.fori_loop(0, tb, issue, 0)
    for k in range(TOP_K):
        pltpu.make_async_copy(ys_ref.at[pl.ds(0, tb)], gbuf.at[k], sem).wait()
    tw = tw_ref[...]
    y = (gbuf[0] * tw[:, 0:1] + gbuf[1] * tw[:, 1:2]) + (gbuf[2] * tw[:, 2:3] + gbuf[3] * tw[:, 3:4])
    o_ref[...] = xn_ref[...] + gt2_ref[...] * y


def _combine(ti4, rk4, pad_start, xn, tw, gt2, ys, *, tb):
    B, T, D = xn.shape
    nt = T // tb
    idx = pl.BlockSpec((None, 1, tb * TOP_K), lambda b, t: (b * nt + t, 0, 0), memory_space=pltpu.SMEM)
    row = lambda w: pl.BlockSpec((None, tb, w), lambda b, t: (b, t, 0))
    return pl.pallas_call(
        functools.partial(_combine_kernel, tb=tb),
        out_shape=jax.ShapeDtypeStruct((B, T, D), F32),
        grid=(B, nt),
        in_specs=[idx, idx, pl.BlockSpec(memory_space=pltpu.SMEM), row(D), row(LANES),
                  pl.BlockSpec((None, 1, D), lambda b, t: (b, 0, 0)), pl.BlockSpec(memory_space=pl.ANY)],
        out_specs=row(D),
        scratch_shapes=[pltpu.VMEM((TOP_K, tb, D), F32), pltpu.SemaphoreType.DMA(())],
        compiler_params=_cparams(("arbitrary", "arbitrary")),
        name="combine",
    )(ti4, rk4, pad_start, xn, tw, gt2, ys)


def _rope_tables(T):
    rows = T // GRID_W
    row = jnp.repeat(jnp.arange(rows, dtype=F32), GRID_W)
    col = jnp.tile(jnp.arange(GRID_W, dtype=F32), rows)
    inv = ROPE_BASE ** (-jnp.arange(0, AXIS_ROT, 2, dtype=F32) / AXIS_ROT)
    ang_r, ang_c = row[:, None] * inv, col[:, None] * inv
    m = AXIS_ROT // 2
    ang = jnp.concatenate([ang_r, ang_r, ang_c, ang_c], axis=1)
    sign = jnp.tile(jnp.concatenate([-jnp.ones((m,), F32), jnp.ones((m,), F32)]), 2)
    cos = jnp.tile(jnp.cos(ang), (1, LANES // HEAD_DIM))
    sin = jnp.tile(jnp.sin(ang) * sign, (1, LANES // HEAD_DIM))
    return cos, sin


def _head_perm():
    order = []
    for m in range(ATT_GROUP):
        for kv in range(ATT_KV_HEADS):
            h = kv * ATT_GROUP + m
            order.extend(range(h * HEAD_DIM, (h + 1) * HEAD_DIM))
    return np.asarray(order)


def kernel(x, c, ctx, c_ctx, w_mod, b_mod, norm1, norm2, w_in, q_norm, k_norm, attn_sink,
           w_alpha_f, b_alpha_f, w_alpha_b, b_alpha_b, gla_norm, w_branch_attn, w_branch_gla,
           w_out, w_router, b_router, w_exp_in, b_exp_in, w_exp_out, b_exp_out):
    B, T, D = x.shape
    depth = w_mod.shape[0]
    assert depth == 1, "single-layer kernel: the context stream update only feeds later layers"
    l = 0
    perm = _head_perm()

    rows = ((B + 1 + 7) // 8) * 8
    c_all = jnp.zeros((rows, D), F32).at[:B].set(c).at[B].set(c_ctx)
    mod = _modulation(c_all, w_mod[l], b_mod[l])
    sh1, sc1, gt1, sh2, sc2, gt2 = [mod[:B, j * D:(j + 1) * D].reshape(B, 1, D) for j in range(6)]
    csh1, csc1 = [jnp.broadcast_to(mod[B, j * D:(j + 1) * D].reshape(1, 1, D), (B, 1, D)) for j in range(2)]

    offs = np.concatenate([[0], np.cumsum(IN_SPLITS)])
    cols = lambda j: w_in[l][:, offs[j]:offs[j + 1]]
    wal = jnp.zeros((2 * GLA_RANK, 2 * GLA_K_W), F32)
    wal = wal.at[:GLA_RANK, :GLA_K_W].set(w_alpha_f[l]).at[GLA_RANK:, GLA_K_W:].set(w_alpha_b[l])
    pw = {
        "wq": cols(0)[:, perm].astype(BF16), "wk": cols(1).astype(BF16), "wv": cols(2).astype(BF16),
        "wgq": cols(3).astype(BF16), "wgk": cols(4).astype(BF16), "wgv": cols(5).astype(BF16),
        "wgr": cols(6).astype(BF16), "wga": cols(9).astype(BF16), "wgg": cols(10).astype(BF16),
        "wlr": jnp.concatenate([cols(7), cols(8)], axis=1).astype(BF16),
        "qn": jnp.tile(q_norm[l], LANES // HEAD_DIM).reshape(1, LANES),
        "kn": jnp.tile(k_norm[l], LANES // HEAD_DIM).reshape(1, LANES),
        "wal": wal.astype(BF16),
        "bal": jnp.concatenate([b_alpha_f[l], b_alpha_b[l]]).reshape(1, 2 * GLA_K_W),
    }
    cos, sin = _rope_tables(T)
    n1 = norm1[l].reshape(1, D)
    tm = min(256, T)
    aq, ak, av, gq, gk, gv, gr, ga, gg, la = _inproj(
        x, sh1, sc1, n1, {"cos": cos, "sin": sin}, pw, rope=True, full=True, tm=tm)
    cak, cav, cgk, cgv, cla = _inproj(
        ctx, csh1, csc1, n1, None, pw, rope=False, full=False, tm=min(256, ctx.shape[1]))

    attn_o = _attention(attn_sink[l], aq, ak, av, cak, cav)
    gla_o = _gla(gq, gk, gv, la, cgk, cgv, cla)

    wr = jnp.zeros((D, LANES), F32).at[:, :N_EXPERTS].set(w_router[l])
    wrh = wr.astype(BF16)
    mw = {
        "gn": jnp.tile(gla_norm[l], GLA_HEADS).reshape(1, GLA_V_W),
        "wba": w_branch_attn[l][perm, :].astype(BF16), "wbg": w_branch_gla[l].astype(BF16),
        "wo": w_out[l].astype(BF16), "wrh": wrh, "wrl": (wr - wrh.astype(F32)).astype(BF16),
        "br": jnp.full((1, LANES), NEG, F32).at[0, :N_EXPERTS].set(b_router[l]),
    }
    xn, h2, ti, rk, tw, cnt = _merge(x, attn_o, gla_o, gr, ga, gg, gt1, sc2, sh2, norm2[l].reshape(1, D), mw, tm=tm)

    n = B * T
    counts = cnt[0, :N_EXPERTS]
    padded = (counts + MOE_BLOCK - 1) // MOE_BLOCK * MOE_BLOCK
    pad_end = jnp.cumsum(padded)
    pad_start = (pad_end - padded).astype(jnp.int32)
    zstart = jnp.maximum(pad_end - MOE_BLOCK, 0).astype(jnp.int32)
    cap = (n * TOP_K + N_EXPERTS * (MOE_BLOCK - 1)) // MOE_BLOCK * MOE_BLOCK
    n_blk = cap // MOE_BLOCK
    blk_e = jnp.minimum(jnp.searchsorted(pad_end, jnp.arange(n_blk, dtype=jnp.int32) * MOE_BLOCK, side='right'),
                        N_EXPERTS - 1).astype(jnp.int32)
    n_used = (pad_end[-1] // MOE_BLOCK).astype(jnp.int32).reshape(1)

    tb = min(256, T)
    ti4 = ti[:, :, :TOP_K].reshape(n // tb, 1, tb * TOP_K)
    rk4 = rk[:, :, :TOP_K].reshape(n // tb, 1, tb * TOP_K)
    xs = _dispatch(ti4, rk4, pad_start, zstart, n_used, h2.reshape(n, D), cap, tb=tb)
    ys = _experts(blk_e, n_used, xs, w_exp_in[l], b_exp_in[l], w_exp_out[l], b_exp_out[l])
    return _combine(ti4, rk4, pad_start, xn, tw, gt2, ys, tb=tb)
```

```python
import functools

import numpy as np
import jax
import jax.numpy as jnp
from jax import lax
from jax.experimental import pallas as pl
from jax.experimental.pallas import tpu as pltpu

F32 = jnp.float32
BF16 = jnp.bfloat16

D_MODEL = 1024
GRID_W = 64
EPS = 1e-6
ATT_HEADS = 8
ATT_KV_HEADS = 2
ATT_GROUP = ATT_HEADS // ATT_KV_HEADS
HEAD_DIM = 64
WINDOW = 128
ATT_BLOCK = 128
ROPE_BASE = 10000.0
AXIS_ROT = HEAD_DIM // 2
GLA_HEADS = 4
GLA_DK = 64
GLA_DV = 128
GLA_RANK = 16
GLA_TAU = 16.0
N_EXPERTS = 32
TOP_K = 4
D_FF = D_MODEL
SWIGLU_ALPHA = 1.702
SWIGLU_LIMIT = 7.0
MOE_STEP = 512

ATT_W = ATT_HEADS * HEAD_DIM
ATT_KV_W = ATT_KV_HEADS * HEAD_DIM
GLA_K_W = GLA_HEADS * GLA_DK
GLA_V_W = GLA_HEADS * GLA_DV
IN_SPLITS = (ATT_W, ATT_KV_W, ATT_KV_W, GLA_K_W, GLA_K_W, GLA_V_W, GLA_V_W, GLA_RANK, GLA_RANK, D_MODEL, D_MODEL)

LANES = 128
VMEM_LIMIT = 56 * 1024 * 1024
NEG = -1e30

GLA_C = 64
GLA_SUB = 4
GLA_LEVELS = 4


def _cparams(sem):
    return pltpu.CompilerParams(dimension_semantics=sem, vmem_limit_bytes=VMEM_LIMIT)


def _full(shape):
    n = len(shape)
    return pl.BlockSpec(shape, lambda *_: (0,) * n)


def _mod_kernel(c_ref, w_ref, b_ref, o_ref):
    c = c_ref[...]
    s = c * (1.0 / (1.0 + jnp.exp(-c)))
    o_ref[...] = jnp.dot(s, w_ref[...], preferred_element_type=F32,
                         precision=lax.Precision.HIGHEST) + b_ref[...]


def _modulation(c_all, w_mod, b_mod):
    rows = c_all.shape[0]
    n = w_mod.shape[1]
    tn = 1536
    return pl.pallas_call(
        _mod_kernel,
        out_shape=jax.ShapeDtypeStruct((rows, n), F32),
        grid=(n // tn,),
        in_specs=[pl.BlockSpec((rows, D_MODEL), lambda j: (0, 0)),
                  pl.BlockSpec((D_MODEL, tn), lambda j: (0, j)),
                  pl.BlockSpec((1, tn), lambda j: (0, j))],
        out_specs=pl.BlockSpec((rows, tn), lambda j: (0, j)),
        compiler_params=_cparams(("arbitrary",)),
        name="mod",
    )(c_all, w_mod, b_mod.reshape(1, n))


def _pair_norm(a, g, lo):
    s = a * a
    tot = jnp.sum(s, axis=-1, keepdims=True)
    slo = jnp.sum(jnp.where(lo, s, 0.0), axis=-1, keepdims=True)
    ms = jnp.where(lo, slo, tot - slo) * (1.0 / HEAD_DIM)
    return a * lax.rsqrt(ms + EPS) * g


def _rope(y, cos, sin, first):
    up = pltpu.roll(y, LANES - AXIS_ROT // 2, 1)
    dn = pltpu.roll(y, AXIS_ROT // 2, 1)
    return y * cos + jnp.where(first, up, dn) * sin


def _inproj_kernel(*refs, rope, full):
    if full:
        (x_ref, sh_ref, sc_ref, n1_ref, cos_ref, sin_ref, qn_ref, kn_ref, wal_ref, bal_ref,
         wq, wk, wv, wgq, wgk, wgv, wgr, wga, wgg, wlr,
         oq, ok, ov, ogq, ogk, ogv, ogr, oga, ogg, ola) = refs
    else:
        (x_ref, sh_ref, sc_ref, n1_ref, kn_ref, wal_ref, bal_ref,
         wk, wv, wgk, wgv, wlr,
         ok, ov, ogk, ogv, ola) = refs
    x = x_ref[...]
    tm = x.shape[0]
    ms = jnp.mean(x * x, axis=-1, keepdims=True)
    h = (x * lax.rsqrt(ms + EPS) * n1_ref[...]) * (1.0 + sc_ref[...]) + sh_ref[...]
    hb = h.astype(BF16)

    def proj(w_ref):
        return jnp.dot(hb, w_ref[...], preferred_element_type=F32)

    lane = lax.broadcasted_iota(jnp.int32, (tm, LANES), 1)
    lo = lane < HEAD_DIM
    first = (lane % AXIS_ROT) < (AXIS_ROT // 2)
    if rope:
        cos = cos_ref[...]
        sin = sin_ref[...]

    k = _pair_norm(proj(wk), kn_ref[...], lo)
    if rope:
        k = _rope(k, cos, sin, first)
    ok[...] = k.astype(BF16)
    ov[...] = proj(wv).astype(BF16)
    ogk[...] = proj(wgk).astype(BF16)
    ogv[...] = proj(wgv).astype(BF16)
    lr = proj(wlr).astype(BF16)
    z = jnp.dot(lr, wal_ref[...], preferred_element_type=F32) + bal_ref[...]
    ola[...] = (jnp.minimum(z, 0.0) - jnp.log(1.0 + jnp.exp(-jnp.abs(z)))) * (1.0 / GLA_TAU)
    if full:
        q = proj(wq)
        for p in range(ATT_W // LANES):
            y = _pair_norm(q[:, p * LANES:(p + 1) * LANES], qn_ref[...], lo)
            if rope:
                y = _rope(y, cos, sin, first)
            oq[:, p * LANES:(p + 1) * LANES] = (y * HEAD_DIM ** -0.5).astype(BF16)
        ogq[...] = (proj(wgq) * GLA_DK ** -0.5).astype(BF16)
        g = proj(wgr)
        ogr[...] = (g * (1.0 / (1.0 + jnp.exp(-g)))).astype(BF16)
        oga[...] = (1.0 / (1.0 + jnp.exp(-proj(wga)))).astype(BF16)
        ogg[...] = (1.0 / (1.0 + jnp.exp(-proj(wgg)))).astype(BF16)


def _inproj(x, sh, sc, norm1, tabs, wts, *, rope, full, tm):
    B, T, D = x.shape
    grid = (B, T // tm)
    row = lambda w: pl.BlockSpec((None, tm, w), lambda b, t: (b, t, 0))
    vec = pl.BlockSpec((None, 1, D), lambda b, t: (b, 0, 0))
    tab = pl.BlockSpec((tm, LANES), lambda b, t: (t, 0))
    if full:
        names = ("wq", "wk", "wv", "wgq", "wgk", "wgv", "wgr", "wga", "wgg", "wlr")
        ins = [x, sh, sc, norm1, tabs["cos"], tabs["sin"], wts["qn"], wts["kn"], wts["wal"], wts["bal"]]
        specs = [row(D), vec, vec, _full((1, D)), tab, tab, _full((1, LANES)), _full((1, LANES)),
                 _full(wts["wal"].shape), _full(wts["bal"].shape)]
        out_w = (ATT_W, ATT_KV_W, ATT_KV_W, GLA_K_W, GLA_K_W, GLA_V_W, GLA_V_W, D, D)
    else:
        names = ("wk", "wv", "wgk", "wgv", "wlr")
        ins = [x, sh, sc, norm1, wts["kn"], wts["wal"], wts["bal"]]
        specs = [row(D), vec, vec, _full((1, D)), _full((1, LANES)),
                 _full(wts["wal"].shape), _full(wts["bal"].shape)]
        out_w = (ATT_KV_W, ATT_KV_W, GLA_K_W, GLA_V_W)
    ins += [wts[n] for n in names]
    specs += [_full(wts[n].shape) for n in names]
    out_shape = [jax.ShapeDtypeStruct((B, T, w), BF16) for w in out_w]
    out_shape.append(jax.ShapeDtypeStruct((B, T, 2 * GLA_K_W), F32))
    out_specs = [row(w) for w in out_w] + [row(2 * GLA_K_W)]
    return pl.pallas_call(
        functools.partial(_inproj_kernel, rope=rope, full=full),
        out_shape=out_shape, grid=grid, in_specs=specs, out_specs=out_specs,
        compiler_params=_cparams(("parallel", "arbitrary")),
        name="inproj_full" if full else "inproj_ctx",
    )(*ins)


def _attn_kernel(sink_ref, q_ref, k0_ref, k1_ref, k2_ref, k3_ref, kx_ref, v0_ref, v1_ref, v2_ref, v3_ref,
                 vx_ref, o_ref, *, seq):
    n = pl.program_id(1)
    blk = ATT_BLOCK
    lc = kx_ref.shape[0]
    nk = 3 * blk + lc
    nslab = ATT_W // LANES
    rows = nslab * blk
    kblocks = (k0_ref, k1_ref, k2_ref, k3_ref)
    vblocks = (v0_ref, v1_ref, v2_ref, v3_ref)
    ri = lax.broadcasted_iota(jnp.int32, (rows, nk), 0)
    kj = lax.broadcasted_iota(jnp.int32, (rows, nk), 1)
    qi = ri % blk
    lane = lax.broadcasted_iota(jnp.int32, (blk, LANES), 1)
    lo = lane < HEAD_DIM
    hrow = lax.broadcasted_iota(jnp.int32, (rows, 1), 0) // blk
    for sb in range(2):
        kcat = jnp.concatenate([r[...] for r in kblocks[sb:sb + 3]] + [kx_ref[...]], axis=0)
        vcat = jnp.concatenate([r[...] for r in vblocks[sb:sb + 3]] + [vx_ref[...]], axis=0)
        kpos = (2 * n + sb - 1) * blk + kj
        valid = ((jnp.abs(kj - blk - qi) <= WINDOW) & (kpos >= 0) & (kpos < seq)) | (kj >= 3 * blk)
        q = q_ref[sb * blk:(sb + 1) * blk, :]
        outs = []
        for kv in range(ATT_KV_HEADS):
            keep = lo if kv == 0 else jnp.logical_not(lo)
            qs = jnp.concatenate([jnp.where(keep, q[:, m * LANES:(m + 1) * LANES], jnp.zeros((blk, LANES), BF16))
                                  for m in range(nslab)], axis=0)
            s = lax.dot_general(qs, kcat, (((1,), (1,)), ((), ())), preferred_element_type=F32)
            s = jnp.where(valid, s, NEG)
            snk = jnp.zeros((rows, 1), F32)
            for m in range(nslab):
                snk = jnp.where(hrow == m, sink_ref[kv * ATT_GROUP + m], snk)
            mx = jnp.maximum(jnp.max(s, axis=-1, keepdims=True), snk)
            p = jnp.exp(s - mx)
            den = jnp.sum(p, axis=-1, keepdims=True) + jnp.exp(snk - mx)
            outs.append(jnp.dot(p.astype(BF16), vcat, preferred_element_type=F32) / den)
        for m in range(nslab):
            o_ref[sb * blk:(sb + 1) * blk, m * LANES:(m + 1) * LANES] = jnp.where(
                lo, outs[0][m * blk:(m + 1) * blk], outs[1][m * blk:(m + 1) * blk]).astype(BF16)


def _attention(sink, aq, ak, av, cak, cav):
    B, T, _ = aq.shape
    lc = cak.shape[1]
    blk = ATT_BLOCK
    nb = T // blk
    assert nb % 2 == 0
    kvspec = lambda off: pl.BlockSpec((None, blk, ATT_KV_W),
                                      lambda b, n: (b, jnp.clip(2 * n + off, 0, nb - 1), 0))
    cspec = pl.BlockSpec((None, lc, ATT_KV_W), lambda b, n: (b, 0, 0))
    qspec = pl.BlockSpec((None, 2 * blk, ATT_W), lambda b, n: (b, n, 0))
    kvs = [kvspec(off) for off in (-1, 0, 1, 2)]
    return pl.pallas_call(
        functools.partial(_attn_kernel, seq=T),
        out_shape=jax.ShapeDtypeStruct((B, T, ATT_W), BF16),
        grid=(B, nb // 2),
        in_specs=[pl.BlockSpec(memory_space=pltpu.SMEM), qspec] + kvs + [cspec] + kvs + [cspec],
        out_specs=qspec,
        compiler_params=_cparams(("parallel", "arbitrary")),
        name="attn",
    )(sink, aq, ak, ak, ak, ak, cak, av, av, av, av, cav)


def _gla_constants():
    C, sub, L = GLA_C, GLA_SUB, GLA_LEVELS
    i = np.arange(C)[:, None]
    t = np.arange(C)[None, :]
    tabs = [t <= i, t > i]
    rowq, same = [], []
    for l in range(L):
        s = C >> l
        mid = (i // s) * s + s // 2
        isq = i >= mid
        tabs.append(np.where(isq, (t >= mid) & (t <= i), (t > i) & (t < mid)))
        rowq.append(np.broadcast_to(isq, (C, C)))
        same.append((i // s) == (t // s))
    shifts, dmask = [], [t == i]
    for d in range(1, sub):
        ok = (i % sub) >= d
        tabs.append(ok & (t > i - d) & (t <= i))
        shifts.append(ok & (t == i - d))
        dmask.append(ok & (t == i - d))
    tabs.append(np.ones((8, C), bool))
    flip = lambda a: a[::-1, ::-1]
    tile = lambda a: np.tile(a, (1, GLA_HEADS))

    def both(xs, lanes):
        f = (lambda a: tile(a)) if lanes else (lambda a: a)
        return np.stack([np.concatenate([f(a) for a in xs], 0),
                         np.concatenate([f(flip(a)) for a in xs], 0)]).astype(np.float32)

    hk = np.arange(GLA_K_W) // GLA_DK
    hv = np.arange(GLA_V_W) // GLA_DV
    ind = (hk[:, None] == hk[None, :]).astype(np.float32)
    bdv = (hk[:, None] == hv[None, :]).astype(np.float32)
    return (both(tabs, False), both(shifts, False), both(rowq, True), both(same, True),
            both(dmask, True), ind, bdv, np.ascontiguousarray(bdv.T))


def _gla_chunk(q_b, k_b, v_b, la, cst, d):
    mtab_ref, shm_ref, rq_ref, sm_ref, dm_ref, ind_ref, bdv_ref, bds_ref = cst
    C = GLA_C
    kw = GLA_K_W
    q = q_b.astype(F32)
    k = k_b.astype(F32)
    hi = la.astype(BF16)
    r1 = la - hi.astype(F32)
    mid = r1.astype(BF16)
    lo = (r1 - mid.astype(F32)).astype(BF16)
    g3 = jnp.dot(mtab_ref[d], jnp.concatenate([hi, mid, lo], axis=1), preferred_element_type=F32)
    e = jnp.exp(g3[:, :kw] + g3[:, kw:2 * kw] + g3[:, 2 * kw:])

    qt = (q * e[0:C]).astype(BF16)
    kt = (k * e[C:2 * C]).astype(BF16)
    nt = 2 + GLA_LEVELS + GLA_SUB - 1
    gamma = e[nt * C:nt * C + 1]

    ind = ind_ref[...]
    a = jnp.zeros((C, kw), F32)
    for l in range(GLA_LEVELS):
        el = e[(2 + l) * C:(3 + l) * C]
        rq = rq_ref[d, l * C:(l + 1) * C, :]
        qh = (q * (el * rq)).astype(BF16)
        kh = (k * (el * (1.0 - rq))).astype(BF16)
        bdk = jnp.concatenate([kh] * GLA_HEADS, axis=0) * ind
        al = lax.dot_general(qh, bdk, (((1,), (1,)), ((), ())), preferred_element_type=F32)
        a = a + al * sm_ref[d, l * C:(l + 1) * C, :]
    ksh = jnp.dot(shm_ref[d], k_b, preferred_element_type=F32)
    for j in range(GLA_SUB):
        if j == 0:
            p = q * k
        else:
            p = q * ksh[(j - 1) * C:j * C] * e[(2 + GLA_LEVELS + j - 1) * C:(2 + GLA_LEVELS + j) * C]
        w = jnp.dot(p.astype(BF16), ind, preferred_element_type=F32)
        a = a + w * dm_ref[d, j * C:(j + 1) * C, :]

    bdv = jnp.concatenate([v_b] * GLA_HEADS, axis=0) * bdv_ref[...]
    o_intra = jnp.dot(a.astype(BF16), bdv, preferred_element_type=F32)
    upd = lax.dot_general(v_b, kt, (((0,), (0,)), ((), ())), preferred_element_type=F32) * bds_ref[...]
    return o_intra, qt, upd, gamma


def _gla_kernel(qf_ref, kf_ref, vf_ref, laf_ref, qb_ref, kb_ref, vb_ref, lab_ref,
                ckf_ref, cvf_ref, claf_ref, ckb_ref, cvb_ref, clab_ref,
                mtab_ref, shm_ref, rq_ref, sm_ref, dm_ref, ind_ref, bdv_ref, bds_ref,
                of_ref, ob_ref, st_ref, *, n_ctx_steps):
    s = pl.program_id(1)
    C = GLA_C
    cst = (mtab_ref, shm_ref, rq_ref, sm_ref, dm_ref, ind_ref, bdv_ref, bds_ref)

    @pl.when(s == 0)
    def _():
        st_ref[...] = jnp.zeros_like(st_ref)

    is_ctx = s < n_ctx_steps
    dirs = ((0, qf_ref, kf_ref, vf_ref, laf_ref, ckf_ref, cvf_ref, claf_ref, of_ref),
            (1, qb_ref, kb_ref, vb_ref, lab_ref, ckb_ref, cvb_ref, clab_ref, ob_ref))
    work = []
    for d, q_ref, k_ref, v_ref, la_ref, ck_ref, cv_ref, cla_ref, o_ref in dirs:
        order = (0, 1) if d == 0 else (1, 0)
        for c in order:
            rows = slice(c * C, (c + 1) * C)
            k_b = jnp.where(is_ctx, ck_ref[rows, :], k_ref[rows, :])
            v_b = jnp.where(is_ctx, cv_ref[rows, :], v_ref[rows, :])
            la = jnp.where(is_ctx, cla_ref[rows, :], la_ref[rows, :])
            work.append((d, rows, o_ref, _gla_chunk(q_ref[rows, :], k_b, v_b, la, cst, d)))
    states = [st_ref[0], st_ref[1]]
    for d, rows, o_ref, (o_intra, qt, upd, gamma) in work:
        st = states[d]
        o_ref[rows, :] = o_intra + lax.dot_general(qt, st.astype(BF16), (((1,), (1,)), ((), ())),
                                                   preferred_element_type=F32)
        states[d] = st * gamma + upd
    st_ref[0] = states[0]
    st_ref[1] = states[1]


def _gla(gq, gk, gv, la, cgk, cgv, cla):
    B, T, _ = gq.shape
    lc = cgk.shape[1]
    R = 2 * GLA_C
    assert lc % R == 0 and T % R == 0
    n_ctx, n_lat = lc // R, T // R
    consts = _gla_constants()
    mtab, shm = jnp.asarray(consts[0], BF16), jnp.asarray(consts[1], BF16)
    rq, sm, dm = [jnp.asarray(c) for c in consts[2:5]]
    ind, bdv = jnp.asarray(consts[5], BF16), jnp.asarray(consts[6], BF16)
    bds = jnp.asarray(consts[7])

    def lat(s, d):
        j = jnp.maximum(s - n_ctx, 0)
        return j if d == 0 else n_lat - 1 - j

    def ctx(s, d):
        j = jnp.minimum(s, n_ctx - 1)
        return j if d == 0 else n_ctx - 1 - j

    lspec = lambda w, d, c=0: pl.BlockSpec((None, R, w), lambda b, s: (b, lat(s, d), c))
    cspec = lambda w, d, c=0: pl.BlockSpec((None, R, w), lambda b, s: (b, ctx(s, d), c))
    lat_specs = lambda d: [lspec(GLA_K_W, d), lspec(GLA_K_W, d), lspec(GLA_V_W, d), lspec(GLA_K_W, d, d)]
    ctx_specs = lambda d: [cspec(GLA_K_W, d), cspec(GLA_V_W, d), cspec(GLA_K_W, d, d)]
    cs = [mtab, shm, rq, sm, dm, ind, bdv, bds]
    return pl.pallas_call(
        functools.partial(_gla_kernel, n_ctx_steps=n_ctx),
        out_shape=[jax.ShapeDtypeStruct((B, T, GLA_V_W), F32)] * 2,
        grid=(B, n_ctx + n_lat),
        in_specs=lat_specs(0) + lat_specs(1) + ctx_specs(0) + ctx_specs(1) + [_full(c.shape) for c in cs],
        out_specs=[lspec(GLA_V_W, 0), lspec(GLA_V_W, 1)],
        scratch_shapes=[pltpu.VMEM((2, GLA_V_W, GLA_K_W), F32)],
        compiler_params=_cparams(("parallel", "arbitrary")),
        name="gla",
    )(gq, gk, gv, la, gq, gk, gv, la, cgk, cgv, cla, cgk, cgv, cla, *cs)


def _merge_kernel(x_ref, at_ref, of_ref, ob_ref, gr_ref, ga_ref, gg_ref, gt1_ref, sc2_ref, sh2_ref,
                  n2_ref, gn_ref, ltri_ref, wba_ref, wbg_ref, wo_ref, wrh_ref, wrl_ref, br_ref,
                  xn_ref, h2_ref, ti_ref, rk_ref, tw_ref, cnt_ref, carry_ref):
    tm = x_ref.shape[0]

    @pl.when((pl.program_id(0) == 0) & (pl.program_id(1) == 0))
    def _():
        carry_ref[...] = jnp.zeros_like(carry_ref)

    go = of_ref[...] + ob_ref[...]
    parts = []
    for h in range(GLA_HEADS):
        gh = go[:, h * GLA_DV:(h + 1) * GLA_DV]
        ms = jnp.mean(gh * gh, axis=-1, keepdims=True)
        parts.append(gh * lax.rsqrt(ms + EPS))
    o = jnp.concatenate(parts, axis=1) * gn_ref[...] * gr_ref[...].astype(F32)
    ya = jnp.dot(at_ref[...], wba_ref[...], preferred_element_type=F32)
    yg = jnp.dot(o.astype(BF16), wbg_ref[...], preferred_element_type=F32)
    y = ga_ref[...].astype(F32) * ya + gg_ref[...].astype(F32) * yg
    z = jnp.dot(y.astype(BF16), wo_ref[...], preferred_element_type=F32)
    xn = x_ref[...] + gt1_ref[...] * z
    xn_ref[...] = xn
    ms = jnp.mean(xn * xn, axis=-1, keepdims=True)
    h2 = (xn * lax.rsqrt(ms + EPS) * n2_ref[...]) * (1.0 + sc2_ref[...]) + sh2_ref[...]
    hh = h2.astype(BF16)
    hl = (h2 - hh.astype(F32)).astype(BF16)
    h2_ref[...] = h2
    logits = (jnp.dot(hh, wrh_ref[...], preferred_element_type=F32)
              + jnp.dot(hl, wrh_ref[...], preferred_element_type=F32)
              + jnp.dot(hh, wrl_ref[...], preferred_element_type=F32)) + br_ref[...]
    lane = lax.broadcasted_iota(jnp.int32, (tm, LANES), 1).astype(F32)
    vals, idxs = [], []
    l = logits
    for _ in range(TOP_K):
        m = jnp.max(l, axis=-1, keepdims=True)
        ix = jnp.min(jnp.where(l == m, lane, float(LANES)), axis=-1, keepdims=True)
        vals.append(m)
        idxs.append(ix)
        l = jnp.where(lane == ix, -3.0e38, l)
    ex = [jnp.exp(v - vals[0]) for v in vals]
    den = ex[0] + ex[1] + ex[2] + ex[3]
    mh = jnp.zeros((tm, LANES), F32)
    for j in range(TOP_K):
        mh = mh + jnp.where(lane == idxs[j], 1.0, 0.0)
    pc = jnp.dot(ltri_ref[...], mh.astype(BF16), preferred_element_type=F32) + carry_ref[...]
    ti = jnp.zeros((tm, LANES), F32)
    rk = jnp.zeros((tm, LANES), F32)
    tw = jnp.zeros((tm, LANES), F32)
    for j in range(TOP_K):
        rj = jnp.sum(jnp.where(lane == idxs[j], pc, 0.0), axis=-1, keepdims=True)
        ti = jnp.where(lane == float(j), idxs[j], ti)
        rk = jnp.where(lane == float(j), rj, rk)
        tw = jnp.where(lane == float(j), ex[j] / den, tw)
    ti_ref[...] = ti.astype(jnp.int32)
    rk_ref[...] = rk.astype(jnp.int32)
    tw_ref[...] = tw
    total = carry_ref[...] + jnp.sum(mh, axis=0, keepdims=True)
    carry_ref[...] = total
    cnt_ref[...] = total.astype(jnp.int32)


def _merge(x, attn_o, gla_o, gr, ga, gg, gt1, sc2, sh2, norm2, wts, *, tm):
    B, T, D = x.shape
    row = lambda w: pl.BlockSpec((None, tm, w), lambda b, t: (b, t, 0))
    vec = pl.BlockSpec((None, 1, D), lambda b, t: (b, 0, 0))
    names = ("wba", "wbg", "wo", "wrh", "wrl", "br")
    ltri = jnp.asarray(np.tril(np.ones((tm, tm), np.float32), -1), BF16)
    return pl.pallas_call(
        _merge_kernel,
        out_shape=[jax.ShapeDtypeStruct((B, T, D), F32), jax.ShapeDtypeStruct((B, T, D), F32),
                   jax.ShapeDtypeStruct((B, T, LANES), jnp.int32), jax.ShapeDtypeStruct((B, T, LANES), jnp.int32),
                   jax.ShapeDtypeStruct((B, T, LANES), F32), jax.ShapeDtypeStruct((1, LANES), jnp.int32)],
        grid=(B, T // tm),
        in_specs=[row(D), row(ATT_W), row(GLA_V_W), row(GLA_V_W),
                  row(GLA_V_W), row(D), row(D), vec, vec, vec,
                  _full((1, D)), _full((1, GLA_V_W)), _full((tm, tm))] + [_full(wts[n].shape) for n in names],
        out_specs=[row(D), row(D), row(LANES), row(LANES), row(LANES), _full((1, LANES))],
        scratch_shapes=[pltpu.VMEM((1, LANES), F32)],
        compiler_params=_cparams(("arbitrary", "arbitrary")),
        name="merge",
    )(x, attn_o, gla_o[0], gla_o[1], gr, ga, gg, gt1, sc2, sh2, norm2, wts["gn"], ltri, *[wts[n] for n in names])


def _dispatch_kernel(ti_ref, rk_ref, ps_ref, zs_ref, nu_ref, h_ref, xs_ref, buf, zbuf, sem, zsem, *, tb, nsteps):
    s = pl.program_id(0)
    nblk = xs_ref.shape[0] // MOE_STEP
    slot = s % 2

    def wait_rows(sl):
        for _ in range(TOP_K):
            pltpu.make_async_copy(buf.at[sl], xs_ref.at[pl.ds(0, tb)], sem.at[sl]).wait()

    @pl.when(s == 0)
    def _():
        zbuf[...] = jnp.zeros_like(zbuf)

        def zstart(e, c):
            z0 = pl.multiple_of(zs_ref[e], MOE_STEP)
            pltpu.make_async_copy(zbuf, xs_ref.at[pl.ds(z0, MOE_STEP)], zsem).start()
            return c

        def zwait(e, c):
            pltpu.make_async_copy(zbuf, xs_ref.at[pl.ds(0, MOE_STEP)], zsem).wait()
            return c

        lax.fori_loop(0, N_EXPERTS, zstart, 0)
        lax.fori_loop(0, N_EXPERTS, zwait, 0)

        def tstart(j, c):
            pltpu.make_async_copy(zbuf, xs_ref.at[pl.ds(pl.multiple_of(j * MOE_STEP, MOE_STEP), MOE_STEP)],
                                  zsem).start()
            return c

        lax.fori_loop(nu_ref[0], nblk, tstart, 0)
        lax.fori_loop(nu_ref[0], nblk, zwait, 0)

    @pl.when(s >= 2)
    def _():
        wait_rows(slot)

    buf[slot] = h_ref[...]

    def issue(r, c):
        for k in range(TOP_K):
            d = ps_ref[ti_ref[0, r * TOP_K + k]] + rk_ref[0, r * TOP_K + k]
            pltpu.make_async_copy(buf.at[slot, pl.ds(r, 1)], xs_ref.at[pl.ds(d, 1)], sem.at[slot]).start()
        return c

    lax.fori_loop(0, tb, issue, 0, unroll=8)

    @pl.when(s == nsteps - 1)
    def _():
        wait_rows(slot)
        if nsteps >= 2:
            wait_rows(1 - slot)


def _dispatch(ti4, rk4, pad_start, zstart, n_used, h2, cap, *, tb):
    n, D = h2.shape
    nsteps = n // tb
    idx = pl.BlockSpec((None, 1, tb * TOP_K), lambda s: (s, 0, 0), memory_space=pltpu.SMEM)
    smem = pl.BlockSpec(memory_space=pltpu.SMEM)
    return pl.pallas_call(
        functools.partial(_dispatch_kernel, tb=tb, nsteps=nsteps),
        out_shape=jax.ShapeDtypeStruct((cap, D), F32),
        grid=(nsteps,),
        in_specs=[idx, idx, smem, smem, smem, pl.BlockSpec((tb, D), lambda s: (s, 0))],
        out_specs=pl.BlockSpec(memory_space=pl.ANY),
        scratch_shapes=[pltpu.VMEM((2, tb, D), F32), pltpu.VMEM((MOE_STEP, D), F32),
                        pltpu.SemaphoreType.DMA((2,)), pltpu.SemaphoreType.DMA(())],
        compiler_params=_cparams(("arbitrary",)),
        name="dispatch",
    )(ti4, rk4, pad_start, zstart, n_used, h2)


def _expert_kernel(be_ref, nv_ref, xs_ref, w1_ref, b1_ref, w2_ref, b2_ref, o_ref, w1b, w2b):
    i = pl.program_id(0)
    e = be_ref[i]
    prev = be_ref[jnp.maximum(i - 1, 0)]

    @pl.when((i == 0) | (e != prev))
    def _():
        w1b[...] = w1_ref[...].astype(BF16)
        w2b[...] = w2_ref[...].astype(BF16)

    def mlp(rows):
        xb = xs_ref[0:rows, :].astype(BF16)
        y = jnp.zeros((rows, D_MODEL), F32)
        fh = D_FF // 2
        for h in range(2):
            g = jnp.dot(xb, w1b[:, h * fh:(h + 1) * fh], preferred_element_type=F32) + b1_ref[:, h * fh:(h + 1) * fh]
            u = (jnp.dot(xb, w1b[:, D_FF + h * fh:D_FF + (h + 1) * fh], preferred_element_type=F32)
                 + b1_ref[:, D_FF + h * fh:D_FF + (h + 1) * fh])
            gate = jnp.minimum(g, SWIGLU_LIMIT)
            up = jnp.clip(u, -SWIGLU_LIMIT, SWIGLU_LIMIT)
            act = gate * (1.0 / (1.0 + jnp.exp(-SWIGLU_ALPHA * gate))) * (up + 1.0)
            y = y + jnp.dot(act.astype(BF16), w2b[h * fh:(h + 1) * fh, :], preferred_element_type=F32)
        o_ref[0:rows, :] = y + b2_ref[...]

    nv = nv_ref[i]
    half = MOE_STEP // 2

    @pl.when(nv > half)
    def _():
        mlp(MOE_STEP)

    @pl.when((nv > 0) & (nv <= half))
    def _():
        mlp(half)
        o_ref[half:, :] = jnp.zeros((MOE_STEP - half, D_MODEL), F32)

    @pl.when(nv == 0)
    def _():
        o_ref[...] = jnp.zeros_like(o_ref)


def _experts(blk_e, nv, xs, w1, b1, w2, b2):
    cap = xs.shape[0]
    n_blk = cap // MOE_STEP
    ne = w1.shape[0]
    gs = pltpu.PrefetchScalarGridSpec(
        num_scalar_prefetch=2, grid=(n_blk,),
        in_specs=[pl.BlockSpec((MOE_STEP, D_MODEL), lambda i, be, nu: (i, 0)),
                  pl.BlockSpec((None, D_MODEL, 2 * D_FF), lambda i, be, nu: (be[i], 0, 0)),
                  pl.BlockSpec((None, 1, 2 * D_FF), lambda i, be, nu: (be[i], 0, 0)),
                  pl.BlockSpec((None, D_FF, D_MODEL), lambda i, be, nu: (be[i], 0, 0)),
                  pl.BlockSpec((None, 1, D_MODEL), lambda i, be, nu: (be[i], 0, 0))],
        out_specs=pl.BlockSpec((MOE_STEP, D_MODEL), lambda i, be, nu: (i, 0)),
        scratch_shapes=[pltpu.VMEM((D_MODEL, 2 * D_FF), BF16), pltpu.VMEM((D_FF, D_MODEL), BF16)])
    return pl.pallas_call(
        _expert_kernel, grid_spec=gs,
        out_shape=jax.ShapeDtypeStruct((cap, D_MODEL), F32),
        compiler_params=_cparams(("arbitrary",)),
        name="experts",
    )(blk_e, nv, xs, w1, b1.reshape(ne, 1, 2 * D_FF), w2, b2.reshape(ne, 1, D_MODEL))


def _combine_kernel(ti_ref, rk_ref, ps_ref, xn_ref, tw_ref, gt2_ref, ys_ref, o_ref, gbuf, sem, *, tb):
    def issue(r, c):
        for k in range(TOP_K):
            d = ps_ref[ti_ref[0, r * TOP_K + k]] + rk_ref[0, r * TOP_K + k]
            pltpu.make_async_copy(ys_ref.at[pl.ds(d, 1)], gbuf.at[k, pl.ds(r, 1)], sem).start()
        return c

    lax.fori_loop(0, tb, issue, 0, unroll=8)
    for k in range(TOP_K):
        pltpu.make_async_copy(ys_ref.at[pl.ds(0, tb)], gbuf.at[k], sem).wait()
    tw = tw_ref[...]
    y = (gbuf[0] * tw[:, 0:1] + gbuf[1] * tw[:, 1:2]) + (gbuf[2] * tw[:, 2:3] + gbuf[3] * tw[:, 3:4])
    o_ref[...] = xn_ref[...] + gt2_ref[...] * y


def _combine(ti4, rk4, pad_start, xn, tw, gt2, ys, *, tb):
    B, T, D = xn.shape
    nt = T // tb
    idx = pl.BlockSpec((None, 1, tb * TOP_K), lambda b, t: (b * nt + t, 0, 0), memory_space=pltpu.SMEM)
    row = lambda w: pl.BlockSpec((None, tb, w), lambda b, t: (b, t, 0))
    return pl.pallas_call(
        functools.partial(_combine_kernel, tb=tb),
        out_shape=jax.ShapeDtypeStruct((B, T, D), F32),
        grid=(B, nt),
        in_specs=[idx, idx, pl.BlockSpec(memory_space=pltpu.SMEM), row(D), row(LANES),
                  pl.BlockSpec((None, 1, D), lambda b, t: (b, 0, 0)), pl.BlockSpec(memory_space=pl.ANY)],
        out_specs=row(D),
        scratch_shapes=[pltpu.VMEM((TOP_K, tb, D), F32), pltpu.SemaphoreType.DMA(())],
        compiler_params=_cparams(("arbitrary", "arbitrary")),
        name="combine",
    )(ti4, rk4, pad_start, xn, tw, gt2, ys)


def _rope_tables(T):
    rows = T // GRID_W
    row = jnp.repeat(jnp.arange(rows, dtype=F32), GRID_W)
    col = jnp.tile(jnp.arange(GRID_W, dtype=F32), rows)
    inv = ROPE_BASE ** (-jnp.arange(0, AXIS_ROT, 2, dtype=F32) / AXIS_ROT)
    ang_r, ang_c = row[:, None] * inv, col[:, None] * inv
    m = AXIS_ROT // 2
    ang = jnp.concatenate([ang_r, ang_r, ang_c, ang_c], axis=1)
    sign = jnp.tile(jnp.concatenate([-jnp.ones((m,), F32), jnp.ones((m,), F32)]), 2)
    cos = jnp.tile(jnp.cos(ang), (1, LANES // HEAD_DIM))
    sin = jnp.tile(jnp.sin(ang) * sign, (1, LANES // HEAD_DIM))
    return cos, sin


def _head_perm():
    order = []
    for m in range(ATT_GROUP):
        for kv in range(ATT_KV_HEADS):
            h = kv * ATT_GROUP + m
            order.extend(range(h * HEAD_DIM, (h + 1) * HEAD_DIM))
    return np.asarray(order)


def kernel(x, c, ctx, c_ctx, w_mod, b_mod, norm1, norm2, w_in, q_norm, k_norm, attn_sink,
           w_alpha_f, b_alpha_f, w_alpha_b, b_alpha_b, gla_norm, w_branch_attn, w_branch_gla,
           w_out, w_router, b_router, w_exp_in, b_exp_in, w_exp_out, b_exp_out):
    B, T, D = x.shape
    depth = w_mod.shape[0]
    assert depth == 1, "single-layer kernel: the context stream update only feeds later layers"
    l = 0
    perm = _head_perm()

    rows = ((B + 1 + 7) // 8) * 8
    c_all = jnp.zeros((rows, D), F32).at[:B].set(c).at[B].set(c_ctx)
    mod = _modulation(c_all, w_mod[l], b_mod[l])
    sh1, sc1, gt1, sh2, sc2, gt2 = [mod[:B, j * D:(j + 1) * D].reshape(B, 1, D) for j in range(6)]
    csh1, csc1 = [jnp.broadcast_to(mod[B, j * D:(j + 1) * D].reshape(1, 1, D), (B, 1, D)) for j in range(2)]

    offs = np.concatenate([[0], np.cumsum(IN_SPLITS)])
    cols = lambda j: w_in[l][:, offs[j]:offs[j + 1]]
    wal = jnp.zeros((2 * GLA_RANK, 2 * GLA_K_W), F32)
    wal = wal.at[:GLA_RANK, :GLA_K_W].set(w_alpha_f[l]).at[GLA_RANK:, GLA_K_W:].set(w_alpha_b[l])
    pw = {
        "wq": cols(0)[:, perm].astype(BF16), "wk": cols(1).astype(BF16), "wv": cols(2).astype(BF16),
        "wgq": cols(3).astype(BF16), "wgk": cols(4).astype(BF16), "wgv": cols(5).astype(BF16),
        "wgr": cols(6).astype(BF16), "wga": cols(9).astype(BF16), "wgg": cols(10).astype(BF16),
        "wlr": jnp.concatenate([cols(7), cols(8)], axis=1).astype(BF16),
        "qn": jnp.tile(q_norm[l], LANES // HEAD_DIM).reshape(1, LANES),
        "kn": jnp.tile(k_norm[l], LANES // HEAD_DIM).reshape(1, LANES),
        "wal": wal.astype(BF16),
        "bal": jnp.concatenate([b_alpha_f[l], b_alpha_b[l]]).reshape(1, 2 * GLA_K_W),
    }
    cos, sin = _rope_tables(T)
    n1 = norm1[l].reshape(1, D)
    tm = min(256, T)
    aq, ak, av, gq, gk, gv, gr, ga, gg, la = _inproj(
        x, sh1, sc1, n1, {"cos": cos, "sin": sin}, pw, rope=True, full=True, tm=tm)
    cak, cav, cgk, cgv, cla = _inproj(
        ctx, csh1, csc1, n1, None, pw, rope=False, full=False, tm=min(256, ctx.shape[1]))

    attn_o = _attention(attn_sink[l], aq, ak, av, cak, cav)
    gla_o = _gla(gq, gk, gv, la, cgk, cgv, cla)

    wr = jnp.zeros((D, LANES), F32).at[:, :N_EXPERTS].set(w_router[l])
    wrh = wr.astype(BF16)
    mw = {
        "gn": jnp.tile(gla_norm[l], GLA_HEADS).reshape(1, GLA_V_W),
        "wba": w_branch_attn[l][perm, :].astype(BF16), "wbg": w_branch_gla[l].astype(BF16),
        "wo": w_out[l].astype(BF16), "wrh": wrh, "wrl": (wr - wrh.astype(F32)).astype(BF16),
        "br": jnp.full((1, LANES), NEG, F32).at[0, :N_EXPERTS].set(b_router[l]),
    }
    xn, h2, ti, rk, tw, cnt = _merge(x, attn_o, gla_o, gr, ga, gg, gt1, sc2, sh2, norm2[l].reshape(1, D), mw, tm=tm)

    n = B * T
    counts = cnt[0, :N_EXPERTS]
    padded = (counts + MOE_STEP - 1) // MOE_STEP * MOE_STEP
    pad_end = jnp.cumsum(padded)
    pad_start = (pad_end - padded).astype(jnp.int32)
    zstart = jnp.maximum(pad_end - MOE_STEP, 0).astype(jnp.int32)
    cap = (n * TOP_K + N_EXPERTS * (MOE_STEP - 1)) // MOE_STEP * MOE_STEP
    n_blk = cap // MOE_STEP
    row0 = jnp.arange(n_blk, dtype=jnp.int32) * MOE_STEP
    blk_e = jnp.minimum(jnp.sum((pad_end[None, :] <= row0[:, None]).astype(jnp.int32), axis=1), N_EXPERTS - 1)
    onehot = (blk_e[:, None] == jnp.arange(N_EXPERTS, dtype=jnp.int32)[None, :]).astype(jnp.int32)
    valid_end = jnp.sum(onehot * (pad_start + counts)[None, :], axis=1)
    nv = jnp.clip(valid_end - row0, 0, MOE_STEP).astype(jnp.int32)
    n_used = (pad_end[-1] // MOE_STEP).astype(jnp.int32).reshape(1)

    tb = min(256, T)
    ti4 = ti[:, :, :TOP_K].reshape(n // tb, 1, tb * TOP_K)
    rk4 = rk[:, :, :TOP_K].reshape(n // tb, 1, tb * TOP_K)
    xs = _dispatch(ti4, rk4, pad_start, zstart, n_used, h2.reshape(n, D), cap, tb=tb)
    ys = _experts(blk_e, nv, xs, w_exp_in[l], b_exp_in[l], w_exp_out[l], b_exp_out[l])
    return _combine(ti4, rk4, pad_start, xn, tw, gt2, ys, tb=tb)
```

```python
import functools

import numpy as np
import jax
import jax.numpy as jnp
from jax import lax
from jax.experimental import pallas as pl
from jax.experimental.pallas import tpu as pltpu

F32 = jnp.float32
BF16 = jnp.bfloat16

D_MODEL = 1024
GRID_W = 64
EPS = 1e-6
ATT_HEADS = 8
ATT_KV_HEADS = 2
ATT_GROUP = ATT_HEADS // ATT_KV_HEADS
HEAD_DIM = 64
WINDOW = 128
ATT_BLOCK = 128
ROPE_BASE = 10000.0
AXIS_ROT = HEAD_DIM // 2
GLA_HEADS = 4
GLA_DK = 64
GLA_DV = 128
GLA_RANK = 16
GLA_TAU = 16.0
N_EXPERTS = 32
TOP_K = 4
D_FF = D_MODEL
SWIGLU_ALPHA = 1.702
SWIGLU_LIMIT = 7.0
MOE_STEP = 512

ATT_W = ATT_HEADS * HEAD_DIM
ATT_KV_W = ATT_KV_HEADS * HEAD_DIM
GLA_K_W = GLA_HEADS * GLA_DK
GLA_V_W = GLA_HEADS * GLA_DV
IN_SPLITS = (ATT_W, ATT_KV_W, ATT_KV_W, GLA_K_W, GLA_K_W, GLA_V_W, GLA_V_W, GLA_RANK, GLA_RANK, D_MODEL, D_MODEL)

LANES = 128
ROW_TILE = D_MODEL // LANES
VMEM_LIMIT = 56 * 1024 * 1024
NEG = -1e30

GLA_C = 64
GLA_SUB = 4
GLA_LEVELS = 4


def _cparams(sem):
    return pltpu.CompilerParams(dimension_semantics=sem, vmem_limit_bytes=VMEM_LIMIT)


def _full(shape):
    n = len(shape)
    return pl.BlockSpec(shape, lambda *_: (0,) * n)


def _mod_kernel(c_ref, w_ref, b_ref, o_ref):
    c = c_ref[...]
    s = c * (1.0 / (1.0 + jnp.exp(-c)))
    o_ref[...] = jnp.dot(s, w_ref[...], preferred_element_type=F32,
                         precision=lax.Precision.HIGHEST) + b_ref[...]


def _modulation(c_all, w_mod, b_mod):
    rows = c_all.shape[0]
    n = w_mod.shape[1]
    tn = 1536
    return pl.pallas_call(
        _mod_kernel,
        out_shape=jax.ShapeDtypeStruct((rows, n), F32),
        grid=(n // tn,),
        in_specs=[pl.BlockSpec((rows, D_MODEL), lambda j: (0, 0)),
                  pl.BlockSpec((D_MODEL, tn), lambda j: (0, j)),
                  pl.BlockSpec((1, tn), lambda j: (0, j))],
        out_specs=pl.BlockSpec((rows, tn), lambda j: (0, j)),
        compiler_params=_cparams(("arbitrary",)),
        name="mod",
    )(c_all, w_mod, b_mod.reshape(1, n))


def _pair_norm(a, g, lo):
    s = a * a
    tot = jnp.sum(s, axis=-1, keepdims=True)
    slo = jnp.sum(jnp.where(lo, s, 0.0), axis=-1, keepdims=True)
    ms = jnp.where(lo, slo, tot - slo) * (1.0 / HEAD_DIM)
    return a * lax.rsqrt(ms + EPS) * g


def _rope(y, cos, sin, first):
    up = pltpu.roll(y, LANES - AXIS_ROT // 2, 1)
    dn = pltpu.roll(y, AXIS_ROT // 2, 1)
    return y * cos + jnp.where(first, up, dn) * sin


def _inproj_kernel(*refs, rope, full):
    if full:
        (x_ref, sh_ref, sc_ref, n1_ref, cos_ref, sin_ref, qn_ref, kn_ref, wal_ref, bal_ref,
         wq, wk, wv, wgq, wgk, wgv, wgr, wga, wgg, wlr,
         oq, ok, ov, ogq, ogk, ogv, ogr, oga, ogg, ola) = refs
    else:
        (x_ref, sh_ref, sc_ref, n1_ref, kn_ref, wal_ref, bal_ref,
         wk, wv, wgk, wgv, wlr,
         ok, ov, ogk, ogv, ola) = refs
    x = x_ref[...]
    tm = x.shape[0]
    ms = jnp.mean(x * x, axis=-1, keepdims=True)
    h = (x * lax.rsqrt(ms + EPS) * n1_ref[...]) * (1.0 + sc_ref[...]) + sh_ref[...]
    hb = h.astype(BF16)

    def proj(w_ref):
        return jnp.dot(hb, w_ref[...], preferred_element_type=F32)

    lane = lax.broadcasted_iota(jnp.int32, (tm, LANES), 1)
    lo = lane < HEAD_DIM
    first = (lane % AXIS_ROT) < (AXIS_ROT // 2)
    if rope:
        cos = cos_ref[...]
        sin = sin_ref[...]

    k = _pair_norm(proj(wk), kn_ref[...], lo)
    if rope:
        k = _rope(k, cos, sin, first)
    ok[...] = k.astype(BF16)
    ov[...] = proj(wv).astype(BF16)
    ogk[...] = proj(wgk).astype(BF16)
    ogv[...] = proj(wgv).astype(BF16)
    lr = proj(wlr).astype(BF16)
    z = jnp.dot(lr, wal_ref[...], preferred_element_type=F32) + bal_ref[...]
    ola[...] = (jnp.minimum(z, 0.0) - jnp.log(1.0 + jnp.exp(-jnp.abs(z)))) * (1.0 / GLA_TAU)
    if full:
        q = proj(wq)
        for p in range(ATT_W // LANES):
            y = _pair_norm(q[:, p * LANES:(p + 1) * LANES], qn_ref[...], lo)
            if rope:
                y = _rope(y, cos, sin, first)
            oq[:, p * LANES:(p + 1) * LANES] = (y * HEAD_DIM ** -0.5).astype(BF16)
        ogq[...] = (proj(wgq) * GLA_DK ** -0.5).astype(BF16)
        g = proj(wgr)
        ogr[...] = (g * (1.0 / (1.0 + jnp.exp(-g)))).astype(BF16)
        oga[...] = (1.0 / (1.0 + jnp.exp(-proj(wga)))).astype(BF16)
        ogg[...] = (1.0 / (1.0 + jnp.exp(-proj(wgg)))).astype(BF16)


def _inproj(x, sh, sc, norm1, tabs, wts, *, rope, full, tm):
    B, T, D = x.shape
    grid = (B, T // tm)
    row = lambda w: pl.BlockSpec((None, tm, w), lambda b, t: (b, t, 0))
    vec = pl.BlockSpec((None, 1, D), lambda b, t: (b, 0, 0))
    tab = pl.BlockSpec((tm, LANES), lambda b, t: (t, 0))
    if full:
        names = ("wq", "wk", "wv", "wgq", "wgk", "wgv", "wgr", "wga", "wgg", "wlr")
        ins = [x, sh, sc, norm1, tabs["cos"], tabs["sin"], wts["qn"], wts["kn"], wts["wal"], wts["bal"]]
        specs = [row(D), vec, vec, _full((1, D)), tab, tab, _full((1, LANES)), _full((1, LANES)),
                 _full(wts["wal"].shape), _full(wts["bal"].shape)]
        out_w = (ATT_W, ATT_KV_W, ATT_KV_W, GLA_K_W, GLA_K_W, GLA_V_W, GLA_V_W, D, D)
    else:
        names = ("wk", "wv", "wgk", "wgv", "wlr")
        ins = [x, sh, sc, norm1, wts["kn"], wts["wal"], wts["bal"]]
        specs = [row(D), vec, vec, _full((1, D)), _full((1, LANES)),
                 _full(wts["wal"].shape), _full(wts["bal"].shape)]
        out_w = (ATT_KV_W, ATT_KV_W, GLA_K_W, GLA_V_W)
    ins += [wts[n] for n in names]
    specs += [_full(wts[n].shape) for n in names]
    out_shape = [jax.ShapeDtypeStruct((B, T, w), BF16) for w in out_w]
    out_shape.append(jax.ShapeDtypeStruct((B, T, 2 * GLA_K_W), F32))
    out_specs = [row(w) for w in out_w] + [row(2 * GLA_K_W)]
    return pl.pallas_call(
        functools.partial(_inproj_kernel, rope=rope, full=full),
        out_shape=out_shape, grid=grid, in_specs=specs, out_specs=out_specs,
        compiler_params=_cparams(("parallel", "arbitrary")),
        name="inproj_full" if full else "inproj_ctx",
    )(*ins)


def _attn_kernel(sink_ref, q_ref, k0_ref, k1_ref, k2_ref, k3_ref, kx_ref, v0_ref, v1_ref, v2_ref, v3_ref,
                 vx_ref, o_ref, *, seq):
    n = pl.program_id(1)
    blk = ATT_BLOCK
    lc = kx_ref.shape[0]
    nk = 3 * blk + lc
    nslab = ATT_W // LANES
    rows = nslab * blk
    kblocks = (k0_ref, k1_ref, k2_ref, k3_ref)
    vblocks = (v0_ref, v1_ref, v2_ref, v3_ref)
    ri = lax.broadcasted_iota(jnp.int32, (rows, nk), 0)
    kj = lax.broadcasted_iota(jnp.int32, (rows, nk), 1)
    qi = ri % blk
    lane = lax.broadcasted_iota(jnp.int32, (blk, LANES), 1)
    lo = lane < HEAD_DIM
    hrow = lax.broadcasted_iota(jnp.int32, (rows, 1), 0) // blk
    for sb in range(2):
        kcat = jnp.concatenate([r[...] for r in kblocks[sb:sb + 3]] + [kx_ref[...]], axis=0)
        vcat = jnp.concatenate([r[...] for r in vblocks[sb:sb + 3]] + [vx_ref[...]], axis=0)
        kpos = (2 * n + sb - 1) * blk + kj
        valid = ((jnp.abs(kj - blk - qi) <= WINDOW) & (kpos >= 0) & (kpos < seq)) | (kj >= 3 * blk)
        q = q_ref[sb * blk:(sb + 1) * blk, :]
        outs = []
        for kv in range(ATT_KV_HEADS):
            keep = lo if kv == 0 else jnp.logical_not(lo)
            qs = jnp.concatenate([jnp.where(keep, q[:, m * LANES:(m + 1) * LANES], jnp.zeros((blk, LANES), BF16))
                                  for m in range(nslab)], axis=0)
            s = lax.dot_general(qs, kcat, (((1,), (1,)), ((), ())), preferred_element_type=F32)
            s = jnp.where(valid, s, NEG)
            snk = jnp.zeros((rows, 1), F32)
            for m in range(nslab):
                snk = jnp.where(hrow == m, sink_ref[kv * ATT_GROUP + m], snk)
            mx = jnp.maximum(jnp.max(s, axis=-1, keepdims=True), snk)
            p = jnp.exp(s - mx)
            den = jnp.sum(p, axis=-1, keepdims=True) + jnp.exp(snk - mx)
            outs.append(jnp.dot(p.astype(BF16), vcat, preferred_element_type=F32) / den)
        for m in range(nslab):
            o_ref[sb * blk:(sb + 1) * blk, m * LANES:(m + 1) * LANES] = jnp.where(
                lo, outs[0][m * blk:(m + 1) * blk], outs[1][m * blk:(m + 1) * blk]).astype(BF16)


def _attention(sink, aq, ak, av, cak, cav):
    B, T, _ = aq.shape
    lc = cak.shape[1]
    blk = ATT_BLOCK
    nb = T // blk
    assert nb % 2 == 0
    kvspec = lambda off: pl.BlockSpec((None, blk, ATT_KV_W),
                                      lambda b, n: (b, jnp.clip(2 * n + off, 0, nb - 1), 0))
    cspec = pl.BlockSpec((None, lc, ATT_KV_W), lambda b, n: (b, 0, 0))
    qspec = pl.BlockSpec((None, 2 * blk, ATT_W), lambda b, n: (b, n, 0))
    kvs = [kvspec(off) for off in (-1, 0, 1, 2)]
    return pl.pallas_call(
        functools.partial(_attn_kernel, seq=T),
        out_shape=jax.ShapeDtypeStruct((B, T, ATT_W), BF16),
        grid=(B, nb // 2),
        in_specs=[pl.BlockSpec(memory_space=pltpu.SMEM), qspec] + kvs + [cspec] + kvs + [cspec],
        out_specs=qspec,
        compiler_params=_cparams(("parallel", "arbitrary")),
        name="attn",
    )(sink, aq, ak, ak, ak, ak, cak, av, av, av, av, cav)


def _gla_constants():
    C, sub, L = GLA_C, GLA_SUB, GLA_LEVELS
    i = np.arange(C)[:, None]
    t = np.arange(C)[None, :]
    tabs = [t <= i, t > i]
    rowq, same = [], []
    for l in range(L):
        s = C >> l
        mid = (i // s) * s + s // 2
        isq = i >= mid
        tabs.append(np.where(isq, (t >= mid) & (t <= i), (t > i) & (t < mid)))
        rowq.append(np.broadcast_to(isq, (C, C)))
        same.append((i // s) == (t // s))
    shifts, dmask = [], [t == i]
    for d in range(1, sub):
        ok = (i % sub) >= d
        tabs.append(ok & (t > i - d) & (t <= i))
        shifts.append(ok & (t == i - d))
        dmask.append(ok & (t == i - d))
    tabs.append(np.ones((8, C), bool))
    flip = lambda a: a[::-1, ::-1]
    tile = lambda a: np.tile(a, (1, GLA_HEADS))

    def both(xs, lanes):
        f = (lambda a: tile(a)) if lanes else (lambda a: a)
        return np.stack([np.concatenate([f(a) for a in xs], 0),
                         np.concatenate([f(flip(a)) for a in xs], 0)]).astype(np.float32)

    hk = np.arange(GLA_K_W) // GLA_DK
    hv = np.arange(GLA_V_W) // GLA_DV
    ind = (hk[:, None] == hk[None, :]).astype(np.float32)
    bdv = (hk[:, None] == hv[None, :]).astype(np.float32)
    return (both(tabs, False), both(shifts, False), both(rowq, True), both(same, True),
            both(dmask, True), ind, bdv, np.ascontiguousarray(bdv.T))


def _gla_chunk(q_b, k_b, v_b, la, cst, d):
    mtab_ref, shm_ref, rq_ref, sm_ref, dm_ref, ind_ref, bdv_ref, bds_ref = cst
    C = GLA_C
    kw = GLA_K_W
    q = q_b.astype(F32)
    k = k_b.astype(F32)
    hi = la.astype(BF16)
    r1 = la - hi.astype(F32)
    mid = r1.astype(BF16)
    lo = (r1 - mid.astype(F32)).astype(BF16)
    g3 = jnp.dot(mtab_ref[d], jnp.concatenate([hi, mid, lo], axis=1), preferred_element_type=F32)
    e = jnp.exp(g3[:, :kw] + g3[:, kw:2 * kw] + g3[:, 2 * kw:])

    qt = (q * e[0:C]).astype(BF16)
    kt = (k * e[C:2 * C]).astype(BF16)
    nt = 2 + GLA_LEVELS + GLA_SUB - 1
    gamma = e[nt * C:nt * C + 1]

    ind = ind_ref[...]
    a = jnp.zeros((C, kw), F32)
    for l in range(GLA_LEVELS):
        el = e[(2 + l) * C:(3 + l) * C]
        rq = rq_ref[d, l * C:(l + 1) * C, :]
        qh = (q * (el * rq)).astype(BF16)
        kh = (k * (el * (1.0 - rq))).astype(BF16)
        bdk = jnp.concatenate([kh] * GLA_HEADS, axis=0) * ind
        al = lax.dot_general(qh, bdk, (((1,), (1,)), ((), ())), preferred_element_type=F32)
        a = a + al * sm_ref[d, l * C:(l + 1) * C, :]
    ksh = jnp.dot(shm_ref[d], k_b, preferred_element_type=F32)
    for j in range(GLA_SUB):
        if j == 0:
            p = q * k
        else:
            p = q * ksh[(j - 1) * C:j * C] * e[(2 + GLA_LEVELS + j - 1) * C:(2 + GLA_LEVELS + j) * C]
        w = jnp.dot(p.astype(BF16), ind, preferred_element_type=F32)
        a = a + w * dm_ref[d, j * C:(j + 1) * C, :]

    bdv = jnp.concatenate([v_b] * GLA_HEADS, axis=0) * bdv_ref[...]
    o_intra = jnp.dot(a.astype(BF16), bdv, preferred_element_type=F32)
    upd = lax.dot_general(v_b, kt, (((0,), (0,)), ((), ())), preferred_element_type=F32) * bds_ref[...]
    return o_intra, qt, upd, gamma


def _gla_kernel(qf_ref, kf_ref, vf_ref, laf_ref, qb_ref, kb_ref, vb_ref, lab_ref,
                ckf_ref, cvf_ref, claf_ref, ckb_ref, cvb_ref, clab_ref,
                mtab_ref, shm_ref, rq_ref, sm_ref, dm_ref, ind_ref, bdv_ref, bds_ref,
                of_ref, ob_ref, st_ref, *, n_ctx_steps):
    s = pl.program_id(1)
    C = GLA_C
    cst = (mtab_ref, shm_ref, rq_ref, sm_ref, dm_ref, ind_ref, bdv_ref, bds_ref)

    @pl.when(s == 0)
    def _():
        st_ref[...] = jnp.zeros_like(st_ref)

    is_ctx = s < n_ctx_steps
    dirs = ((0, qf_ref, kf_ref, vf_ref, laf_ref, ckf_ref, cvf_ref, claf_ref, of_ref),
            (1, qb_ref, kb_ref, vb_ref, lab_ref, ckb_ref, cvb_ref, clab_ref, ob_ref))
    work = []
    for d, q_ref, k_ref, v_ref, la_ref, ck_ref, cv_ref, cla_ref, o_ref in dirs:
        order = (0, 1) if d == 0 else (1, 0)
        for c in order:
            rows = slice(c * C, (c + 1) * C)
            k_b = jnp.where(is_ctx, ck_ref[rows, :], k_ref[rows, :])
            v_b = jnp.where(is_ctx, cv_ref[rows, :], v_ref[rows, :])
            la = jnp.where(is_ctx, cla_ref[rows, :], la_ref[rows, :])
            work.append((d, rows, o_ref, _gla_chunk(q_ref[rows, :], k_b, v_b, la, cst, d)))
    states = [st_ref[0], st_ref[1]]
    for d, rows, o_ref, (o_intra, qt, upd, gamma) in work:
        st = states[d]
        o_ref[rows, :] = o_intra + lax.dot_general(qt, st.astype(BF16), (((1,), (1,)), ((), ())),
                                                   preferred_element_type=F32)
        states[d] = st * gamma + upd
    st_ref[0] = states[0]
    st_ref[1] = states[1]


def _gla(gq, gk, gv, la, cgk, cgv, cla):
    B, T, _ = gq.shape
    lc = cgk.shape[1]
    R = 2 * GLA_C
    assert lc % R == 0 and T % R == 0
    n_ctx, n_lat = lc // R, T // R
    consts = _gla_constants()
    mtab, shm = jnp.asarray(consts[0], BF16), jnp.asarray(consts[1], BF16)
    rq, sm, dm = [jnp.asarray(c) for c in consts[2:5]]
    ind, bdv = jnp.asarray(consts[5], BF16), jnp.asarray(consts[6], BF16)
    bds = jnp.asarray(consts[7])

    def lat(s, d):
        j = jnp.maximum(s - n_ctx, 0)
        return j if d == 0 else n_lat - 1 - j

    def ctx(s, d):
        j = jnp.minimum(s, n_ctx - 1)
        return j if d == 0 else n_ctx - 1 - j

    lspec = lambda w, d, c=0: pl.BlockSpec((None, R, w), lambda b, s: (b, lat(s, d), c))
    cspec = lambda w, d, c=0: pl.BlockSpec((None, R, w), lambda b, s: (b, ctx(s, d), c))
    lat_specs = lambda d: [lspec(GLA_K_W, d), lspec(GLA_K_W, d), lspec(GLA_V_W, d), lspec(GLA_K_W, d, d)]
    ctx_specs = lambda d: [cspec(GLA_K_W, d), cspec(GLA_V_W, d), cspec(GLA_K_W, d, d)]
    cs = [mtab, shm, rq, sm, dm, ind, bdv, bds]
    return pl.pallas_call(
        functools.partial(_gla_kernel, n_ctx_steps=n_ctx),
        out_shape=[jax.ShapeDtypeStruct((B, T, GLA_V_W), F32)] * 2,
        grid=(B, n_ctx + n_lat),
        in_specs=lat_specs(0) + lat_specs(1) + ctx_specs(0) + ctx_specs(1) + [_full(c.shape) for c in cs],
        out_specs=[lspec(GLA_V_W, 0), lspec(GLA_V_W, 1)],
        scratch_shapes=[pltpu.VMEM((2, GLA_V_W, GLA_K_W), F32)],
        compiler_params=_cparams(("parallel", "arbitrary")),
        name="gla",
    )(gq, gk, gv, la, gq, gk, gv, la, cgk, cgv, cla, cgk, cgv, cla, *cs)


def _merge_kernel(x_ref, at_ref, of_ref, ob_ref, gr_ref, ga_ref, gg_ref, gt1_ref, sc2_ref, sh2_ref,
                  n2_ref, gn_ref, ltri_ref, wba_ref, wbg_ref, wo_ref, wrh_ref, wrl_ref, br_ref,
                  xn_ref, h2_ref, ti_ref, rk_ref, tw_ref, cnt_ref, carry_ref):
    tm = x_ref.shape[0]

    @pl.when((pl.program_id(0) == 0) & (pl.program_id(1) == 0))
    def _():
        carry_ref[...] = jnp.zeros_like(carry_ref)

    go = of_ref[...] + ob_ref[...]
    parts = []
    for h in range(GLA_HEADS):
        gh = go[:, h * GLA_DV:(h + 1) * GLA_DV]
        ms = jnp.mean(gh * gh, axis=-1, keepdims=True)
        parts.append(gh * lax.rsqrt(ms + EPS))
    o = jnp.concatenate(parts, axis=1) * gn_ref[...] * gr_ref[...].astype(F32)
    ya = jnp.dot(at_ref[...], wba_ref[...], preferred_element_type=F32)
    yg = jnp.dot(o.astype(BF16), wbg_ref[...], preferred_element_type=F32)
    y = ga_ref[...].astype(F32) * ya + gg_ref[...].astype(F32) * yg
    z = jnp.dot(y.astype(BF16), wo_ref[...], preferred_element_type=F32)
    xn = x_ref[...] + gt1_ref[...] * z
    xn_ref[...] = xn
    ms = jnp.mean(xn * xn, axis=-1, keepdims=True)
    h2 = (xn * lax.rsqrt(ms + EPS) * n2_ref[...]) * (1.0 + sc2_ref[...]) + sh2_ref[...]
    hh = h2.astype(BF16)
    hl = (h2 - hh.astype(F32)).astype(BF16)
    h2_ref[...] = h2
    logits = (jnp.dot(hh, wrh_ref[...], preferred_element_type=F32)
              + jnp.dot(hl, wrh_ref[...], preferred_element_type=F32)
              + jnp.dot(hh, wrl_ref[...], preferred_element_type=F32)) + br_ref[...]
    lane = lax.broadcasted_iota(jnp.int32, (tm, LANES), 1).astype(F32)
    vals, idxs = [], []
    l = logits
    for _ in range(TOP_K):
        m = jnp.max(l, axis=-1, keepdims=True)
        ix = jnp.min(jnp.where(l == m, lane, float(LANES)), axis=-1, keepdims=True)
        vals.append(m)
        idxs.append(ix)
        l = jnp.where(lane == ix, -3.0e38, l)
    ex = [jnp.exp(v - vals[0]) for v in vals]
    den = ex[0] + ex[1] + ex[2] + ex[3]
    mh = jnp.zeros((tm, LANES), F32)
    for j in range(TOP_K):
        mh = mh + jnp.where(lane == idxs[j], 1.0, 0.0)
    pc = jnp.dot(ltri_ref[...], mh.astype(BF16), preferred_element_type=F32) + carry_ref[...]
    ti = jnp.zeros((tm, LANES), F32)
    rk = jnp.zeros((tm, LANES), F32)
    tw = jnp.zeros((tm, LANES), F32)
    for j in range(TOP_K):
        rj = jnp.sum(jnp.where(lane == idxs[j], pc, 0.0), axis=-1, keepdims=True)
        ti = jnp.where(lane == float(j), idxs[j], ti)
        rk = jnp.where(lane == float(j), rj, rk)
        tw = jnp.where(lane == float(j), ex[j] / den, tw)
    ti_ref[...] = ti.astype(jnp.int32)
    rk_ref[...] = rk.astype(jnp.int32)
    tw_ref[...] = tw
    total = carry_ref[...] + jnp.sum(mh, axis=0, keepdims=True)
    carry_ref[...] = total
    cnt_ref[...] = total.astype(jnp.int32)


def _merge(x, attn_o, gla_o, gr, ga, gg, gt1, sc2, sh2, norm2, wts, *, tm):
    B, T, D = x.shape
    row = lambda w: pl.BlockSpec((None, tm, w), lambda b, t: (b, t, 0))
    vec = pl.BlockSpec((None, 1, D), lambda b, t: (b, 0, 0))
    names = ("wba", "wbg", "wo", "wrh", "wrl", "br")
    ltri = jnp.asarray(np.tril(np.ones((tm, tm), np.float32), -1), BF16)
    return pl.pallas_call(
        _merge_kernel,
        out_shape=[jax.ShapeDtypeStruct((B, T, D), F32), jax.ShapeDtypeStruct((B, T, D), F32),
                   jax.ShapeDtypeStruct((B, T, LANES), jnp.int32), jax.ShapeDtypeStruct((B, T, LANES), jnp.int32),
                   jax.ShapeDtypeStruct((B, T, LANES), F32), jax.ShapeDtypeStruct((1, LANES), jnp.int32)],
        grid=(B, T // tm),
        in_specs=[row(D), row(ATT_W), row(GLA_V_W), row(GLA_V_W),
                  row(GLA_V_W), row(D), row(D), vec, vec, vec,
                  _full((1, D)), _full((1, GLA_V_W)), _full((tm, tm))] + [_full(wts[n].shape) for n in names],
        out_specs=[row(D), row(D), row(LANES), row(LANES), row(LANES), _full((1, LANES))],
        scratch_shapes=[pltpu.VMEM((1, LANES), F32)],
        compiler_params=_cparams(("arbitrary", "arbitrary")),
        name="merge",
    )(x, attn_o, gla_o[0], gla_o[1], gr, ga, gg, gt1, sc2, sh2, norm2, wts["gn"], ltri, *[wts[n] for n in names])


def _dispatch_kernel(ti_ref, rk_ref, ps_ref, zs_ref, nu_ref, h_ref, xs_ref, buf, zbuf, isem, sem, zsem,
                     *, tb, nsteps):
    s = pl.program_id(0)
    nblk = xs_ref.shape[0] // MOE_STEP
    slot = s % 3
    nxt = (s + 1) % 3

    def loads(step, sl):
        r0 = pl.multiple_of(step * tb, tb)
        return [pltpu.make_async_copy(h_ref.at[pl.ds(r0, tb), pl.ds(j * LANES, LANES)], buf.at[sl, :, j, :],
                                      isem.at[sl]) for j in range(ROW_TILE)]

    def wait_rows(sl):
        for _ in range(TOP_K):
            pltpu.make_async_copy(buf.at[sl], xs_ref.at[pl.ds(0, tb)], sem.at[sl]).wait()

    @pl.when(s == 0)
    def _():
        zbuf[...] = jnp.zeros_like(zbuf)

        def zstart(e, c):
            z0 = pl.multiple_of(zs_ref[e], MOE_STEP)
            pltpu.make_async_copy(zbuf, xs_ref.at[pl.ds(z0, MOE_STEP)], zsem).start()
            return c

        def zwait(e, c):
            pltpu.make_async_copy(zbuf, xs_ref.at[pl.ds(0, MOE_STEP)], zsem).wait()
            return c

        lax.fori_loop(0, N_EXPERTS, zstart, 0)
        lax.fori_loop(0, N_EXPERTS, zwait, 0)

        def tstart(j, c):
            pltpu.make_async_copy(zbuf, xs_ref.at[pl.ds(pl.multiple_of(j * MOE_STEP, MOE_STEP), MOE_STEP)],
                                  zsem).start()
            return c

        lax.fori_loop(nu_ref[0], nblk, tstart, 0)
        lax.fori_loop(nu_ref[0], nblk, zwait, 0)
        for c in loads(0, 0):
            c.start()

    for c in loads(s, slot):
        c.wait()

    @pl.when(s + 1 < nsteps)
    def _():
        @pl.when(s >= 2)
        def _():
            wait_rows(nxt)
        for c in loads(s + 1, nxt):
            c.start()

    def issue(r, c):
        for k in range(TOP_K):
            d = ps_ref[ti_ref[0, r * TOP_K + k]] + rk_ref[0, r * TOP_K + k]
            pltpu.make_async_copy(buf.at[slot, r], xs_ref.at[d], sem.at[slot]).start(priority=k % 2)
        return c

    lax.fori_loop(0, tb, issue, 0, unroll=8)

    @pl.when(s == nsteps - 1)
    def _():
        wait_rows(slot)
        if nsteps >= 2:
            wait_rows((s + 2) % 3)
        if nsteps >= 3:
            wait_rows(nxt)


def _dispatch(ti4, rk4, pad_start, zstart, n_used, h2, cap, *, tb):
    n, D = h2.shape
    nsteps = n // tb
    idx = pl.BlockSpec((None, 1, tb * TOP_K), lambda s: (s, 0, 0), memory_space=pltpu.SMEM)
    smem = pl.BlockSpec(memory_space=pltpu.SMEM)
    anyspec = pl.BlockSpec(memory_space=pl.ANY)
    return pl.pallas_call(
        functools.partial(_dispatch_kernel, tb=tb, nsteps=nsteps),
        out_shape=jax.ShapeDtypeStruct((cap, ROW_TILE, LANES), F32),
        grid=(nsteps,),
        in_specs=[idx, idx, smem, smem, smem, anyspec],
        out_specs=anyspec,
        scratch_shapes=[pltpu.VMEM((3, tb, ROW_TILE, LANES), F32), pltpu.VMEM((MOE_STEP, ROW_TILE, LANES), F32),
                        pltpu.SemaphoreType.DMA((3,)), pltpu.SemaphoreType.DMA((3,)), pltpu.SemaphoreType.DMA(())],
        compiler_params=_cparams(("arbitrary",)),
        name="dispatch",
    )(ti4, rk4, pad_start, zstart, n_used, h2)


def _expert_kernel(be_ref, nv_ref, xs_ref, w1_ref, b1_ref, w2_ref, b2_ref, ys_ref, w1b, w2b, xin, yout,
                   isem, osem, *, nsteps):
    i = pl.program_id(0)
    slot = i % 2
    e = be_ref[i]
    prev = be_ref[jnp.maximum(i - 1, 0)]

    def loads(step, sl):
        r0 = pl.multiple_of(step * MOE_STEP, MOE_STEP)
        return [pltpu.make_async_copy(xs_ref.at[pl.ds(r0, MOE_STEP), j, :],
                                      xin.at[sl, :, pl.ds(j * LANES, LANES)], isem.at[sl]) for j in range(ROW_TILE)]

    def stores(step, sl):
        r0 = pl.multiple_of(step * MOE_STEP, MOE_STEP)
        return [pltpu.make_async_copy(yout.at[sl, :, pl.ds(j * LANES, LANES)],
                                      ys_ref.at[pl.ds(r0, MOE_STEP), j, :], osem.at[sl]) for j in range(ROW_TILE)]

    @pl.when(i == 0)
    def _():
        for c in loads(0, 0):
            c.start()

    @pl.when((i == 0) | (e != prev))
    def _():
        w1b[...] = w1_ref[...].astype(BF16)
        w2b[...] = w2_ref[...].astype(BF16)

    for c in loads(i, slot):
        c.wait()

    @pl.when(i + 1 < nsteps)
    def _():
        for c in loads(i + 1, 1 - slot):
            c.start()

    @pl.when(i >= 2)
    def _():
        for c in stores(i - 2, slot):
            c.wait()

    def mlp(rows):
        xb = xin[slot, 0:rows, :].astype(BF16)
        y = jnp.zeros((rows, D_MODEL), F32)
        fh = D_FF // 2
        for h in range(2):
            g = jnp.dot(xb, w1b[:, h * fh:(h + 1) * fh], preferred_element_type=F32) + b1_ref[:, h * fh:(h + 1) * fh]
            u = (jnp.dot(xb, w1b[:, D_FF + h * fh:D_FF + (h + 1) * fh], preferred_element_type=F32)
                 + b1_ref[:, D_FF + h * fh:D_FF + (h + 1) * fh])
            gate = jnp.minimum(g, SWIGLU_LIMIT)
            up = jnp.clip(u, -SWIGLU_LIMIT, SWIGLU_LIMIT)
            act = gate * (1.0 / (1.0 + jnp.exp(-SWIGLU_ALPHA * gate))) * (up + 1.0)
            y = y + jnp.dot(act.astype(BF16), w2b[h * fh:(h + 1) * fh, :], preferred_element_type=F32)
        yout[slot, 0:rows, :] = y + b2_ref[...]

    nv = nv_ref[i]
    half = MOE_STEP // 2

    @pl.when(nv > half)
    def _():
        mlp(MOE_STEP)

    @pl.when((nv > 0) & (nv <= half))
    def _():
        mlp(half)
        yout[slot, half:, :] = jnp.zeros((MOE_STEP - half, D_MODEL), F32)

    @pl.when(nv == 0)
    def _():
        yout[slot] = jnp.zeros((MOE_STEP, D_MODEL), F32)

    for c in stores(i, slot):
        c.start()

    @pl.when(i == nsteps - 1)
    def _():
        for c in stores(i, slot):
            c.wait()
        if nsteps >= 2:
            for c in stores(i - 1, 1 - slot):
                c.wait()


def _experts(blk_e, nv, xs, w1, b1, w2, b2):
    cap = xs.shape[0]
    n_blk = cap // MOE_STEP
    ne = w1.shape[0]
    anyspec = pl.BlockSpec(memory_space=pl.ANY)
    gs = pltpu.PrefetchScalarGridSpec(
        num_scalar_prefetch=2, grid=(n_blk,),
        in_specs=[anyspec,
                  pl.BlockSpec((None, D_MODEL, 2 * D_FF), lambda i, be, nu: (be[i], 0, 0)),
                  pl.BlockSpec((None, 1, 2 * D_FF), lambda i, be, nu: (be[i], 0, 0)),
                  pl.BlockSpec((None, D_FF, D_MODEL), lambda i, be, nu: (be[i], 0, 0)),
                  pl.BlockSpec((None, 1, D_MODEL), lambda i, be, nu: (be[i], 0, 0))],
        out_specs=anyspec,
        scratch_shapes=[pltpu.VMEM((D_MODEL, 2 * D_FF), BF16), pltpu.VMEM((D_FF, D_MODEL), BF16),
                        pltpu.VMEM((2, MOE_STEP, D_MODEL), F32), pltpu.VMEM((2, MOE_STEP, D_MODEL), F32),
                        pltpu.SemaphoreType.DMA((2,)), pltpu.SemaphoreType.DMA((2,))])
    return pl.pallas_call(
        functools.partial(_expert_kernel, nsteps=n_blk), grid_spec=gs,
        out_shape=jax.ShapeDtypeStruct((cap, ROW_TILE, LANES), F32),
        compiler_params=_cparams(("arbitrary",)),
        name="experts",
    )(blk_e, nv, xs, w1, b1.reshape(ne, 1, 2 * D_FF), w2, b2.reshape(ne, 1, D_MODEL))


def _combine_kernel(tic_ref, rkc_ref, tin_ref, rkn_ref, tw_ref, ps_ref, gt2_ref, xn_ref, ys_ref, o_ref,
                    gbuf, xt, ot, gsem, xsem, osem, *, tb, nsteps):
    s = pl.program_id(0)
    slot = s % 2
    other = 1 - slot

    def xloads(step, sl):
        r0 = pl.multiple_of(step * tb, tb)
        return [pltpu.make_async_copy(xn_ref.at[pl.ds(r0, tb), pl.ds(j * LANES, LANES)], xt.at[sl, :, j, :],
                                      xsem.at[sl]) for j in range(ROW_TILE)]

    def ostores(step, sl):
        r0 = pl.multiple_of(step * tb, tb)
        return [pltpu.make_async_copy(ot.at[sl, :, j, :], o_ref.at[pl.ds(r0, tb), pl.ds(j * LANES, LANES)],
                                      osem.at[sl]) for j in range(ROW_TILE)]

    def gathers(ti_ref, rk_ref, sl):
        def issue(r, c):
            for k in range(TOP_K):
                d = ps_ref[ti_ref[0, r * TOP_K + k]] + rk_ref[0, r * TOP_K + k]
                pltpu.make_async_copy(ys_ref.at[d], gbuf.at[sl, k, r], gsem.at[sl]).start(priority=k % 2)
            return c

        lax.fori_loop(0, tb, issue, 0, unroll=8)

    @pl.when(s == 0)
    def _():
        gathers(tic_ref, rkc_ref, 0)
        for c in xloads(0, 0):
            c.start()

    @pl.when(s + 1 < nsteps)
    def _():
        gathers(tin_ref, rkn_ref, other)
        for c in xloads(s + 1, other):
            c.start()

    for k in range(TOP_K):
        pltpu.make_async_copy(ys_ref.at[pl.ds(0, tb)], gbuf.at[slot, k], gsem.at[slot]).wait()
    for c in xloads(s, slot):
        c.wait()

    @pl.when(s >= 2)
    def _():
        for c in ostores(s - 2, slot):
            c.wait()

    g2 = gt2_ref[...]

    def wsum(r, c):
        acc = tw_ref[0, r * TOP_K] * gbuf[slot, 0, r]
        for k in range(1, TOP_K):
            acc = acc + tw_ref[0, r * TOP_K + k] * gbuf[slot, k, r]
        ot[slot, r] = xt[slot, r] + g2 * acc
        return c

    lax.fori_loop(0, tb, wsum, 0, unroll=8)
    for c in ostores(s, slot):
        c.start()

    @pl.when(s == nsteps - 1)
    def _():
        for c in ostores(s, slot):
            c.wait()
        if nsteps >= 2:
            for c in ostores(s - 1, other):
                c.wait()


def _combine(ti4, rk4, tw4, pad_start, gt2t, xn, ys, *, tb, seq):
    n, D = xn.shape
    nsteps = n // tb
    cur = lambda s: (s, 0, 0)
    nxt = lambda s: (jnp.minimum(s + 1, nsteps - 1), 0, 0)
    idx = lambda f: pl.BlockSpec((None, 1, tb * TOP_K), f, memory_space=pltpu.SMEM)
    anyspec = pl.BlockSpec(memory_space=pl.ANY)
    tile = (tb, ROW_TILE, LANES)
    return pl.pallas_call(
        functools.partial(_combine_kernel, tb=tb, nsteps=nsteps),
        out_shape=jax.ShapeDtypeStruct((n, D), F32),
        grid=(nsteps,),
        in_specs=[idx(cur), idx(cur), idx(nxt), idx(nxt), idx(cur), pl.BlockSpec(memory_space=pltpu.SMEM),
                  pl.BlockSpec((None, ROW_TILE, LANES), lambda s: ((s * tb) // seq, 0, 0)), anyspec, anyspec],
        out_specs=anyspec,
        scratch_shapes=[pltpu.VMEM((2, TOP_K) + tile, F32), pltpu.VMEM((2,) + tile, F32),
                        pltpu.VMEM((2,) + tile, F32), pltpu.SemaphoreType.DMA((2,)),
                        pltpu.SemaphoreType.DMA((2,)), pltpu.SemaphoreType.DMA((2,))],
        compiler_params=_cparams(("arbitrary",)),
        name="combine",
    )(ti4, rk4, ti4, rk4, tw4, pad_start, gt2t, xn, ys)


def _rope_tables(T):
    rows = T // GRID_W
    row = jnp.repeat(jnp.arange(rows, dtype=F32), GRID_W)
    col = jnp.tile(jnp.arange(GRID_W, dtype=F32), rows)
    inv = ROPE_BASE ** (-jnp.arange(0, AXIS_ROT, 2, dtype=F32) / AXIS_ROT)
    ang_r, ang_c = row[:, None] * inv, col[:, None] * inv
    m = AXIS_ROT // 2
    ang = jnp.concatenate([ang_r, ang_r, ang_c, ang_c], axis=1)
    sign = jnp.tile(jnp.concatenate([-jnp.ones((m,), F32), jnp.ones((m,), F32)]), 2)
    cos = jnp.tile(jnp.cos(ang), (1, LANES // HEAD_DIM))
    sin = jnp.tile(jnp.sin(ang) * sign, (1, LANES // HEAD_DIM))
    return cos, sin


def _head_perm():
    order = []
    for m in range(ATT_GROUP):
        for kv in range(ATT_KV_HEADS):
            h = kv * ATT_GROUP + m
            order.extend(range(h * HEAD_DIM, (h + 1) * HEAD_DIM))
    return np.asarray(order)


def kernel(x, c, ctx, c_ctx, w_mod, b_mod, norm1, norm2, w_in, q_norm, k_norm, attn_sink,
           w_alpha_f, b_alpha_f, w_alpha_b, b_alpha_b, gla_norm, w_branch_attn, w_branch_gla,
           w_out, w_router, b_router, w_exp_in, b_exp_in, w_exp_out, b_exp_out):
    B, T, D = x.shape
    depth = w_mod.shape[0]
    assert depth == 1, "single-layer kernel: the context stream update only feeds later layers"
    l = 0
    perm = _head_perm()

    rows = ((B + 1 + 7) // 8) * 8
    c_all = jnp.zeros((rows, D), F32).at[:B].set(c).at[B].set(c_ctx)
    mod = _modulation(c_all, w_mod[l], b_mod[l])
    sh1, sc1, gt1, sh2, sc2, gt2 = [mod[:B, j * D:(j + 1) * D].reshape(B, 1, D) for j in range(6)]
    csh1, csc1 = [jnp.broadcast_to(mod[B, j * D:(j + 1) * D].reshape(1, 1, D), (B, 1, D)) for j in range(2)]

    offs = np.concatenate([[0], np.cumsum(IN_SPLITS)])
    cols = lambda j: w_in[l][:, offs[j]:offs[j + 1]]
    wal = jnp.zeros((2 * GLA_RANK, 2 * GLA_K_W), F32)
    wal = wal.at[:GLA_RANK, :GLA_K_W].set(w_alpha_f[l]).at[GLA_RANK:, GLA_K_W:].set(w_alpha_b[l])
    pw = {
        "wq": cols(0)[:, perm].astype(BF16), "wk": cols(1).astype(BF16), "wv": cols(2).astype(BF16),
        "wgq": cols(3).astype(BF16), "wgk": cols(4).astype(BF16), "wgv": cols(5).astype(BF16),
        "wgr": cols(6).astype(BF16), "wga": cols(9).astype(BF16), "wgg": cols(10).astype(BF16),
        "wlr": jnp.concatenate([cols(7), cols(8)], axis=1).astype(BF16),
        "qn": jnp.tile(q_norm[l], LANES // HEAD_DIM).reshape(1, LANES),
        "kn": jnp.tile(k_norm[l], LANES // HEAD_DIM).reshape(1, LANES),
        "wal": wal.astype(BF16),
        "bal": jnp.concatenate([b_alpha_f[l], b_alpha_b[l]]).reshape(1, 2 * GLA_K_W),
    }
    cos, sin = _rope_tables(T)
    n1 = norm1[l].reshape(1, D)
    tm = min(512, T)
    aq, ak, av, gq, gk, gv, gr, ga, gg, la = _inproj(
        x, sh1, sc1, n1, {"cos": cos, "sin": sin}, pw, rope=True, full=True, tm=tm)
    cak, cav, cgk, cgv, cla = _inproj(
        ctx, csh1, csc1, n1, None, pw, rope=False, full=False, tm=min(256, ctx.shape[1]))

    attn_o = _attention(attn_sink[l], aq, ak, av, cak, cav)
    gla_o = _gla(gq, gk, gv, la, cgk, cgv, cla)

    wr = jnp.zeros((D, LANES), F32).at[:, :N_EXPERTS].set(w_router[l])
    wrh = wr.astype(BF16)
    mw = {
        "gn": jnp.tile(gla_norm[l], GLA_HEADS).reshape(1, GLA_V_W),
        "wba": w_branch_attn[l][perm, :].astype(BF16), "wbg": w_branch_gla[l].astype(BF16),
        "wo": w_out[l].astype(BF16), "wrh": wrh, "wrl": (wr - wrh.astype(F32)).astype(BF16),
        "br": jnp.full((1, LANES), NEG, F32).at[0, :N_EXPERTS].set(b_router[l]),
    }
    xn, h2, ti, rk, tw, cnt = _merge(x, attn_o, gla_o, gr, ga, gg, gt1, sc2, sh2, norm2[l].reshape(1, D), mw, tm=tm)

    n = B * T
    counts = cnt[0, :N_EXPERTS]
    padded = (counts + MOE_STEP - 1) // MOE_STEP * MOE_STEP
    pad_end = jnp.cumsum(padded)
    pad_start = (pad_end - padded).astype(jnp.int32)
    zstart = jnp.maximum(pad_end - MOE_STEP, 0).astype(jnp.int32)
    cap = (n * TOP_K + N_EXPERTS * (MOE_STEP - 1)) // MOE_STEP * MOE_STEP
    n_blk = cap // MOE_STEP
    row0 = jnp.arange(n_blk, dtype=jnp.int32) * MOE_STEP
    blk_e = jnp.minimum(jnp.sum((pad_end[None, :] <= row0[:, None]).astype(jnp.int32), axis=1), N_EXPERTS - 1)
    onehot = (blk_e[:, None] == jnp.arange(N_EXPERTS, dtype=jnp.int32)[None, :]).astype(jnp.int32)
    valid_end = jnp.sum(onehot * (pad_start + counts)[None, :], axis=1)
    nv = jnp.clip(valid_end - row0, 0, MOE_STEP).astype(jnp.int32)
    n_used = (pad_end[-1] // MOE_STEP).astype(jnp.int32).reshape(1)

    tb = min(256, T)
    ti4 = ti[:, :, :TOP_K].reshape(n // tb, 1, tb * TOP_K)
    rk4 = rk[:, :, :TOP_K].reshape(n // tb, 1, tb * TOP_K)
    tw4 = tw[:, :, :TOP_K].reshape(n // tb, 1, tb * TOP_K)
    xs = _dispatch(ti4, rk4, pad_start, zstart, n_used, h2.reshape(n, D), cap, tb=tb)
    ys = _experts(blk_e, nv, xs, w_exp_in[l], b_exp_in[l], w_exp_out[l], b_exp_out[l])
    out = _combine(ti4, rk4, tw4, pad_start, gt2.reshape(B, ROW_TILE, LANES), xn.reshape(n, D), ys, tb=tb, seq=T)
    return out.reshape(B, T, D)
```

```python
import functools

import numpy as np
import jax
import jax.numpy as jnp
from jax import lax
from jax.experimental import pallas as pl
from jax.experimental.pallas import tpu as pltpu

F32 = jnp.float32
BF16 = jnp.bfloat16

D_MODEL = 1024
GRID_W = 64
EPS = 1e-6
ATT_HEADS = 8
ATT_KV_HEADS = 2
ATT_GROUP = ATT_HEADS // ATT_KV_HEADS
HEAD_DIM = 64
WINDOW = 128
ATT_BLOCK = 128
ROPE_BASE = 10000.0
AXIS_ROT = HEAD_DIM // 2
GLA_HEADS = 4
GLA_DK = 64
GLA_DV = 128
GLA_RANK = 16
GLA_TAU = 16.0
N_EXPERTS = 32
TOP_K = 4
D_FF = D_MODEL
SWIGLU_ALPHA = 1.702
SWIGLU_LIMIT = 7.0
MOE_STEP = 512

ATT_W = ATT_HEADS * HEAD_DIM
ATT_KV_W = ATT_KV_HEADS * HEAD_DIM
GLA_K_W = GLA_HEADS * GLA_DK
GLA_V_W = GLA_HEADS * GLA_DV
IN_SPLITS = (ATT_W, ATT_KV_W, ATT_KV_W, GLA_K_W, GLA_K_W, GLA_V_W, GLA_V_W, GLA_RANK, GLA_RANK, D_MODEL, D_MODEL)

LANES = 128
ROW_TILE = D_MODEL // LANES
VMEM_LIMIT = 56 * 1024 * 1024
NEG = -1e30

GLA_C = 64
GLA_SUB = 4
GLA_LEVELS = 4


def _cparams(sem):
    return pltpu.CompilerParams(dimension_semantics=sem, vmem_limit_bytes=VMEM_LIMIT)


def _full(shape):
    n = len(shape)
    return pl.BlockSpec(shape, lambda *_: (0,) * n)


def _mod_kernel(c_ref, w_ref, b_ref, o_ref):
    c = c_ref[...]
    s = c * (1.0 / (1.0 + jnp.exp(-c)))
    o_ref[...] = jnp.dot(s, w_ref[...], preferred_element_type=F32,
                         precision=lax.Precision.HIGHEST) + b_ref[...]


def _modulation(c_all, w_mod, b_mod):
    rows = c_all.shape[0]
    n = w_mod.shape[1]
    tn = 1536
    return pl.pallas_call(
        _mod_kernel,
        out_shape=jax.ShapeDtypeStruct((rows, n), F32),
        grid=(n // tn,),
        in_specs=[pl.BlockSpec((rows, D_MODEL), lambda j: (0, 0)),
                  pl.BlockSpec((D_MODEL, tn), lambda j: (0, j)),
                  pl.BlockSpec((1, tn), lambda j: (0, j))],
        out_specs=pl.BlockSpec((rows, tn), lambda j: (0, j)),
        compiler_params=_cparams(("arbitrary",)),
        name="mod",
    )(c_all, w_mod, b_mod.reshape(1, n))


def _pair_norm(a, g, lo):
    s = a * a
    tot = jnp.sum(s, axis=-1, keepdims=True)
    slo = jnp.sum(jnp.where(lo, s, 0.0), axis=-1, keepdims=True)
    ms = jnp.where(lo, slo, tot - slo) * (1.0 / HEAD_DIM)
    return a * lax.rsqrt(ms + EPS) * g


def _rope(y, cos, sin, first):
    up = pltpu.roll(y, LANES - AXIS_ROT // 2, 1)
    dn = pltpu.roll(y, AXIS_ROT // 2, 1)
    return y * cos + jnp.where(first, up, dn) * sin


def _inproj_kernel(*refs, rope, full):
    if full:
        (x_ref, sh_ref, sc_ref, n1_ref, cos_ref, sin_ref, qn_ref, kn_ref, wal_ref, bal_ref,
         wq, wk, wv, wgq, wgk, wgv, wgr, wga, wgg, wlr,
         oq, ok, ov, ogq, ogk, ogv, ogr, oga, ogg, ola) = refs
    else:
        (x_ref, sh_ref, sc_ref, n1_ref, kn_ref, wal_ref, bal_ref,
         wk, wv, wgk, wgv, wlr,
         ok, ov, ogk, ogv, ola) = refs
    x = x_ref[...]
    tm = x.shape[0]
    ms = jnp.mean(x * x, axis=-1, keepdims=True)
    h = (x * lax.rsqrt(ms + EPS) * n1_ref[...]) * (1.0 + sc_ref[...]) + sh_ref[...]
    hb = h.astype(BF16)

    def proj(w_ref):
        return jnp.dot(hb, w_ref[...], preferred_element_type=F32)

    lane = lax.broadcasted_iota(jnp.int32, (tm, LANES), 1)
    lo = lane < HEAD_DIM
    first = (lane % AXIS_ROT) < (AXIS_ROT // 2)
    if rope:
        cos = cos_ref[...]
        sin = sin_ref[...]

    k = _pair_norm(proj(wk), kn_ref[...], lo)
    if rope:
        k = _rope(k, cos, sin, first)
    ok[...] = k.astype(BF16)
    ov[...] = proj(wv).astype(BF16)
    ogk[...] = proj(wgk).astype(BF16)
    ogv[...] = proj(wgv).astype(BF16)
    lr = proj(wlr).astype(BF16)
    z = jnp.dot(lr, wal_ref[...], preferred_element_type=F32) + bal_ref[...]
    ola[...] = (jnp.minimum(z, 0.0) - jnp.log(1.0 + jnp.exp(-jnp.abs(z)))) * (1.0 / GLA_TAU)
    if full:
        q = proj(wq)
        for p in range(ATT_W // LANES):
            y = _pair_norm(q[:, p * LANES:(p + 1) * LANES], qn_ref[...], lo)
            if rope:
                y = _rope(y, cos, sin, first)
            oq[:, p * LANES:(p + 1) * LANES] = (y * HEAD_DIM ** -0.5).astype(BF16)
        ogq[...] = (proj(wgq) * GLA_DK ** -0.5).astype(BF16)
        g = proj(wgr)
        ogr[...] = (g * (1.0 / (1.0 + jnp.exp(-g)))).astype(BF16)
        oga[...] = (1.0 / (1.0 + jnp.exp(-proj(wga)))).astype(BF16)
        ogg[...] = (1.0 / (1.0 + jnp.exp(-proj(wgg)))).astype(BF16)


def _inproj(x, sh, sc, norm1, tabs, wts, *, rope, full, tm):
    B, T, D = x.shape
    grid = (B, T // tm)
    row = lambda w: pl.BlockSpec((None, tm, w), lambda b, t: (b, t, 0))
    vec = pl.BlockSpec((None, 1, D), lambda b, t: (b, 0, 0))
    tab = pl.BlockSpec((tm, LANES), lambda b, t: (t, 0))
    if full:
        names = ("wq", "wk", "wv", "wgq", "wgk", "wgv", "wgr", "wga", "wgg", "wlr")
        ins = [x, sh, sc, norm1, tabs["cos"], tabs["sin"], wts["qn"], wts["kn"], wts["wal"], wts["bal"]]
        specs = [row(D), vec, vec, _full((1, D)), tab, tab, _full((1, LANES)), _full((1, LANES)),
                 _full(wts["wal"].shape), _full(wts["bal"].shape)]
        out_w = (ATT_W, ATT_KV_W, ATT_KV_W, GLA_K_W, GLA_K_W, GLA_V_W, GLA_V_W, D, D)
    else:
        names = ("wk", "wv", "wgk", "wgv", "wlr")
        ins = [x, sh, sc, norm1, wts["kn"], wts["wal"], wts["bal"]]
        specs = [row(D), vec, vec, _full((1, D)), _full((1, LANES)),
                 _full(wts["wal"].shape), _full(wts["bal"].shape)]
        out_w = (ATT_KV_W, ATT_KV_W, GLA_K_W, GLA_V_W)
    ins += [wts[n] for n in names]
    specs += [_full(wts[n].shape) for n in names]
    out_shape = [jax.ShapeDtypeStruct((B, T, w), BF16) for w in out_w]
    out_shape.append(jax.ShapeDtypeStruct((B, T, 2 * GLA_K_W), F32))
    out_specs = [row(w) for w in out_w] + [row(2 * GLA_K_W)]
    return pl.pallas_call(
        functools.partial(_inproj_kernel, rope=rope, full=full),
        out_shape=out_shape, grid=grid, in_specs=specs, out_specs=out_specs,
        compiler_params=_cparams(("parallel", "arbitrary")),
        name="inproj_full" if full else "inproj_ctx",
    )(*ins)


def _attn_kernel(sink_ref, q_ref, k0_ref, k1_ref, k2_ref, k3_ref, kx_ref, v0_ref, v1_ref, v2_ref, v3_ref,
                 vx_ref, o_ref, *, seq):
    n = pl.program_id(1)
    blk = ATT_BLOCK
    lc = kx_ref.shape[0]
    nk = 3 * blk + lc
    nslab = ATT_W // LANES
    rows = nslab * blk
    kblocks = (k0_ref, k1_ref, k2_ref, k3_ref)
    vblocks = (v0_ref, v1_ref, v2_ref, v3_ref)
    ri = lax.broadcasted_iota(jnp.int32, (rows, nk), 0)
    kj = lax.broadcasted_iota(jnp.int32, (rows, nk), 1)
    qi = ri % blk
    lane = lax.broadcasted_iota(jnp.int32, (blk, LANES), 1)
    lo = lane < HEAD_DIM
    hrow = lax.broadcasted_iota(jnp.int32, (rows, 1), 0) // blk
    for sb in range(2):
        kcat = jnp.concatenate([r[...] for r in kblocks[sb:sb + 3]] + [kx_ref[...]], axis=0)
        vcat = jnp.concatenate([r[...] for r in vblocks[sb:sb + 3]] + [vx_ref[...]], axis=0)
        kpos = (2 * n + sb - 1) * blk + kj
        valid = ((jnp.abs(kj - blk - qi) <= WINDOW) & (kpos >= 0) & (kpos < seq)) | (kj >= 3 * blk)
        q = q_ref[sb * blk:(sb + 1) * blk, :]
        outs = []
        for kv in range(ATT_KV_HEADS):
            keep = lo if kv == 0 else jnp.logical_not(lo)
            qs = jnp.concatenate([jnp.where(keep, q[:, m * LANES:(m + 1) * LANES], jnp.zeros((blk, LANES), BF16))
                                  for m in range(nslab)], axis=0)
            s = lax.dot_general(qs, kcat, (((1,), (1,)), ((), ())), preferred_element_type=F32)
            s = jnp.where(valid, s, NEG)
            snk = jnp.zeros((rows, 1), F32)
            for m in range(nslab):
                snk = jnp.where(hrow == m, sink_ref[kv * ATT_GROUP + m], snk)
            mx = jnp.maximum(jnp.max(s, axis=-1, keepdims=True), snk)
            p = jnp.exp(s - mx)
            den = jnp.sum(p, axis=-1, keepdims=True) + jnp.exp(snk - mx)
            outs.append(jnp.dot(p.astype(BF16), vcat, preferred_element_type=F32) / den)
        for m in range(nslab):
            o_ref[sb * blk:(sb + 1) * blk, m * LANES:(m + 1) * LANES] = jnp.where(
                lo, outs[0][m * blk:(m + 1) * blk], outs[1][m * blk:(m + 1) * blk]).astype(BF16)


def _attention(sink, aq, ak, av, cak, cav):
    B, T, _ = aq.shape
    lc = cak.shape[1]
    blk = ATT_BLOCK
    nb = T // blk
    assert nb % 2 == 0
    kvspec = lambda off: pl.BlockSpec((None, blk, ATT_KV_W),
                                      lambda b, n: (b, jnp.clip(2 * n + off, 0, nb - 1), 0))
    cspec = pl.BlockSpec((None, lc, ATT_KV_W), lambda b, n: (b, 0, 0))
    qspec = pl.BlockSpec((None, 2 * blk, ATT_W), lambda b, n: (b, n, 0))
    kvs = [kvspec(off) for off in (-1, 0, 1, 2)]
    return pl.pallas_call(
        functools.partial(_attn_kernel, seq=T),
        out_shape=jax.ShapeDtypeStruct((B, T, ATT_W), BF16),
        grid=(B, nb // 2),
        in_specs=[pl.BlockSpec(memory_space=pltpu.SMEM), qspec] + kvs + [cspec] + kvs + [cspec],
        out_specs=qspec,
        compiler_params=_cparams(("parallel", "arbitrary")),
        name="attn",
    )(sink, aq, ak, ak, ak, ak, cak, av, av, av, av, cav)


def _gla_constants():
    C, sub, L = GLA_C, GLA_SUB, GLA_LEVELS
    i = np.arange(C)[:, None]
    t = np.arange(C)[None, :]
    tabs = [t <= i]
    rowq, same = [], []
    for l in range(L):
        s = C >> l
        mid = (i // s) * s + s // 2
        rowq.append(np.broadcast_to(i >= mid, (C, C)))
        same.append((i // s) == (t // s))
    shifts, dmask, dvalid = [], [t == i], []
    for d in range(1, sub):
        ok = (i % sub) >= d
        shifts.append(ok & (t == i - d))
        dmask.append(ok & (t == i - d))
        dvalid.append(np.broadcast_to(ok, (C, C)))
    flip = lambda a: a[::-1, ::-1]
    tile = lambda a: np.tile(a, (1, GLA_HEADS))

    def both(xs, lanes):
        f = (lambda a: tile(a)) if lanes else (lambda a: a)
        return np.stack([np.concatenate([f(a) for a in xs], 0),
                         np.concatenate([f(flip(a)) for a in xs], 0)]).astype(np.float32)

    hk = np.arange(GLA_K_W) // GLA_DK
    hv = np.arange(GLA_V_W) // GLA_DV
    ind = (hk[:, None] == hk[None, :]).astype(np.float32)
    bdv = (hk[:, None] == hv[None, :]).astype(np.float32)
    return (both(tabs, False), both(shifts, False), both(rowq, True), both(same, True),
            both(dmask, True), ind, bdv, np.ascontiguousarray(bdv.T), both(dvalid, True))


def _gla_chunk(q_b, k_b, v_b, la, cst, d):
    tri_ref, shm_ref, rq_ref, sm_ref, dm_ref, ind_ref, bdv_ref, bds_ref, dv_ref = cst
    C = GLA_C
    kw = GLA_K_W
    q = q_b.astype(F32)
    k = k_b.astype(F32)
    hi = la.astype(BF16)
    r1 = la - hi.astype(F32)
    mid = r1.astype(BF16)
    lo = (r1 - mid.astype(F32)).astype(BF16)
    b3 = jnp.dot(tri_ref[d], jnp.concatenate([hi, mid, lo], axis=1), preferred_element_type=F32)
    b = b3[:, :kw] + b3[:, kw:2 * kw] + b3[:, 2 * kw:]
    last = b[C - 1:C] if d == 0 else b[0:1]

    qt = (q * jnp.exp(b)).astype(BF16)
    kt = (k * jnp.exp(last - b)).astype(BF16)
    gamma = jnp.exp(last)

    ind = ind_ref[...]
    a = jnp.zeros((C, kw), F32)
    for l in range(GLA_LEVELS):
        s = C >> l
        off = s // 2 - 1 if d == 0 else s // 2
        bref = jnp.concatenate([jnp.broadcast_to(b[st + off:st + off + 1], (s, kw)) for st in range(0, C, s)], axis=0)
        rq = rq_ref[d, l * C:(l + 1) * C, :]
        el = jnp.exp((b - bref) * (2.0 * rq - 1.0))
        qh = (q * (el * rq)).astype(BF16)
        kh = (k * (el * (1.0 - rq))).astype(BF16)
        bdk = jnp.concatenate([kh] * GLA_HEADS, axis=0) * ind
        al = lax.dot_general(qh, bdk, (((1,), (1,)), ((), ())), preferred_element_type=F32)
        a = a + al * sm_ref[d, l * C:(l + 1) * C, :]
    ksh = jnp.dot(shm_ref[d], k_b, preferred_element_type=F32)
    ps = [q * k]
    for j in range(1, GLA_SUB):
        bsh = pltpu.roll(b, j if d == 0 else C - j, 0)
        ej = jnp.exp((b - bsh) * dv_ref[d, (j - 1) * C:j * C, :])
        ps.append(q * ksh[(j - 1) * C:j * C] * ej)
    w = jnp.dot(jnp.concatenate(ps, axis=0).astype(BF16), ind, preferred_element_type=F32)
    for j in range(GLA_SUB):
        a = a + w[j * C:(j + 1) * C] * dm_ref[d, j * C:(j + 1) * C, :]

    bdv = jnp.concatenate([v_b] * GLA_HEADS, axis=0) * bdv_ref[...]
    o_intra = jnp.dot(a.astype(BF16), bdv, preferred_element_type=F32)
    upd = lax.dot_general(v_b, kt, (((0,), (0,)), ((), ())), preferred_element_type=F32) * bds_ref[...]
    return o_intra, qt, upd, gamma


def _gla_kernel(qf_ref, kf_ref, vf_ref, laf_ref, qb_ref, kb_ref, vb_ref, lab_ref,
                ckf_ref, cvf_ref, claf_ref, ckb_ref, cvb_ref, clab_ref,
                tri_ref, shm_ref, rq_ref, sm_ref, dm_ref, ind_ref, bdv_ref, bds_ref, dv_ref,
                of_ref, ob_ref, st_ref, *, n_ctx_steps):
    s = pl.program_id(1)
    C = GLA_C
    cst = (tri_ref, shm_ref, rq_ref, sm_ref, dm_ref, ind_ref, bdv_ref, bds_ref, dv_ref)

    @pl.when(s == 0)
    def _():
        st_ref[...] = jnp.zeros_like(st_ref)

    is_ctx = s < n_ctx_steps
    dirs = ((0, qf_ref, kf_ref, vf_ref, laf_ref, ckf_ref, cvf_ref, claf_ref, of_ref),
            (1, qb_ref, kb_ref, vb_ref, lab_ref, ckb_ref, cvb_ref, clab_ref, ob_ref))
    work = []
    for d, q_ref, k_ref, v_ref, la_ref, ck_ref, cv_ref, cla_ref, o_ref in dirs:
        order = (0, 1) if d == 0 else (1, 0)
        for c in order:
            rows = slice(c * C, (c + 1) * C)
            k_b = jnp.where(is_ctx, ck_ref[rows, :], k_ref[rows, :])
            v_b = jnp.where(is_ctx, cv_ref[rows, :], v_ref[rows, :])
            la = jnp.where(is_ctx, cla_ref[rows, :], la_ref[rows, :])
            work.append((d, rows, o_ref, _gla_chunk(q_ref[rows, :], k_b, v_b, la, cst, d)))
    states = [st_ref[0], st_ref[1]]
    for d, rows, o_ref, (o_intra, qt, upd, gamma) in work:
        st = states[d]
        o_ref[rows, :] = o_intra + lax.dot_general(qt, st.astype(BF16), (((1,), (1,)), ((), ())),
                                                   preferred_element_type=F32)
        states[d] = st * gamma + upd
    st_ref[0] = states[0]
    st_ref[1] = states[1]


def _gla(gq, gk, gv, la, cgk, cgv, cla):
    B, T, _ = gq.shape
    lc = cgk.shape[1]
    R = 2 * GLA_C
    assert lc % R == 0 and T % R == 0
    n_ctx, n_lat = lc // R, T // R
    consts = _gla_constants()
    tri, shm = jnp.asarray(consts[0], BF16), jnp.asarray(consts[1], BF16)
    rq, sm, dm = [jnp.asarray(c) for c in consts[2:5]]
    ind, bdv = jnp.asarray(consts[5], BF16), jnp.asarray(consts[6], BF16)
    bds, dv = jnp.asarray(consts[7]), jnp.asarray(consts[8])

    def lat(s, d):
        j = jnp.maximum(s - n_ctx, 0)
        return j if d == 0 else n_lat - 1 - j

    def ctx(s, d):
        j = jnp.minimum(s, n_ctx - 1)
        return j if d == 0 else n_ctx - 1 - j

    lspec = lambda w, d, c=0: pl.BlockSpec((None, R, w), lambda b, s: (b, lat(s, d), c))
    cspec = lambda w, d, c=0: pl.BlockSpec((None, R, w), lambda b, s: (b, ctx(s, d), c))
    lat_specs = lambda d: [lspec(GLA_K_W, d), lspec(GLA_K_W, d), lspec(GLA_V_W, d), lspec(GLA_K_W, d, d)]
    ctx_specs = lambda d: [cspec(GLA_K_W, d), cspec(GLA_V_W, d), cspec(GLA_K_W, d, d)]
    cs = [tri, shm, rq, sm, dm, ind, bdv, bds, dv]
    return pl.pallas_call(
        functools.partial(_gla_kernel, n_ctx_steps=n_ctx),
        out_shape=[jax.ShapeDtypeStruct((B, T, GLA_V_W), F32)] * 2,
        grid=(B, n_ctx + n_lat),
        in_specs=lat_specs(0) + lat_specs(1) + ctx_specs(0) + ctx_specs(1) + [_full(c.shape) for c in cs],
        out_specs=[lspec(GLA_V_W, 0), lspec(GLA_V_W, 1)],
        scratch_shapes=[pltpu.VMEM((2, GLA_V_W, GLA_K_W), F32)],
        compiler_params=_cparams(("parallel", "arbitrary")),
        name="gla",
    )(gq, gk, gv, la, gq, gk, gv, la, cgk, cgv, cla, cgk, cgv, cla, *cs)


def _merge_kernel(x_ref, at_ref, of_ref, ob_ref, gr_ref, ga_ref, gg_ref, gt1_ref, sc2_ref, sh2_ref,
                  n2_ref, gn_ref, ltri_ref, wba_ref, wbg_ref, wo_ref, wrh_ref, wrl_ref, br_ref,
                  xn_ref, h2_ref, ti_ref, rk_ref, tw_ref, cnt_ref, carry_ref):
    tm = x_ref.shape[0]

    @pl.when((pl.program_id(0) == 0) & (pl.program_id(1) == 0))
    def _():
        carry_ref[...] = jnp.zeros_like(carry_ref)

    go = of_ref[...] + ob_ref[...]
    parts = []
    for h in range(GLA_HEADS):
        gh = go[:, h * GLA_DV:(h + 1) * GLA_DV]
        ms = jnp.mean(gh * gh, axis=-1, keepdims=True)
        parts.append(gh * lax.rsqrt(ms + EPS))
    o = jnp.concatenate(parts, axis=1) * gn_ref[...] * gr_ref[...].astype(F32)
    ya = jnp.dot(at_ref[...], wba_ref[...], preferred_element_type=F32)
    yg = jnp.dot(o.astype(BF16), wbg_ref[...], preferred_element_type=F32)
    y = ga_ref[...].astype(F32) * ya + gg_ref[...].astype(F32) * yg
    z = jnp.dot(y.astype(BF16), wo_ref[...], preferred_element_type=F32)
    xn = x_ref[...] + gt1_ref[...] * z
    xn_ref[...] = xn
    ms = jnp.mean(xn * xn, axis=-1, keepdims=True)
    h2 = (xn * lax.rsqrt(ms + EPS) * n2_ref[...]) * (1.0 + sc2_ref[...]) + sh2_ref[...]
    hh = h2.astype(BF16)
    hl = (h2 - hh.astype(F32)).astype(BF16)
    h2_ref[...] = h2
    logits = (jnp.dot(hh, wrh_ref[...], preferred_element_type=F32)
              + jnp.dot(hl, wrh_ref[...], preferred_element_type=F32)
              + jnp.dot(hh, wrl_ref[...], preferred_element_type=F32)) + br_ref[...]
    lane = lax.broadcasted_iota(jnp.int32, (tm, LANES), 1).astype(F32)
    vals, idxs = [], []
    l = logits
    for _ in range(TOP_K):
        m = jnp.max(l, axis=-1, keepdims=True)
        ix = jnp.min(jnp.where(l == m, lane, float(LANES)), axis=-1, keepdims=True)
        vals.append(m)
        idxs.append(ix)
        l = jnp.where(lane == ix, -3.0e38, l)
    ex = [jnp.exp(v - vals[0]) for v in vals]
    den = ex[0] + ex[1] + ex[2] + ex[3]
    mh = jnp.zeros((tm, LANES), F32)
    for j in range(TOP_K):
        mh = mh + jnp.where(lane == idxs[j], 1.0, 0.0)
    pc = jnp.dot(ltri_ref[...], mh.astype(BF16), preferred_element_type=F32) + carry_ref[...]
    ti = jnp.zeros((tm, LANES), F32)
    rk = jnp.zeros((tm, LANES), F32)
    tw = jnp.zeros((tm, LANES), F32)
    for j in range(TOP_K):
        rj = jnp.sum(jnp.where(lane == idxs[j], pc, 0.0), axis=-1, keepdims=True)
        ti = jnp.where(lane == float(j), idxs[j], ti)
        rk = jnp.where(lane == float(j), rj, rk)
        tw = jnp.where(lane == float(j), ex[j] / den, tw)
    ti_ref[...] = ti.astype(jnp.int32)
    rk_ref[...] = rk.astype(jnp.int32)
    tw_ref[...] = tw
    total = carry_ref[...] + jnp.sum(mh, axis=0, keepdims=True)
    carry_ref[...] = total
    cnt_ref[...] = total.astype(jnp.int32)


def _merge(x, attn_o, gla_o, gr, ga, gg, gt1, sc2, sh2, norm2, wts, *, tm):
    B, T, D = x.shape
    row = lambda w: pl.BlockSpec((None, tm, w), lambda b, t: (b, t, 0))
    vec = pl.BlockSpec((None, 1, D), lambda b, t: (b, 0, 0))
    names = ("wba", "wbg", "wo", "wrh", "wrl", "br")
    ltri = jnp.asarray(np.tril(np.ones((tm, tm), np.float32), -1), BF16)
    return pl.pallas_call(
        _merge_kernel,
        out_shape=[jax.ShapeDtypeStruct((B, T, D), F32), jax.ShapeDtypeStruct((B, T, D), F32),
                   jax.ShapeDtypeStruct((B, T, LANES), jnp.int32), jax.ShapeDtypeStruct((B, T, LANES), jnp.int32),
                   jax.ShapeDtypeStruct((B, T, LANES), F32), jax.ShapeDtypeStruct((1, LANES), jnp.int32)],
        grid=(B, T // tm),
        in_specs=[row(D), row(ATT_W), row(GLA_V_W), row(GLA_V_W),
                  row(GLA_V_W), row(D), row(D), vec, vec, vec,
                  _full((1, D)), _full((1, GLA_V_W)), _full((tm, tm))] + [_full(wts[n].shape) for n in names],
        out_specs=[row(D), row(D), row(LANES), row(LANES), row(LANES), _full((1, LANES))],
        scratch_shapes=[pltpu.VMEM((1, LANES), F32)],
        compiler_params=_cparams(("arbitrary", "arbitrary")),
        name="merge",
    )(x, attn_o, gla_o[0], gla_o[1], gr, ga, gg, gt1, sc2, sh2, norm2, wts["gn"], ltri, *[wts[n] for n in names])


def _dest_kernel(ti_ref, rk_ref, ps_ref, o_ref):
    tm = ti_ref.shape[0]
    lane = lax.broadcasted_iota(jnp.int32, (tm, LANES), 1)
    ti = ti_ref[...]
    ps = ps_ref[...].astype(F32)
    out = jnp.zeros((tm, LANES), F32)
    for k in range(TOP_K):
        start = jnp.sum(jnp.where(lane == ti[:, k:k + 1], ps, 0.0), axis=-1, keepdims=True)
        out = jnp.where(lane == k, start, out)
    o_ref[...] = out.astype(jnp.int32) + jnp.where(lane < TOP_K, rk_ref[...], 0)


def _dest(ti, rk, pad_start, *, tm):
    n = ti.shape[0]
    ps = jnp.zeros((1, LANES), jnp.int32).at[0, :N_EXPERTS].set(pad_start)
    row = pl.BlockSpec((tm, LANES), lambda i: (i, 0))
    return pl.pallas_call(
        _dest_kernel, out_shape=jax.ShapeDtypeStruct((n, LANES), jnp.int32), grid=(n // tm,),
        in_specs=[row, row, _full((1, LANES))], out_specs=row,
        compiler_params=_cparams(("arbitrary",)), name="dest",
    )(ti, rk, ps)


def _dispatch_kernel(d_ref, zs_ref, nu_ref, h_ref, xs_ref, buf, zbuf, isem, sem, zsem,
                     *, tb, nsteps):
    s = pl.program_id(0)
    nblk = xs_ref.shape[0] // MOE_STEP
    slot = s % 3
    nxt = (s + 1) % 3

    def loads(step, sl):
        r0 = pl.multiple_of(step * tb, tb)
        return [pltpu.make_async_copy(h_ref.at[pl.ds(r0, tb), pl.ds(j * LANES, LANES)], buf.at[sl, :, j, :],
                                      isem.at[sl]) for j in range(ROW_TILE)]

    def wait_rows(sl):
        for _ in range(TOP_K):
            pltpu.make_async_copy(buf.at[sl], xs_ref.at[pl.ds(0, tb)], sem.at[sl]).wait()

    @pl.when(s == 0)
    def _():
        zbuf[...] = jnp.zeros_like(zbuf)

        def zstart(e, c):
            z0 = pl.multiple_of(zs_ref[e], MOE_STEP)
            pltpu.make_async_copy(zbuf, xs_ref.at[pl.ds(z0, MOE_STEP)], zsem).start()
            return c

        def zwait(e, c):
            pltpu.make_async_copy(zbuf, xs_ref.at[pl.ds(0, MOE_STEP)], zsem).wait()
            return c

        lax.fori_loop(0, N_EXPERTS, zstart, 0)
        lax.fori_loop(0, N_EXPERTS, zwait, 0)

        def tstart(j, c):
            pltpu.make_async_copy(zbuf, xs_ref.at[pl.ds(pl.multiple_of(j * MOE_STEP, MOE_STEP), MOE_STEP)],
                                  zsem).start()
            return c

        lax.fori_loop(nu_ref[0], nblk, tstart, 0)
        lax.fori_loop(nu_ref[0], nblk, zwait, 0)
        for c in loads(0, 0):
            c.start()

    for c in loads(s, slot):
        c.wait()

    @pl.when(s + 1 < nsteps)
    def _():
        @pl.when(s >= 2)
        def _():
            wait_rows(nxt)
        for c in loads(s + 1, nxt):
            c.start()

    def issue(r, c):
        for k in range(TOP_K):
            d = d_ref[0, r * TOP_K + k]
            pltpu.make_async_copy(buf.at[slot, r], xs_ref.at[d], sem.at[slot]).start(priority=k % 2)
        return c

    lax.fori_loop(0, tb, issue, 0, unroll=8)

    @pl.when(s == nsteps - 1)
    def _():
        wait_rows(slot)
        if nsteps >= 2:
            wait_rows((s + 2) % 3)
        if nsteps >= 3:
            wait_rows(nxt)


def _dispatch(dest4, zstart, n_used, h2, cap, *, tb):
    n, D = h2.shape
    nsteps = n // tb
    idx = pl.BlockSpec((None, 1, tb * TOP_K), lambda s: (s, 0, 0), memory_space=pltpu.SMEM)
    smem = pl.BlockSpec(memory_space=pltpu.SMEM)
    anyspec = pl.BlockSpec(memory_space=pl.ANY)
    return pl.pallas_call(
        functools.partial(_dispatch_kernel, tb=tb, nsteps=nsteps),
        out_shape=jax.ShapeDtypeStruct((cap, ROW_TILE, LANES), F32),
        grid=(nsteps,),
        in_specs=[idx, smem, smem, anyspec],
        out_specs=anyspec,
        scratch_shapes=[pltpu.VMEM((3, tb, ROW_TILE, LANES), F32), pltpu.VMEM((MOE_STEP, ROW_TILE, LANES), F32),
                        pltpu.SemaphoreType.DMA((3,)), pltpu.SemaphoreType.DMA((3,)), pltpu.SemaphoreType.DMA(())],
        compiler_params=_cparams(("arbitrary",)),
        name="dispatch",
    )(dest4, zstart, n_used, h2)


def _expert_kernel(be_ref, nv_ref, fs_ref, nx_ref, pr_ref, xs_ref, w1_ref, b1_ref, w2_ref, b2_ref, ys_ref,
                   w1b, w2b, w1f, w2f, xin, yout, isem, osem, wsem, *, nsteps):
    i = pl.program_id(0)
    slot = i % 2

    def wloads(ex, sl):
        return [pltpu.make_async_copy(w1_ref.at[ex], w1f.at[sl], wsem.at[sl]),
                pltpu.make_async_copy(w2_ref.at[ex], w2f.at[sl], wsem.at[sl])]

    def loads(step, sl):
        r0 = pl.multiple_of(step * MOE_STEP, MOE_STEP)
        return [pltpu.make_async_copy(xs_ref.at[pl.ds(r0, MOE_STEP), j, :],
                                      xin.at[sl, :, pl.ds(j * LANES, LANES)], isem.at[sl]) for j in range(ROW_TILE)]

    def stores(step, sl):
        r0 = pl.multiple_of(step * MOE_STEP, MOE_STEP)
        return [pltpu.make_async_copy(yout.at[sl, :, pl.ds(j * LANES, LANES)],
                                      ys_ref.at[pl.ds(r0, MOE_STEP), j, :], osem.at[sl]) for j in range(ROW_TILE)]

    @pl.when(i == 0)
    def _():
        for c in loads(0, 0):
            c.start()
        for c in wloads(be_ref[0], 0):
            c.start()

    @pl.when(fs_ref[i] == 1)
    def _():
        par = pr_ref[i]
        for c in wloads(be_ref[i], par):
            c.wait()
        w1b[...] = w1f[par].astype(BF16)
        w2b[...] = w2f[par].astype(BF16)

        @pl.when(nx_ref[i] >= 0)
        def _():
            for c in wloads(nx_ref[i], 1 - par):
                c.start()

    for c in loads(i, slot):
        c.wait()

    @pl.when(i + 1 < nsteps)
    def _():
        for c in loads(i + 1, 1 - slot):
            c.start()

    @pl.when(i >= 2)
    def _():
        for c in stores(i - 2, slot):
            c.wait()

    def mlp(rows):
        xb = xin[slot, 0:rows, :].astype(BF16)
        y = jnp.zeros((rows, D_MODEL), F32)
        fh = D_FF // 2
        for h in range(2):
            g = jnp.dot(xb, w1b[:, h * fh:(h + 1) * fh], preferred_element_type=F32) + b1_ref[:, h * fh:(h + 1) * fh]
            u = (jnp.dot(xb, w1b[:, D_FF + h * fh:D_FF + (h + 1) * fh], preferred_element_type=F32)
                 + b1_ref[:, D_FF + h * fh:D_FF + (h + 1) * fh])
            gate = jnp.minimum(g, SWIGLU_LIMIT)
            up = jnp.clip(u, -SWIGLU_LIMIT, SWIGLU_LIMIT)
            act = gate * (1.0 / (1.0 + jnp.exp(-SWIGLU_ALPHA * gate))) * (up + 1.0)
            y = y + jnp.dot(act.astype(BF16), w2b[h * fh:(h + 1) * fh, :], preferred_element_type=F32)
        yout[slot, 0:rows, :] = y + b2_ref[...]

    nv = nv_ref[i]
    half = MOE_STEP // 2

    @pl.when(nv > half)
    def _():
        mlp(MOE_STEP)

    @pl.when((nv > 0) & (nv <= half))
    def _():
        mlp(half)
        yout[slot, half:, :] = jnp.zeros((MOE_STEP - half, D_MODEL), F32)

    @pl.when(nv == 0)
    def _():
        yout[slot] = jnp.zeros((MOE_STEP, D_MODEL), F32)

    for c in stores(i, slot):
        c.start()

    @pl.when(i == nsteps - 1)
    def _():
        for c in stores(i, slot):
            c.wait()
        if nsteps >= 2:
            for c in stores(i - 1, 1 - slot):
                c.wait()


def _experts(blk_e, nv, first, nxt_e, parity, xs, w1, b1, w2, b2):
    cap = xs.shape[0]
    n_blk = cap // MOE_STEP
    ne = w1.shape[0]
    anyspec = pl.BlockSpec(memory_space=pl.ANY)
    bias = lambda w: pl.BlockSpec((None, 1, w), lambda i, be, *_: (be[i], 0, 0))
    gs = pltpu.PrefetchScalarGridSpec(
        num_scalar_prefetch=5, grid=(n_blk,),
        in_specs=[anyspec, anyspec, bias(2 * D_FF), anyspec, bias(D_MODEL)],
        out_specs=anyspec,
        scratch_shapes=[pltpu.VMEM((D_MODEL, 2 * D_FF), BF16), pltpu.VMEM((D_FF, D_MODEL), BF16),
                        pltpu.VMEM((2, D_MODEL, 2 * D_FF), F32), pltpu.VMEM((2, D_FF, D_MODEL), F32),
                        pltpu.VMEM((2, MOE_STEP, D_MODEL), F32), pltpu.VMEM((2, MOE_STEP, D_MODEL), F32),
                        pltpu.SemaphoreType.DMA((2,)), pltpu.SemaphoreType.DMA((2,)),
                        pltpu.SemaphoreType.DMA((2,))])
    return pl.pallas_call(
        functools.partial(_expert_kernel, nsteps=n_blk), grid_spec=gs,
        out_shape=jax.ShapeDtypeStruct((cap, ROW_TILE, LANES), F32),
        compiler_params=_cparams(("arbitrary",)),
        name="experts",
    )(blk_e, nv, first, nxt_e, parity, xs, w1, b1.reshape(ne, 1, 2 * D_FF), w2, b2.reshape(ne, 1, D_MODEL))


def _combine_kernel(dc_ref, dn_ref, tw_ref, gt2_ref, xn_ref, ys_ref, o_ref,
                    gbuf, xt, ot, gsem, xsem, osem, *, tb, nsteps):
    s = pl.program_id(0)
    slot = s % 2
    other = 1 - slot

    def xloads(step, sl):
        r0 = pl.multiple_of(step * tb, tb)
        return [pltpu.make_async_copy(xn_ref.at[pl.ds(r0, tb), pl.ds(j * LANES, LANES)], xt.at[sl, :, j, :],
                                      xsem.at[sl]) for j in range(ROW_TILE)]

    def ostores(step, sl):
        r0 = pl.multiple_of(step * tb, tb)
        return [pltpu.make_async_copy(ot.at[sl, :, j, :], o_ref.at[pl.ds(r0, tb), pl.ds(j * LANES, LANES)],
                                      osem.at[sl]) for j in range(ROW_TILE)]

    def gather_row(d_ref, sl, r):
        for k in range(TOP_K):
            d = d_ref[0, r * TOP_K + k]
            pltpu.make_async_copy(ys_ref.at[d], gbuf.at[sl, k, r], gsem.at[sl]).start(priority=k % 2)

    def wait_gathers(sl):
        for k in range(TOP_K):
            pltpu.make_async_copy(ys_ref.at[pl.ds(0, tb)], gbuf.at[sl, k], gsem.at[sl]).wait()

    @pl.when(s == 0)
    def _():
        def issue(r, c):
            gather_row(dc_ref, 0, r)
            return c

        lax.fori_loop(0, tb, issue, 0, unroll=8)
        for c in xloads(0, 0):
            c.start()

    wait_gathers(slot)
    for c in xloads(s, slot):
        c.wait()

    @pl.when(s >= 2)
    def _():
        for c in ostores(s - 2, slot):
            c.wait()

    @pl.when(s + 1 < nsteps)
    def _():
        def issue(r, c):
            gather_row(dn_ref, other, r)
            return c

        lax.fori_loop(0, tb, issue, 0, unroll=8)
        for c in xloads(s + 1, other):
            c.start()

    g2 = gt2_ref[...]

    def wsum(r, c):
        acc = tw_ref[0, r * TOP_K] * gbuf[slot, 0, r]
        for k in range(1, TOP_K):
            acc = acc + tw_ref[0, r * TOP_K + k] * gbuf[slot, k, r]
        ot[slot, r] = xt[slot, r] + g2 * acc
        return c

    lax.fori_loop(0, tb, wsum, 0, unroll=8)
    for c in ostores(s, slot):
        c.start()

    @pl.when(s == nsteps - 1)
    def _():
        for c in ostores(s, slot):
            c.wait()
        if nsteps >= 2:
            for c in ostores(s - 1, other):
                c.wait()


def _combine(dest4, tw4, gt2t, xn, ys, *, tb, seq):
    n, D = xn.shape
    nsteps = n // tb
    cur = lambda s: (s, 0, 0)
    nxt = lambda s: (jnp.minimum(s + 1, nsteps - 1), 0, 0)
    idx = lambda f: pl.BlockSpec((None, 1, tb * TOP_K), f, memory_space=pltpu.SMEM)
    anyspec = pl.BlockSpec(memory_space=pl.ANY)
    tile = (tb, ROW_TILE, LANES)
    return pl.pallas_call(
        functools.partial(_combine_kernel, tb=tb, nsteps=nsteps),
        out_shape=jax.ShapeDtypeStruct((n, D), F32),
        grid=(nsteps,),
        in_specs=[idx(cur), idx(nxt), idx(cur),
                  pl.BlockSpec((None, ROW_TILE, LANES), lambda s: ((s * tb) // seq, 0, 0)), anyspec, anyspec],
        out_specs=anyspec,
        scratch_shapes=[pltpu.VMEM((2, TOP_K) + tile, F32), pltpu.VMEM((2,) + tile, F32),
                        pltpu.VMEM((2,) + tile, F32), pltpu.SemaphoreType.DMA((2,)),
                        pltpu.SemaphoreType.DMA((2,)), pltpu.SemaphoreType.DMA((2,))],
        compiler_params=_cparams(("arbitrary",)),
        name="combine",
    )(dest4, dest4, tw4, gt2t, xn, ys)


def _rope_tables(T):
    rows = T // GRID_W
    row = jnp.repeat(jnp.arange(rows, dtype=F32), GRID_W)
    col = jnp.tile(jnp.arange(GRID_W, dtype=F32), rows)
    inv = ROPE_BASE ** (-jnp.arange(0, AXIS_ROT, 2, dtype=F32) / AXIS_ROT)
    ang_r, ang_c = row[:, None] * inv, col[:, None] * inv
    m = AXIS_ROT // 2
    ang = jnp.concatenate([ang_r, ang_r, ang_c, ang_c], axis=1)
    sign = jnp.tile(jnp.concatenate([-jnp.ones((m,), F32), jnp.ones((m,), F32)]), 2)
    cos = jnp.tile(jnp.cos(ang), (1, LANES // HEAD_DIM))
    sin = jnp.tile(jnp.sin(ang) * sign, (1, LANES // HEAD_DIM))
    return cos, sin


def _head_perm():
    order = []
    for m in range(ATT_GROUP):
        for kv in range(ATT_KV_HEADS):
            h = kv * ATT_GROUP + m
            order.extend(range(h * HEAD_DIM, (h + 1) * HEAD_DIM))
    return np.asarray(order)


def kernel(x, c, ctx, c_ctx, w_mod, b_mod, norm1, norm2, w_in, q_norm, k_norm, attn_sink,
           w_alpha_f, b_alpha_f, w_alpha_b, b_alpha_b, gla_norm, w_branch_attn, w_branch_gla,
           w_out, w_router, b_router, w_exp_in, b_exp_in, w_exp_out, b_exp_out):
    B, T, D = x.shape
    depth = w_mod.shape[0]
    assert depth == 1, "single-layer kernel: the context stream update only feeds later layers"
    l = 0
    perm = _head_perm()

    rows = ((B + 1 + 7) // 8) * 8
    c_all = jnp.zeros((rows, D), F32).at[:B].set(c).at[B].set(c_ctx)
    mod = _modulation(c_all, w_mod[l], b_mod[l])
    sh1, sc1, gt1, sh2, sc2, gt2 = [mod[:B, j * D:(j + 1) * D].reshape(B, 1, D) for j in range(6)]
    csh1, csc1 = [jnp.broadcast_to(mod[B, j * D:(j + 1) * D].reshape(1, 1, D), (B, 1, D)) for j in range(2)]

    offs = np.concatenate([[0], np.cumsum(IN_SPLITS)])
    cols = lambda j: w_in[l][:, offs[j]:offs[j + 1]]
    wal = jnp.zeros((2 * GLA_RANK, 2 * GLA_K_W), F32)
    wal = wal.at[:GLA_RANK, :GLA_K_W].set(w_alpha_f[l]).at[GLA_RANK:, GLA_K_W:].set(w_alpha_b[l])
    pw = {
        "wq": cols(0)[:, perm].astype(BF16), "wk": cols(1).astype(BF16), "wv": cols(2).astype(BF16),
        "wgq": cols(3).astype(BF16), "wgk": cols(4).astype(BF16), "wgv": cols(5).astype(BF16),
        "wgr": cols(6).astype(BF16), "wga": cols(9).astype(BF16), "wgg": cols(10).astype(BF16),
        "wlr": jnp.concatenate([cols(7), cols(8)], axis=1).astype(BF16),
        "qn": jnp.tile(q_norm[l], LANES // HEAD_DIM).reshape(1, LANES),
        "kn": jnp.tile(k_norm[l], LANES // HEAD_DIM).reshape(1, LANES),
        "wal": wal.astype(BF16),
        "bal": jnp.concatenate([b_alpha_f[l], b_alpha_b[l]]).reshape(1, 2 * GLA_K_W),
    }
    cos, sin = _rope_tables(T)
    n1 = norm1[l].reshape(1, D)
    tm = min(512, T)
    aq, ak, av, gq, gk, gv, gr, ga, gg, la = _inproj(
        x, sh1, sc1, n1, {"cos": cos, "sin": sin}, pw, rope=True, full=True, tm=tm)
    cak, cav, cgk, cgv, cla = _inproj(
        ctx, csh1, csc1, n1, None, pw, rope=False, full=False, tm=min(256, ctx.shape[1]))

    attn_o = _attention(attn_sink[l], aq, ak, av, cak, cav)
    gla_o = _gla(gq, gk, gv, la, cgk, cgv, cla)

    wr = jnp.zeros((D, LANES), F32).at[:, :N_EXPERTS].set(w_router[l])
    wrh = wr.astype(BF16)
    mw = {
        "gn": jnp.tile(gla_norm[l], GLA_HEADS).reshape(1, GLA_V_W),
        "wba": w_branch_attn[l][perm, :].astype(BF16), "wbg": w_branch_gla[l].astype(BF16),
        "wo": w_out[l].astype(BF16), "wrh": wrh, "wrl": (wr - wrh.astype(F32)).astype(BF16),
        "br": jnp.full((1, LANES), NEG, F32).at[0, :N_EXPERTS].set(b_router[l]),
    }
    xn, h2, ti, rk, tw, cnt = _merge(x, attn_o, gla_o, gr, ga, gg, gt1, sc2, sh2, norm2[l].reshape(1, D), mw, tm=tm)

    n = B * T
    counts = cnt[0, :N_EXPERTS]
    padded = (counts + MOE_STEP - 1) // MOE_STEP * MOE_STEP
    pad_end = jnp.cumsum(padded)
    pad_start = (pad_end - padded).astype(jnp.int32)
    zstart = jnp.maximum(pad_end - MOE_STEP, 0).astype(jnp.int32)
    cap = (n * TOP_K + N_EXPERTS * (MOE_STEP - 1)) // MOE_STEP * MOE_STEP
    n_blk = cap // MOE_STEP
    row0 = jnp.arange(n_blk, dtype=jnp.int32) * MOE_STEP
    blk_e = jnp.minimum(jnp.sum((pad_end[None, :] <= row0[:, None]).astype(jnp.int32), axis=1), N_EXPERTS - 1)
    onehot = (blk_e[:, None] == jnp.arange(N_EXPERTS, dtype=jnp.int32)[None, :]).astype(jnp.int32)
    valid_end = jnp.sum(onehot * (pad_start + counts)[None, :], axis=1)
    nv = jnp.clip(valid_end - row0, 0, MOE_STEP).astype(jnp.int32)
    n_used = (pad_end[-1] // MOE_STEP).astype(jnp.int32).reshape(1)
    changed = jnp.concatenate([jnp.ones((1,), bool), blk_e[1:] != blk_e[:-1]])
    first = (changed & (nv > 0)).astype(jnp.int32)
    parity = ((jnp.cumsum(first) - 1) % 2).astype(jnp.int32)
    eid = jnp.arange(N_EXPERTS, dtype=jnp.int32)
    later = (eid[None, :] > eid[:, None]) & (counts[None, :] > 0)
    nxt_of = jnp.where(jnp.any(later, axis=1), jnp.argmax(later, axis=1), -1).astype(jnp.int32)
    nxt_e = jnp.sum(onehot * nxt_of[None, :], axis=1).astype(jnp.int32)

    tb = min(256, T)
    dest = _dest(ti.reshape(n, LANES), rk.reshape(n, LANES), pad_start, tm=min(2048, n))
    dest4 = dest[:, :TOP_K].reshape(n // tb, 1, tb * TOP_K)
    tw4 = tw[:, :, :TOP_K].reshape(n // tb, 1, tb * TOP_K)
    xs = _dispatch(dest4, zstart, n_used, h2.reshape(n, D), cap, tb=tb)
    ys = _experts(blk_e, nv, first, nxt_e, parity, xs, w_exp_in[l], b_exp_in[l], w_exp_out[l], b_exp_out[l])
    out = _combine(dest4, tw4, gt2.reshape(B, ROW_TILE, LANES), xn.reshape(n, D), ys, tb=tb, seq=T)
    return out.reshape(B, T, D)
```

```python
import functools

import numpy as np
import jax
import jax.numpy as jnp
from jax import lax
from jax.experimental import pallas as pl
from jax.experimental.pallas import tpu as pltpu

F32 = jnp.float32
BF16 = jnp.bfloat16

D_MODEL = 1024
GRID_W = 64
EPS = 1e-6
ATT_HEADS = 8
ATT_KV_HEADS = 2
ATT_GROUP = ATT_HEADS // ATT_KV_HEADS
HEAD_DIM = 64
WINDOW = 128
ATT_BLOCK = 128
ROPE_BASE = 10000.0
AXIS_ROT = HEAD_DIM // 2
GLA_HEADS = 4
GLA_DK = 64
GLA_DV = 128
GLA_RANK = 16
GLA_TAU = 16.0
N_EXPERTS = 32
TOP_K = 4
D_FF = D_MODEL
SWIGLU_ALPHA = 1.702
SWIGLU_LIMIT = 7.0
MOE_STEP = 512

ATT_W = ATT_HEADS * HEAD_DIM
ATT_KV_W = ATT_KV_HEADS * HEAD_DIM
GLA_K_W = GLA_HEADS * GLA_DK
GLA_V_W = GLA_HEADS * GLA_DV
IN_SPLITS = (ATT_W, ATT_KV_W, ATT_KV_W, GLA_K_W, GLA_K_W, GLA_V_W, GLA_V_W, GLA_RANK, GLA_RANK, D_MODEL, D_MODEL)

LANES = 128
ROW_TILE = D_MODEL // LANES
VMEM_LIMIT = 56 * 1024 * 1024
NEG = -1e30

GLA_C = 64
GLA_SUB = 4
GLA_LEVELS = 4


def _cparams(sem):
    return pltpu.CompilerParams(dimension_semantics=sem, vmem_limit_bytes=VMEM_LIMIT)


def _full(shape):
    n = len(shape)
    return pl.BlockSpec(shape, lambda *_: (0,) * n)


def _mod_kernel(c_ref, w_ref, b_ref, o_ref):
    c = c_ref[...]
    s = c * (1.0 / (1.0 + jnp.exp(-c)))
    o_ref[...] = jnp.dot(s, w_ref[...], preferred_element_type=F32,
                         precision=lax.Precision.HIGHEST) + b_ref[...]


def _modulation(c_all, w_mod, b_mod):
    rows = c_all.shape[0]
    n = w_mod.shape[1]
    tn = 1536
    return pl.pallas_call(
        _mod_kernel,
        out_shape=jax.ShapeDtypeStruct((rows, n), F32),
        grid=(n // tn,),
        in_specs=[pl.BlockSpec((rows, D_MODEL), lambda j: (0, 0)),
                  pl.BlockSpec((D_MODEL, tn), lambda j: (0, j)),
                  pl.BlockSpec((1, tn), lambda j: (0, j))],
        out_specs=pl.BlockSpec((rows, tn), lambda j: (0, j)),
        compiler_params=_cparams(("arbitrary",)),
        name="mod",
    )(c_all, w_mod, b_mod.reshape(1, n))


def _pair_norm(a, g, lo):
    s = a * a
    tot = jnp.sum(s, axis=-1, keepdims=True)
    slo = jnp.sum(jnp.where(lo, s, 0.0), axis=-1, keepdims=True)
    ms = jnp.where(lo, slo, tot - slo) * (1.0 / HEAD_DIM)
    return a * lax.rsqrt(ms + EPS) * g


def _rope(y, cos, sin, first):
    up = pltpu.roll(y, LANES - AXIS_ROT // 2, 1)
    dn = pltpu.roll(y, AXIS_ROT // 2, 1)
    return y * cos + jnp.where(first, up, dn) * sin


def _inproj_kernel(*refs, rope, full):
    if full:
        (x_ref, sh_ref, sc_ref, n1_ref, cos_ref, sin_ref, qn_ref, kn_ref, wal_ref, bal_ref,
         wq, wk, wv, wgq, wgk, wgv, wgr, wga, wgg, wlr,
         oq, ok, ov, ogq, ogk, ogv, ogr, oga, ogg, ola) = refs
    else:
        (x_ref, sh_ref, sc_ref, n1_ref, kn_ref, wal_ref, bal_ref,
         wk, wv, wgk, wgv, wlr,
         ok, ov, ogk, ogv, ola) = refs
    x = x_ref[...]
    tm = x.shape[0]
    ms = jnp.mean(x * x, axis=-1, keepdims=True)
    h = (x * lax.rsqrt(ms + EPS) * n1_ref[...]) * (1.0 + sc_ref[...]) + sh_ref[...]
    hb = h.astype(BF16)

    def proj(w_ref):
        return jnp.dot(hb, w_ref[...], preferred_element_type=F32)

    lane = lax.broadcasted_iota(jnp.int32, (tm, LANES), 1)
    lo = lane < HEAD_DIM
    first = (lane % AXIS_ROT) < (AXIS_ROT // 2)
    if rope:
        cos = cos_ref[...]
        sin = sin_ref[...]

    k = _pair_norm(proj(wk), kn_ref[...], lo)
    if rope:
        k = _rope(k, cos, sin, first)
    ok[...] = k.astype(BF16)
    ov[...] = proj(wv).astype(BF16)
    ogk[...] = proj(wgk).astype(BF16)
    ogv[...] = proj(wgv).astype(BF16)
    lr = proj(wlr).astype(BF16)
    z = jnp.dot(lr, wal_ref[...], preferred_element_type=F32) + bal_ref[...]
    ola[...] = (jnp.minimum(z, 0.0) - jnp.log(1.0 + jnp.exp(-jnp.abs(z)))) * (1.0 / GLA_TAU)
    if full:
        q = proj(wq)
        for p in range(ATT_W // LANES):
            y = _pair_norm(q[:, p * LANES:(p + 1) * LANES], qn_ref[...], lo)
            if rope:
                y = _rope(y, cos, sin, first)
            oq[:, p * LANES:(p + 1) * LANES] = (y * HEAD_DIM ** -0.5).astype(BF16)
        ogq[...] = (proj(wgq) * GLA_DK ** -0.5).astype(BF16)
        g = proj(wgr)
        ogr[...] = (g * (1.0 / (1.0 + jnp.exp(-g)))).astype(BF16)
        oga[...] = (1.0 / (1.0 + jnp.exp(-proj(wga)))).astype(BF16)
        ogg[...] = (1.0 / (1.0 + jnp.exp(-proj(wgg)))).astype(BF16)


def _inproj(x, sh, sc, norm1, tabs, wts, *, rope, full, tm):
    B, T, D = x.shape
    grid = (B, T // tm)
    row = lambda w: pl.BlockSpec((None, tm, w), lambda b, t: (b, t, 0))
    vec = pl.BlockSpec((None, 1, D), lambda b, t: (b, 0, 0))
    tab = pl.BlockSpec((tm, LANES), lambda b, t: (t, 0))
    if full:
        names = ("wq", "wk", "wv", "wgq", "wgk", "wgv", "wgr", "wga", "wgg", "wlr")
        ins = [x, sh, sc, norm1, tabs["cos"], tabs["sin"], wts["qn"], wts["kn"], wts["wal"], wts["bal"]]
        specs = [row(D), vec, vec, _full((1, D)), tab, tab, _full((1, LANES)), _full((1, LANES)),
                 _full(wts["wal"].shape), _full(wts["bal"].shape)]
        out_w = (ATT_W, ATT_KV_W, ATT_KV_W, GLA_K_W, GLA_K_W, GLA_V_W, GLA_V_W, D, D)
    else:
        names = ("wk", "wv", "wgk", "wgv", "wlr")
        ins = [x, sh, sc, norm1, wts["kn"], wts["wal"], wts["bal"]]
        specs = [row(D), vec, vec, _full((1, D)), _full((1, LANES)),
                 _full(wts["wal"].shape), _full(wts["bal"].shape)]
        out_w = (ATT_KV_W, ATT_KV_W, GLA_K_W, GLA_V_W)
    ins += [wts[n] for n in names]
    specs += [_full(wts[n].shape) for n in names]
    out_shape = [jax.ShapeDtypeStruct((B, T, w), BF16) for w in out_w]
    out_shape.append(jax.ShapeDtypeStruct((B, T, 2 * GLA_K_W), F32))
    out_specs = [row(w) for w in out_w] + [row(2 * GLA_K_W)]
    return pl.pallas_call(
        functools.partial(_inproj_kernel, rope=rope, full=full),
        out_shape=out_shape, grid=grid, in_specs=specs, out_specs=out_specs,
        compiler_params=_cparams(("parallel", "arbitrary")),
        name="inproj_full" if full else "inproj_ctx",
    )(*ins)


def _attn_kernel(*refs, seq, nsb):
    sink_ref, q_ref = refs[:2]
    kblocks = refs[2:nsb + 4]
    kx_ref = refs[nsb + 4]
    vblocks = refs[nsb + 5:2 * nsb + 7]
    vx_ref, o_ref = refs[2 * nsb + 7:]
    n = pl.program_id(1)
    blk = ATT_BLOCK
    lc = kx_ref.shape[0]
    nk = 3 * blk + lc
    nslab = ATT_W // LANES
    rows = nslab * blk
    ri = lax.broadcasted_iota(jnp.int32, (rows, nk), 0)
    kj = lax.broadcasted_iota(jnp.int32, (rows, nk), 1)
    qi = ri % blk
    lane = lax.broadcasted_iota(jnp.int32, (blk, LANES), 1)
    lo = lane < HEAD_DIM
    hrow = lax.broadcasted_iota(jnp.int32, (rows, 1), 0) // blk
    for sb in range(nsb):
        kcat = jnp.concatenate([r[...] for r in kblocks[sb:sb + 3]] + [kx_ref[...]], axis=0)
        vcat = jnp.concatenate([r[...] for r in vblocks[sb:sb + 3]] + [vx_ref[...]], axis=0)
        kpos = (nsb * n + sb - 1) * blk + kj
        valid = ((jnp.abs(kj - blk - qi) <= WINDOW) & (kpos >= 0) & (kpos < seq)) | (kj >= 3 * blk)
        q = q_ref[sb * blk:(sb + 1) * blk, :]
        outs = []
        for kv in range(ATT_KV_HEADS):
            keep = lo if kv == 0 else jnp.logical_not(lo)
            qs = jnp.concatenate([jnp.where(keep, q[:, m * LANES:(m + 1) * LANES], jnp.zeros((blk, LANES), BF16))
                                  for m in range(nslab)], axis=0)
            s = lax.dot_general(qs, kcat, (((1,), (1,)), ((), ())), preferred_element_type=F32)
            s = jnp.where(valid, s, NEG)
            snk = jnp.zeros((rows, 1), F32)
            for m in range(nslab):
                snk = jnp.where(hrow == m, sink_ref[kv * ATT_GROUP + m], snk)
            mx = jnp.maximum(jnp.max(s, axis=-1, keepdims=True), snk)
            p = jnp.exp(s - mx)
            den = jnp.sum(p, axis=-1, keepdims=True) + jnp.exp(snk - mx)
            outs.append(jnp.dot(p.astype(BF16), vcat, preferred_element_type=F32) / den)
        for m in range(nslab):
            o_ref[sb * blk:(sb + 1) * blk, m * LANES:(m + 1) * LANES] = jnp.where(
                lo, outs[0][m * blk:(m + 1) * blk], outs[1][m * blk:(m + 1) * blk]).astype(BF16)


def _attention(sink, aq, ak, av, cak, cav):
    B, T, _ = aq.shape
    lc = cak.shape[1]
    blk = ATT_BLOCK
    nb = T // blk
    nsb = 4 if nb % 4 == 0 else 2
    assert nb % nsb == 0
    kvspec = lambda off: pl.BlockSpec((None, blk, ATT_KV_W),
                                      lambda b, n: (b, jnp.clip(nsb * n + off, 0, nb - 1), 0))
    cspec = pl.BlockSpec((None, lc, ATT_KV_W), lambda b, n: (b, 0, 0))
    qspec = pl.BlockSpec((None, nsb * blk, ATT_W), lambda b, n: (b, n, 0))
    kvs = [kvspec(off) for off in range(-1, nsb + 1)]
    return pl.pallas_call(
        functools.partial(_attn_kernel, seq=T, nsb=nsb),
        out_shape=jax.ShapeDtypeStruct((B, T, ATT_W), BF16),
        grid=(B, nb // nsb),
        in_specs=[pl.BlockSpec(memory_space=pltpu.SMEM), qspec] + kvs + [cspec] + kvs + [cspec],
        out_specs=qspec,
        compiler_params=_cparams(("parallel", "arbitrary")),
        name="attn",
    )(sink, aq, *([ak] * (nsb + 2)), cak, *([av] * (nsb + 2)), cav)


def _gla_constants():
    C, sub, L = GLA_C, GLA_SUB, GLA_LEVELS
    i = np.arange(C)[:, None]
    t = np.arange(C)[None, :]
    tabs = [t <= i]
    rowq, same = [], []
    for l in range(L):
        s = C >> l
        mid = (i // s) * s + s // 2
        rowq.append(np.broadcast_to(i >= mid, (C, C)))
        same.append((i // s) == (t // s))
    shifts, dmask, dvalid = [], [t == i], []
    for d in range(1, sub):
        ok = (i % sub) >= d
        shifts.append(ok & (t == i - d))
        dmask.append(ok & (t == i - d))
        dvalid.append(np.broadcast_to(ok, (C, C)))
    flip = lambda a: a[::-1, ::-1]
    tile = lambda a: np.tile(a, (1, GLA_HEADS))

    def both(xs, lanes):
        f = (lambda a: tile(a)) if lanes else (lambda a: a)
        return np.stack([np.concatenate([f(a) for a in xs], 0),
                         np.concatenate([f(flip(a)) for a in xs], 0)]).astype(np.float32)

    hk = np.arange(GLA_K_W) // GLA_DK
    hv = np.arange(GLA_V_W) // GLA_DV
    ind = (hk[:, None] == hk[None, :]).astype(np.float32)
    bdv = (hk[:, None] == hv[None, :]).astype(np.float32)
    return (both(tabs, False), both(shifts, False), both(rowq, True), both(same, True),
            both(dmask, True), ind, bdv, np.ascontiguousarray(bdv.T), both(dvalid, True))


def _gla_chunk(q_b, k_b, v_b, la, cst, d):
    tri_ref, shm_ref, rq_ref, sm_ref, dm_ref, ind_ref, bdv_ref, bds_ref, dv_ref = cst
    C = GLA_C
    kw = GLA_K_W
    q = q_b.astype(F32)
    k = k_b.astype(F32)
    hi = la.astype(BF16)
    r1 = la - hi.astype(F32)
    mid = r1.astype(BF16)
    lo = (r1 - mid.astype(F32)).astype(BF16)
    b3 = jnp.dot(tri_ref[d], jnp.concatenate([hi, mid, lo], axis=1), preferred_element_type=F32)
    b = b3[:, :kw] + b3[:, kw:2 * kw] + b3[:, 2 * kw:]
    last = b[C - 1:C] if d == 0 else b[0:1]

    qt = (q * jnp.exp(b)).astype(BF16)
    kt = (k * jnp.exp(last - b)).astype(BF16)
    gamma = jnp.exp(last)

    ind = ind_ref[...]
    a = None
    for l in range(GLA_LEVELS):
        s = C >> l
        off = s // 2 - 1 if d == 0 else s // 2
        bref = jnp.concatenate([jnp.broadcast_to(b[st + off:st + off + 1], (s, kw)) for st in range(0, C, s)], axis=0)
        rq = rq_ref[d, l * C:(l + 1) * C, :]
        el = jnp.exp((b - bref) * (2.0 * rq - 1.0))
        qh = (q * (el * rq)).astype(BF16)
        kh = (k * (el * (1.0 - rq))).astype(BF16)
        bdk = jnp.concatenate([kh] * GLA_HEADS, axis=0) * ind
        al = lax.dot_general(qh, bdk, (((1,), (1,)), ((), ())), preferred_element_type=F32)
        a = al if l == 0 else a + al * sm_ref[d, l * C:(l + 1) * C, :]
    ksh = jnp.dot(shm_ref[d], k_b, preferred_element_type=F32)
    ps = [q * k]
    for j in range(1, GLA_SUB):
        bsh = pltpu.roll(b, j if d == 0 else C - j, 0)
        ej = jnp.exp((b - bsh) * dv_ref[d, (j - 1) * C:j * C, :])
        ps.append(q * ksh[(j - 1) * C:j * C] * ej)
    w = jnp.dot(jnp.concatenate(ps, axis=0).astype(BF16), ind, preferred_element_type=F32)
    for j in range(GLA_SUB):
        a = a + w[j * C:(j + 1) * C] * dm_ref[d, j * C:(j + 1) * C, :]

    bdv = jnp.concatenate([v_b] * GLA_HEADS, axis=0) * bdv_ref[...]
    o_intra = jnp.dot(a.astype(BF16), bdv, preferred_element_type=F32)
    upd = lax.dot_general(v_b, kt, (((0,), (0,)), ((), ())), preferred_element_type=F32) * bds_ref[...]
    return o_intra, qt, upd, gamma


def _gla_kernel(qf_ref, kf_ref, vf_ref, laf_ref, qb_ref, kb_ref, vb_ref, lab_ref,
                ckf_ref, cvf_ref, claf_ref, ckb_ref, cvb_ref, clab_ref,
                tri_ref, shm_ref, rq_ref, sm_ref, dm_ref, ind_ref, bdv_ref, bds_ref, dv_ref,
                of_ref, ob_ref, st_ref, *, n_ctx_steps):
    s = pl.program_id(1)
    C = GLA_C
    cst = (tri_ref, shm_ref, rq_ref, sm_ref, dm_ref, ind_ref, bdv_ref, bds_ref, dv_ref)

    @pl.when(s == 0)
    def _():
        st_ref[...] = jnp.zeros_like(st_ref)

    is_ctx = s < n_ctx_steps
    dirs = ((0, qf_ref, kf_ref, vf_ref, laf_ref, ckf_ref, cvf_ref, claf_ref, of_ref),
            (1, qb_ref, kb_ref, vb_ref, lab_ref, ckb_ref, cvb_ref, clab_ref, ob_ref))
    states = [st_ref[0], st_ref[1]]
    for idx in range(2):
        for d, q_ref, k_ref, v_ref, la_ref, ck_ref, cv_ref, cla_ref, o_ref in dirs:
            c = idx if d == 0 else 1 - idx
            rows = slice(c * C, (c + 1) * C)
            k_b = jnp.where(is_ctx, ck_ref[rows, :], k_ref[rows, :])
            v_b = jnp.where(is_ctx, cv_ref[rows, :], v_ref[rows, :])
            la = jnp.where(is_ctx, cla_ref[rows, :], la_ref[rows, :])
            o_intra, qt, upd, gamma = _gla_chunk(q_ref[rows, :], k_b, v_b, la, cst, d)
            st = states[d]
            o_ref[rows, :] = o_intra + lax.dot_general(qt, st.astype(BF16), (((1,), (1,)), ((), ())),
                                                       preferred_element_type=F32)
            states[d] = st * gamma + upd
    st_ref[0] = states[0]
    st_ref[1] = states[1]


def _gla(gq, gk, gv, la, cgk, cgv, cla):
    B, T, _ = gq.shape
    lc = cgk.shape[1]
    R = 2 * GLA_C
    assert lc % R == 0 and T % R == 0
    n_ctx, n_lat = lc // R, T // R
    consts = _gla_constants()
    tri, shm = jnp.asarray(consts[0], BF16), jnp.asarray(consts[1], BF16)
    rq, sm, dm = [jnp.asarray(c) for c in consts[2:5]]
    ind, bdv = jnp.asarray(consts[5], BF16), jnp.asarray(consts[6], BF16)
    bds, dv = jnp.asarray(consts[7]), jnp.asarray(consts[8])

    def lat(s, d):
        j = jnp.maximum(s - n_ctx, 0)
        return j if d == 0 else n_lat - 1 - j

    def ctx(s, d):
        j = jnp.minimum(s, n_ctx - 1)
        return j if d == 0 else n_ctx - 1 - j

    lspec = lambda w, d, c=0: pl.BlockSpec((None, R, w), lambda b, s: (b, lat(s, d), c))
    cspec = lambda w, d, c=0: pl.BlockSpec((None, R, w), lambda b, s: (b, ctx(s, d), c))
    lat_specs = lambda d: [lspec(GLA_K_W, d), lspec(GLA_K_W, d), lspec(GLA_V_W, d), lspec(GLA_K_W, d, d)]
    ctx_specs = lambda d: [cspec(GLA_K_W, d), cspec(GLA_V_W, d), cspec(GLA_K_W, d, d)]
    cs = [tri, shm, rq, sm, dm, ind, bdv, bds, dv]
    return pl.pallas_call(
        functools.partial(_gla_kernel, n_ctx_steps=n_ctx),
        out_shape=[jax.ShapeDtypeStruct((B, T, GLA_V_W), F32)] * 2,
        grid=(B, n_ctx + n_lat),
        in_specs=lat_specs(0) + lat_specs(1) + ctx_specs(0) + ctx_specs(1) + [_full(c.shape) for c in cs],
        out_specs=[lspec(GLA_V_W, 0), lspec(GLA_V_W, 1)],
        scratch_shapes=[pltpu.VMEM((2, GLA_V_W, GLA_K_W), F32)],
        compiler_params=_cparams(("parallel", "arbitrary")),
        name="gla",
    )(gq, gk, gv, la, gq, gk, gv, la, cgk, cgv, cla, cgk, cgv, cla, *cs)


def _merge_kernel(x_ref, at_ref, of_ref, ob_ref, gr_ref, ga_ref, gg_ref, gt1_ref, sc2_ref, sh2_ref,
                  n2_ref, gn_ref, ltri_ref, wba_ref, wbg_ref, wo_ref, wrh_ref, wrl_ref, br_ref,
                  xn_ref, h2_ref, ti_ref, rk_ref, tw_ref, cnt_ref, carry_ref):
    tm = x_ref.shape[0]

    @pl.when((pl.program_id(0) == 0) & (pl.program_id(1) == 0))
    def _():
        carry_ref[...] = jnp.zeros_like(carry_ref)

    go = of_ref[...] + ob_ref[...]
    parts = []
    for h in range(GLA_HEADS):
        gh = go[:, h * GLA_DV:(h + 1) * GLA_DV]
        ms = jnp.mean(gh * gh, axis=-1, keepdims=True)
        parts.append(gh * lax.rsqrt(ms + EPS))
    o = jnp.concatenate(parts, axis=1) * gn_ref[...] * gr_ref[...].astype(F32)
    ya = jnp.dot(at_ref[...], wba_ref[...], preferred_element_type=F32)
    yg = jnp.dot(o.astype(BF16), wbg_ref[...], preferred_element_type=F32)
    y = ga_ref[...].astype(F32) * ya + gg_ref[...].astype(F32) * yg
    z = jnp.dot(y.astype(BF16), wo_ref[...], preferred_element_type=F32)
    xn = x_ref[...] + gt1_ref[...] * z
    xn_ref[...] = xn
    ms = jnp.mean(xn * xn, axis=-1, keepdims=True)
    h2 = (xn * lax.rsqrt(ms + EPS) * n2_ref[...]) * (1.0 + sc2_ref[...]) + sh2_ref[...]
    hh = h2.astype(BF16)
    hl = (h2 - hh.astype(F32)).astype(BF16)
    h2_ref[...] = h2
    logits = (jnp.dot(hh, wrh_ref[...], preferred_element_type=F32)
              + jnp.dot(hl, wrh_ref[...], preferred_element_type=F32)
              + jnp.dot(hh, wrl_ref[...], preferred_element_type=F32)) + br_ref[...]
    lane = lax.broadcasted_iota(jnp.int32, (tm, LANES), 1).astype(F32)
    vals, idxs = [], []
    l = logits
    for _ in range(TOP_K):
        m = jnp.max(l, axis=-1, keepdims=True)
        ix = jnp.min(jnp.where(l == m, lane, float(LANES)), axis=-1, keepdims=True)
        vals.append(m)
        idxs.append(ix)
        l = jnp.where(lane == ix, -3.0e38, l)
    ex = [jnp.exp(v - vals[0]) for v in vals]
    den = ex[0] + ex[1] + ex[2] + ex[3]
    mh = jnp.zeros((tm, LANES), F32)
    for j in range(TOP_K):
        mh = mh + jnp.where(lane == idxs[j], 1.0, 0.0)
    pc = jnp.dot(ltri_ref[...], mh.astype(BF16), preferred_element_type=F32) + carry_ref[...]
    ti = jnp.zeros((tm, LANES), F32)
    rk = jnp.zeros((tm, LANES), F32)
    tw = jnp.zeros((tm, LANES), F32)
    for j in range(TOP_K):
        rj = jnp.sum(jnp.where(lane == idxs[j], pc, 0.0), axis=-1, keepdims=True)
        ti = jnp.where(lane == float(j), idxs[j], ti)
        rk = jnp.where(lane == float(j), rj, rk)
        tw = jnp.where(lane == float(j), ex[j] / den, tw)
    ti_ref[...] = ti.astype(jnp.int32)
    rk_ref[...] = rk.astype(jnp.int32)
    tw_ref[...] = tw
    total = carry_ref[...] + jnp.sum(mh, axis=0, keepdims=True)
    carry_ref[...] = total
    cnt_ref[...] = total.astype(jnp.int32)


def _merge(x, attn_o, gla_o, gr, ga, gg, gt1, sc2, sh2, norm2, wts, *, tm):
    B, T, D = x.shape
    row = lambda w: pl.BlockSpec((None, tm, w), lambda b, t: (b, t, 0))
    vec = pl.BlockSpec((None, 1, D), lambda b, t: (b, 0, 0))
    names = ("wba", "wbg", "wo", "wrh", "wrl", "br")
    ltri = jnp.asarray(np.tril(np.ones((tm, tm), np.float32), -1), BF16)
    return pl.pallas_call(
        _merge_kernel,
        out_shape=[jax.ShapeDtypeStruct((B, T, D), F32), jax.ShapeDtypeStruct((B, T, D), F32),
                   jax.ShapeDtypeStruct((B, T, LANES), jnp.int32), jax.ShapeDtypeStruct((B, T, LANES), jnp.int32),
                   jax.ShapeDtypeStruct((B, T, LANES), F32), jax.ShapeDtypeStruct((1, LANES), jnp.int32)],
        grid=(B, T // tm),
        in_specs=[row(D), row(ATT_W), row(GLA_V_W), row(GLA_V_W),
                  row(GLA_V_W), row(D), row(D), vec, vec, vec,
                  _full((1, D)), _full((1, GLA_V_W)), _full((tm, tm))] + [_full(wts[n].shape) for n in names],
        out_specs=[row(D), row(D), row(LANES), row(LANES), row(LANES), _full((1, LANES))],
        scratch_shapes=[pltpu.VMEM((1, LANES), F32)],
        compiler_params=_cparams(("arbitrary", "arbitrary")),
        name="merge",
    )(x, attn_o, gla_o[0], gla_o[1], gr, ga, gg, gt1, sc2, sh2, norm2, wts["gn"], ltri, *[wts[n] for n in names])


def _dest_kernel(ti_ref, rk_ref, ps_ref, o_ref):
    tm = ti_ref.shape[0]
    lane = lax.broadcasted_iota(jnp.int32, (tm, LANES), 1)
    ti = ti_ref[...]
    ps = ps_ref[...].astype(F32)
    out = jnp.zeros((tm, LANES), F32)
    for k in range(TOP_K):
        start = jnp.sum(jnp.where(lane == ti[:, k:k + 1], ps, 0.0), axis=-1, keepdims=True)
        out = jnp.where(lane == k, start, out)
    o_ref[...] = out.astype(jnp.int32) + jnp.where(lane < TOP_K, rk_ref[...], 0)


def _dest(ti, rk, pad_start, *, tm):
    n = ti.shape[0]
    ps = jnp.zeros((1, LANES), jnp.int32).at[0, :N_EXPERTS].set(pad_start)
    row = pl.BlockSpec((tm, LANES), lambda i: (i, 0))
    return pl.pallas_call(
        _dest_kernel, out_shape=jax.ShapeDtypeStruct((n, LANES), jnp.int32), grid=(n // tm,),
        in_specs=[row, row, _full((1, LANES))], out_specs=row,
        compiler_params=_cparams(("arbitrary",)), name="dest",
    )(ti, rk, ps)


def _dispatch_kernel(d_ref, zs_ref, nu_ref, h_ref, xs_ref, buf, zbuf, isem, sem, zsem,
                     *, tb, nsteps):
    s = pl.program_id(0)
    nblk = xs_ref.shape[0] // MOE_STEP
    slot = s % 3
    nxt = (s + 1) % 3

    def loads(step, sl):
        r0 = pl.multiple_of(step * tb, tb)
        return [pltpu.make_async_copy(h_ref.at[pl.ds(r0, tb), pl.ds(j * LANES, LANES)], buf.at[sl, :, j, :],
                                      isem.at[sl]) for j in range(ROW_TILE)]

    def wait_rows(sl):
        for _ in range(TOP_K):
            pltpu.make_async_copy(buf.at[sl], xs_ref.at[pl.ds(0, tb)], sem.at[sl]).wait()

    @pl.when(s == 0)
    def _():
        zbuf[...] = jnp.zeros_like(zbuf)

        def zstart(e, c):
            z0 = pl.multiple_of(zs_ref[e], MOE_STEP)
            pltpu.make_async_copy(zbuf, xs_ref.at[pl.ds(z0, MOE_STEP)], zsem).start()
            return c

        def zwait(e, c):
            pltpu.make_async_copy(zbuf, xs_ref.at[pl.ds(0, MOE_STEP)], zsem).wait()
            return c

        lax.fori_loop(0, N_EXPERTS, zstart, 0)
        lax.fori_loop(0, N_EXPERTS, zwait, 0)

        def tstart(j, c):
            pltpu.make_async_copy(zbuf, xs_ref.at[pl.ds(pl.multiple_of(j * MOE_STEP, MOE_STEP), MOE_STEP)],
                                  zsem).start()
            return c

        lax.fori_loop(nu_ref[0], nblk, tstart, 0)
        lax.fori_loop(nu_ref[0], nblk, zwait, 0)
        for c in loads(0, 0):
            c.start()

    for c in loads(s, slot):
        c.wait()

    @pl.when(s + 1 < nsteps)
    def _():
        @pl.when(s >= 2)
        def _():
            wait_rows(nxt)
        for c in loads(s + 1, nxt):
            c.start()

    def issue(r, c):
        for k in range(TOP_K):
            d = d_ref[0, r * TOP_K + k]
            pltpu.make_async_copy(buf.at[slot, r], xs_ref.at[d], sem.at[slot]).start(priority=k % 2)
        return c

    lax.fori_loop(0, tb, issue, 0, unroll=8)

    @pl.when(s == nsteps - 1)
    def _():
        wait_rows(slot)
        if nsteps >= 2:
            wait_rows((s + 2) % 3)
        if nsteps >= 3:
            wait_rows(nxt)


def _dispatch(dest4, zstart, n_used, h2, cap, *, tb):
    n, D = h2.shape
    nsteps = n // tb
    idx = pl.BlockSpec((None, 1, tb * TOP_K), lambda s: (s, 0, 0), memory_space=pltpu.SMEM)
    smem = pl.BlockSpec(memory_space=pltpu.SMEM)
    anyspec = pl.BlockSpec(memory_space=pl.ANY)
    return pl.pallas_call(
        functools.partial(_dispatch_kernel, tb=tb, nsteps=nsteps),
        out_shape=jax.ShapeDtypeStruct((cap, ROW_TILE, LANES), F32),
        grid=(nsteps,),
        in_specs=[idx, smem, smem, anyspec],
        out_specs=anyspec,
        scratch_shapes=[pltpu.VMEM((3, tb, ROW_TILE, LANES), F32), pltpu.VMEM((MOE_STEP, ROW_TILE, LANES), F32),
                        pltpu.SemaphoreType.DMA((3,)), pltpu.SemaphoreType.DMA((3,)), pltpu.SemaphoreType.DMA(())],
        compiler_params=_cparams(("arbitrary",)),
        name="dispatch",
    )(dest4, zstart, n_used, h2)


def _expert_kernel(be_ref, nv_ref, fs_ref, nx_ref, pr_ref, xs_ref, w1_ref, b1_ref, w2_ref, b2_ref, ys_ref,
                   w1b, w2b, w1f, w2f, xin, yout, isem, osem, wsem, *, nsteps):
    i = pl.program_id(0)
    slot = i % 2

    def wloads(ex, sl):
        return [pltpu.make_async_copy(w1_ref.at[ex], w1f.at[sl], wsem.at[sl]),
                pltpu.make_async_copy(w2_ref.at[ex], w2f.at[sl], wsem.at[sl])]

    def loads(step, sl):
        r0 = pl.multiple_of(step * MOE_STEP, MOE_STEP)
        return [pltpu.make_async_copy(xs_ref.at[pl.ds(r0, MOE_STEP), j, :],
                                      xin.at[sl, :, pl.ds(j * LANES, LANES)], isem.at[sl]) for j in range(ROW_TILE)]

    def stores(step, sl):
        r0 = pl.multiple_of(step * MOE_STEP, MOE_STEP)
        return [pltpu.make_async_copy(yout.at[sl, :, pl.ds(j * LANES, LANES)],
                                      ys_ref.at[pl.ds(r0, MOE_STEP), j, :], osem.at[sl]) for j in range(ROW_TILE)]

    @pl.when(i == 0)
    def _():
        for c in loads(0, 0):
            c.start()
        for c in wloads(be_ref[0], 0):
            c.start()

    @pl.when(fs_ref[i] == 1)
    def _():
        par = pr_ref[i]
        for c in wloads(be_ref[i], par):
            c.wait()
        w1b[...] = w1f[par].astype(BF16)
        w2b[...] = w2f[par].astype(BF16)

        @pl.when(nx_ref[i] >= 0)
        def _():
            for c in wloads(nx_ref[i], 1 - par):
                c.start(priority=1)

    for c in loads(i, slot):
        c.wait()

    @pl.when(i + 1 < nsteps)
    def _():
        for c in loads(i + 1, 1 - slot):
            c.start()

    @pl.when(i >= 2)
    def _():
        for c in stores(i - 2, slot):
            c.wait()

    def mlp(rows):
        xb = xin[slot, 0:rows, :].astype(BF16)
        y = jnp.zeros((rows, D_MODEL), F32)
        fh = D_FF // 2
        for h in range(2):
            g = jnp.dot(xb, w1b[:, h * fh:(h + 1) * fh], preferred_element_type=F32) + b1_ref[:, h * fh:(h + 1) * fh]
            u = (jnp.dot(xb, w1b[:, D_FF + h * fh:D_FF + (h + 1) * fh], preferred_element_type=F32)
                 + b1_ref[:, D_FF + h * fh:D_FF + (h + 1) * fh])
            gate = jnp.minimum(g, SWIGLU_LIMIT)
            up = jnp.clip(u, -SWIGLU_LIMIT, SWIGLU_LIMIT)
            act = gate * (1.0 / (1.0 + jnp.exp(-SWIGLU_ALPHA * gate))) * (up + 1.0)
            y = y + jnp.dot(act.astype(BF16), w2b[h * fh:(h + 1) * fh, :], preferred_element_type=F32)
        yout[slot, 0:rows, :] = y + b2_ref[...]

    nv = nv_ref[i]
    half = MOE_STEP // 2

    @pl.when(nv > half)
    def _():
        mlp(MOE_STEP)

    @pl.when((nv > 0) & (nv <= half))
    def _():
        mlp(half)
        yout[slot, half:, :] = jnp.zeros((MOE_STEP - half, D_MODEL), F32)

    @pl.when(nv == 0)
    def _():
        yout[slot] = jnp.zeros((MOE_STEP, D_MODEL), F32)

    for c in stores(i, slot):
        c.start()

    @pl.when(i == nsteps - 1)
    def _():
        for c in stores(i, slot):
            c.wait()
        if nsteps >= 2:
            for c in stores(i - 1, 1 - slot):
                c.wait()


def _experts(blk_e, nv, first, nxt_e, parity, xs, w1, b1, w2, b2):
    cap = xs.shape[0]
    n_blk = cap // MOE_STEP
    ne = w1.shape[0]
    anyspec = pl.BlockSpec(memory_space=pl.ANY)
    bias = lambda w: pl.BlockSpec((None, 1, w), lambda i, be, *_: (be[i], 0, 0))
    gs = pltpu.PrefetchScalarGridSpec(
        num_scalar_prefetch=5, grid=(n_blk,),
        in_specs=[anyspec, anyspec, bias(2 * D_FF), anyspec, bias(D_MODEL)],
        out_specs=anyspec,
        scratch_shapes=[pltpu.VMEM((D_MODEL, 2 * D_FF), BF16), pltpu.VMEM((D_FF, D_MODEL), BF16),
                        pltpu.VMEM((2, D_MODEL, 2 * D_FF), F32), pltpu.VMEM((2, D_FF, D_MODEL), F32),
                        pltpu.VMEM((2, MOE_STEP, D_MODEL), F32), pltpu.VMEM((2, MOE_STEP, D_MODEL), F32),
                        pltpu.SemaphoreType.DMA((2,)), pltpu.SemaphoreType.DMA((2,)),
                        pltpu.SemaphoreType.DMA((2,))])
    return pl.pallas_call(
        functools.partial(_expert_kernel, nsteps=n_blk), grid_spec=gs,
        out_shape=jax.ShapeDtypeStruct((cap, ROW_TILE, LANES), F32),
        compiler_params=_cparams(("arbitrary",)),
        name="experts",
    )(blk_e, nv, first, nxt_e, parity, xs, w1, b1.reshape(ne, 1, 2 * D_FF), w2, b2.reshape(ne, 1, D_MODEL))


def _combine_kernel(dc_ref, dn_ref, tw_ref, gt2_ref, xn_ref, ys_ref, o_ref,
                    gbuf, xt, ot, gsem, xsem, osem, *, tb, nsteps):
    s = pl.program_id(0)
    slot = s % 2
    other = 1 - slot

    def xloads(step, sl):
        r0 = pl.multiple_of(step * tb, tb)
        return [pltpu.make_async_copy(xn_ref.at[pl.ds(r0, tb), pl.ds(j * LANES, LANES)], xt.at[sl, :, j, :],
                                      xsem.at[sl]) for j in range(ROW_TILE)]

    def ostores(step, sl):
        r0 = pl.multiple_of(step * tb, tb)
        return [pltpu.make_async_copy(ot.at[sl, :, j, :], o_ref.at[pl.ds(r0, tb), pl.ds(j * LANES, LANES)],
                                      osem.at[sl]) for j in range(ROW_TILE)]

    def gather_row(d_ref, sl, r):
        for k in range(TOP_K):
            d = d_ref[0, r * TOP_K + k]
            pltpu.make_async_copy(ys_ref.at[d], gbuf.at[sl, k, r], gsem.at[sl]).start(priority=k % 2)

    def wait_gathers(sl):
        for k in range(TOP_K):
            pltpu.make_async_copy(ys_ref.at[pl.ds(0, tb)], gbuf.at[sl, k], gsem.at[sl]).wait()

    @pl.when(s == 0)
    def _():
        def issue(r, c):
            gather_row(dc_ref, 0, r)
            return c

        lax.fori_loop(0, tb, issue, 0, unroll=8)
        for c in xloads(0, 0):
            c.start()

    wait_gathers(slot)
    for c in xloads(s, slot):
        c.wait()

    @pl.when(s >= 2)
    def _():
        for c in ostores(s - 2, slot):
            c.wait()

    @pl.when(s + 1 < nsteps)
    def _():
        def issue(r, c):
            gather_row(dn_ref, other, r)
            return c

        lax.fori_loop(0, tb, issue, 0, unroll=8)
        for c in xloads(s + 1, other):
            c.start()

    g2 = gt2_ref[...]

    def wsum(r, c):
        acc = tw_ref[0, r * TOP_K] * gbuf[slot, 0, r]
        for k in range(1, TOP_K):
            acc = acc + tw_ref[0, r * TOP_K + k] * gbuf[slot, k, r]
        ot[slot, r] = xt[slot, r] + g2 * acc
        return c

    lax.fori_loop(0, tb, wsum, 0, unroll=8)
    for c in ostores(s, slot):
        c.start()

    @pl.when(s == nsteps - 1)
    def _():
        for c in ostores(s, slot):
            c.wait()
        if nsteps >= 2:
            for c in ostores(s - 1, other):
                c.wait()


def _combine(dest4, tw4, gt2t, xn, ys, *, tb, seq):
    n, D = xn.shape
    nsteps = n // tb
    cur = lambda s: (s, 0, 0)
    nxt = lambda s: (jnp.minimum(s + 1, nsteps - 1), 0, 0)
    idx = lambda f: pl.BlockSpec((None, 1, tb * TOP_K), f, memory_space=pltpu.SMEM)
    anyspec = pl.BlockSpec(memory_space=pl.ANY)
    tile = (tb, ROW_TILE, LANES)
    return pl.pallas_call(
        functools.partial(_combine_kernel, tb=tb, nsteps=nsteps),
        out_shape=jax.ShapeDtypeStruct((n, D), F32),
        grid=(nsteps,),
        in_specs=[idx(cur), idx(nxt), idx(cur),
                  pl.BlockSpec((None, ROW_TILE, LANES), lambda s: ((s * tb) // seq, 0, 0)), anyspec, anyspec],
        out_specs=anyspec,
        scratch_shapes=[pltpu.VMEM((2, TOP_K) + tile, F32), pltpu.VMEM((2,) + tile, F32),
                        pltpu.VMEM((2,) + tile, F32), pltpu.SemaphoreType.DMA((2,)),
                        pltpu.SemaphoreType.DMA((2,)), pltpu.SemaphoreType.DMA((2,))],
        compiler_params=_cparams(("arbitrary",)),
        name="combine",
    )(dest4, dest4, tw4, gt2t, xn, ys)


def _rope_tables(T):
    rows = T // GRID_W
    row = jnp.repeat(jnp.arange(rows, dtype=F32), GRID_W)
    col = jnp.tile(jnp.arange(GRID_W, dtype=F32), rows)
    inv = ROPE_BASE ** (-jnp.arange(0, AXIS_ROT, 2, dtype=F32) / AXIS_ROT)
    ang_r, ang_c = row[:, None] * inv, col[:, None] * inv
    m = AXIS_ROT // 2
    ang = jnp.concatenate([ang_r, ang_r, ang_c, ang_c], axis=1)
    sign = jnp.tile(jnp.concatenate([-jnp.ones((m,), F32), jnp.ones((m,), F32)]), 2)
    cos = jnp.tile(jnp.cos(ang), (1, LANES // HEAD_DIM))
    sin = jnp.tile(jnp.sin(ang) * sign, (1, LANES // HEAD_DIM))
    return cos, sin


def _head_perm():
    order = []
    for m in range(ATT_GROUP):
        for kv in range(ATT_KV_HEADS):
            h = kv * ATT_GROUP + m
            order.extend(range(h * HEAD_DIM, (h + 1) * HEAD_DIM))
    return np.asarray(order)


def kernel(x, c, ctx, c_ctx, w_mod, b_mod, norm1, norm2, w_in, q_norm, k_norm, attn_sink,
           w_alpha_f, b_alpha_f, w_alpha_b, b_alpha_b, gla_norm, w_branch_attn, w_branch_gla,
           w_out, w_router, b_router, w_exp_in, b_exp_in, w_exp_out, b_exp_out):
    B, T, D = x.shape
    depth = w_mod.shape[0]
    assert depth == 1, "single-layer kernel: the context stream update only feeds later layers"
    l = 0
    perm = _head_perm()

    rows = ((B + 1 + 7) // 8) * 8
    c_all = jnp.zeros((rows, D), F32).at[:B].set(c).at[B].set(c_ctx)
    mod = _modulation(c_all, w_mod[l], b_mod[l])
    sh1, sc1, gt1, sh2, sc2, gt2 = [mod[:B, j * D:(j + 1) * D].reshape(B, 1, D) for j in range(6)]
    csh1, csc1 = [jnp.broadcast_to(mod[B, j * D:(j + 1) * D].reshape(1, 1, D), (B, 1, D)) for j in range(2)]

    offs = np.concatenate([[0], np.cumsum(IN_SPLITS)])
    cols = lambda j: w_in[l][:, offs[j]:offs[j + 1]]
    wal = jnp.zeros((2 * GLA_RANK, 2 * GLA_K_W), F32)
    wal = wal.at[:GLA_RANK, :GLA_K_W].set(w_alpha_f[l]).at[GLA_RANK:, GLA_K_W:].set(w_alpha_b[l])
    pw = {
        "wq": cols(0)[:, perm].astype(BF16), "wk": cols(1).astype(BF16), "wv": cols(2).astype(BF16),
        "wgq": cols(3).astype(BF16), "wgk": cols(4).astype(BF16), "wgv": cols(5).astype(BF16),
        "wgr": cols(6).astype(BF16), "wga": cols(9).astype(BF16), "wgg": cols(10).astype(BF16),
        "wlr": jnp.concatenate([cols(7), cols(8)], axis=1).astype(BF16),
        "qn": jnp.tile(q_norm[l], LANES // HEAD_DIM).reshape(1, LANES),
        "kn": jnp.tile(k_norm[l], LANES // HEAD_DIM).reshape(1, LANES),
        "wal": wal.astype(BF16),
        "bal": jnp.concatenate([b_alpha_f[l], b_alpha_b[l]]).reshape(1, 2 * GLA_K_W),
    }
    cos, sin = _rope_tables(T)
    n1 = norm1[l].reshape(1, D)
    tm = min(512, T)
    aq, ak, av, gq, gk, gv, gr, ga, gg, la = _inproj(
        x, sh1, sc1, n1, {"cos": cos, "sin": sin}, pw, rope=True, full=True, tm=tm)
    cak, cav, cgk, cgv, cla = _inproj(
        ctx, csh1, csc1, n1, None, pw, rope=False, full=False, tm=min(256, ctx.shape[1]))

    attn_o = _attention(attn_sink[l], aq, ak, av, cak, cav)
    gla_o = _gla(gq, gk, gv, la, cgk, cgv, cla)

    wr = jnp.zeros((D, LANES), F32).at[:, :N_EXPERTS].set(w_router[l])
    wrh = wr.astype(BF16)
    mw = {
        "gn": jnp.tile(gla_norm[l], GLA_HEADS).reshape(1, GLA_V_W),
        "wba": w_branch_attn[l][perm, :].astype(BF16), "wbg": w_branch_gla[l].astype(BF16),
        "wo": w_out[l].astype(BF16), "wrh": wrh, "wrl": (wr - wrh.astype(F32)).astype(BF16),
        "br": jnp.full((1, LANES), NEG, F32).at[0, :N_EXPERTS].set(b_router[l]),
    }
    xn, h2, ti, rk, tw, cnt = _merge(x, attn_o, gla_o, gr, ga, gg, gt1, sc2, sh2, norm2[l].reshape(1, D), mw, tm=tm)

    n = B * T
    counts = cnt[0, :N_EXPERTS]
    padded = (counts + MOE_STEP - 1) // MOE_STEP * MOE_STEP
    pad_end = jnp.cumsum(padded)
    pad_start = (pad_end - padded).astype(jnp.int32)
    zstart = jnp.maximum(pad_end - MOE_STEP, 0).astype(jnp.int32)
    cap = (n * TOP_K + N_EXPERTS * (MOE_STEP - 1)) // MOE_STEP * MOE_STEP
    n_blk = cap // MOE_STEP
    row0 = jnp.arange(n_blk, dtype=jnp.int32) * MOE_STEP
    blk_e = jnp.minimum(jnp.sum((pad_end[None, :] <= row0[:, None]).astype(jnp.int32), axis=1), N_EXPERTS - 1)
    onehot = (blk_e[:, None] == jnp.arange(N_EXPERTS, dtype=jnp.int32)[None, :]).astype(jnp.int32)
    valid_end = jnp.sum(onehot * (pad_start + counts)[None, :], axis=1)
    nv = jnp.clip(valid_end - row0, 0, MOE_STEP).astype(jnp.int32)
    n_used = (pad_end[-1] // MOE_STEP).astype(jnp.int32).reshape(1)
    changed = jnp.concatenate([jnp.ones((1,), bool), blk_e[1:] != blk_e[:-1]])
    first = (changed & (nv > 0)).astype(jnp.int32)
    parity = ((jnp.cumsum(first) - 1) % 2).astype(jnp.int32)
    eid = jnp.arange(N_EXPERTS, dtype=jnp.int32)
    later = (eid[None, :] > eid[:, None]) & (counts[None, :] > 0)
    nxt_of = jnp.where(jnp.any(later, axis=1), jnp.argmax(later, axis=1), -1).astype(jnp.int32)
    nxt_e = jnp.sum(onehot * nxt_of[None, :], axis=1).astype(jnp.int32)

    tb = min(256, T)
    dest = _dest(ti.reshape(n, LANES), rk.reshape(n, LANES), pad_start, tm=min(2048, n))
    dest4 = dest[:, :TOP_K].reshape(n // tb, 1, tb * TOP_K)
    tw4 = tw[:, :, :TOP_K].reshape(n // tb, 1, tb * TOP_K)
    xs = _dispatch(dest4, zstart, n_used, h2.reshape(n, D), cap, tb=tb)
    ys = _experts(blk_e, nv, first, nxt_e, parity, xs, w_exp_in[l], b_exp_in[l], w_exp_out[l], b_exp_out[l])
    out = _combine(dest4, tw4, gt2.reshape(B, ROW_TILE, LANES), xn.reshape(n, D), ys, tb=tb, seq=T)
    return out.reshape(B, T, D)
```

```python
import functools

import numpy as np
import jax
import jax.numpy as jnp
from jax import lax
from jax.experimental import pallas as pl
from jax.experimental.pallas import tpu as pltpu

F32 = jnp.float32
BF16 = jnp.bfloat16

D_MODEL = 1024
GRID_W = 64
EPS = 1e-6
ATT_HEADS = 8
ATT_KV_HEADS = 2
ATT_GROUP = ATT_HEADS // ATT_KV_HEADS
HEAD_DIM = 64
WINDOW = 128
ATT_BLOCK = 128
ROPE_BASE = 10000.0
AXIS_ROT = HEAD_DIM // 2
GLA_HEADS = 4
GLA_DK = 64
GLA_DV = 128
GLA_RANK = 16
GLA_TAU = 16.0
N_EXPERTS = 32
TOP_K = 4
D_FF = D_MODEL
SWIGLU_ALPHA = 1.702
SWIGLU_LIMIT = 7.0
MOE_STEP = 512

ATT_W = ATT_HEADS * HEAD_DIM
ATT_KV_W = ATT_KV_HEADS * HEAD_DIM
GLA_K_W = GLA_HEADS * GLA_DK
GLA_V_W = GLA_HEADS * GLA_DV
IN_SPLITS = (ATT_W, ATT_KV_W, ATT_KV_W, GLA_K_W, GLA_K_W, GLA_V_W, GLA_V_W, GLA_RANK, GLA_RANK, D_MODEL, D_MODEL)

LANES = 128
ROW_TILE = D_MODEL // LANES
VMEM_LIMIT = 56 * 1024 * 1024
NEG = -1e30

GLA_C = 64
GLA_SUB = 4
GLA_LEVELS = 4


def _cparams(sem):
    return pltpu.CompilerParams(dimension_semantics=sem, vmem_limit_bytes=VMEM_LIMIT)


def _full(shape):
    n = len(shape)
    return pl.BlockSpec(shape, lambda *_: (0,) * n)


def _mod_kernel(c_ref, w_ref, b_ref, o_ref):
    c = c_ref[...]
    s = c * (1.0 / (1.0 + jnp.exp(-c)))
    o_ref[...] = jnp.dot(s, w_ref[...], preferred_element_type=F32,
                         precision=lax.Precision.HIGHEST) + b_ref[...]


def _modulation(c_all, w_mod, b_mod):
    rows = c_all.shape[0]
    n = w_mod.shape[1]
    tn = 1536
    return pl.pallas_call(
        _mod_kernel,
        out_shape=jax.ShapeDtypeStruct((rows, n), F32),
        grid=(n // tn,),
        in_specs=[pl.BlockSpec((rows, D_MODEL), lambda j: (0, 0)),
                  pl.BlockSpec((D_MODEL, tn), lambda j: (0, j)),
                  pl.BlockSpec((1, tn), lambda j: (0, j))],
        out_specs=pl.BlockSpec((rows, tn), lambda j: (0, j)),
        compiler_params=_cparams(("arbitrary",)),
        name="mod",
    )(c_all, w_mod, b_mod.reshape(1, n))


def _pair_norm(a, g, lo):
    s = a * a
    tot = jnp.sum(s, axis=-1, keepdims=True)
    slo = jnp.sum(jnp.where(lo, s, 0.0), axis=-1, keepdims=True)
    ms = jnp.where(lo, slo, tot - slo) * (1.0 / HEAD_DIM)
    return a * lax.rsqrt(ms + EPS) * g


def _rope(y, cos, sin, first):
    up = pltpu.roll(y, LANES - AXIS_ROT // 2, 1)
    dn = pltpu.roll(y, AXIS_ROT // 2, 1)
    return y * cos + jnp.where(first, up, dn) * sin


def _inproj_kernel(*refs, rope, full):
    if full:
        (x_ref, sh_ref, sc_ref, n1_ref, cos_ref, sin_ref, qn_ref, kn_ref, wal_ref, bal_ref,
         wq, wk, wv, wgq, wgk, wgv, wgr, wga, wgg, wlr,
         oq, ok, ov, ogq, ogk, ogv, ogr, oga, ogg, ola) = refs
    else:
        (x_ref, sh_ref, sc_ref, n1_ref, kn_ref, wal_ref, bal_ref,
         wk, wv, wgk, wgv, wlr,
         ok, ov, ogk, ogv, ola) = refs
    x = x_ref[...]
    tm = x.shape[0]
    ms = jnp.mean(x * x, axis=-1, keepdims=True)
    h = (x * lax.rsqrt(ms + EPS) * n1_ref[...]) * (1.0 + sc_ref[...]) + sh_ref[...]
    hb = h.astype(BF16)

    def proj(w_ref):
        return jnp.dot(hb, w_ref[...], preferred_element_type=F32)

    lane = lax.broadcasted_iota(jnp.int32, (tm, LANES), 1)
    lo = lane < HEAD_DIM
    first = (lane % AXIS_ROT) < (AXIS_ROT // 2)
    if rope:
        cos = cos_ref[...]
        sin = sin_ref[...]

    k = _pair_norm(proj(wk), kn_ref[...], lo)
    if rope:
        k = _rope(k, cos, sin, first)
    ok[...] = k.astype(BF16)
    ov[...] = proj(wv).astype(BF16)
    ogk[...] = proj(wgk).astype(BF16)
    ogv[...] = proj(wgv).astype(BF16)
    lr = proj(wlr).astype(BF16)
    z = jnp.dot(lr, wal_ref[...], preferred_element_type=F32) + bal_ref[...]
    ola[...] = (jnp.minimum(z, 0.0) - jnp.log(1.0 + jnp.exp(-jnp.abs(z)))) * (1.0 / GLA_TAU)
    if full:
        q = proj(wq)
        for p in range(ATT_W // LANES):
            y = _pair_norm(q[:, p * LANES:(p + 1) * LANES], qn_ref[...], lo)
            if rope:
                y = _rope(y, cos, sin, first)
            oq[:, p * LANES:(p + 1) * LANES] = (y * HEAD_DIM ** -0.5).astype(BF16)
        ogq[...] = (proj(wgq) * GLA_DK ** -0.5).astype(BF16)
        g = proj(wgr)
        ogr[...] = (g * (1.0 / (1.0 + jnp.exp(-g)))).astype(BF16)
        oga[...] = (1.0 / (1.0 + jnp.exp(-proj(wga)))).astype(BF16)
        ogg[...] = (1.0 / (1.0 + jnp.exp(-proj(wgg)))).astype(BF16)


def _inproj(x, sh, sc, norm1, tabs, wts, *, rope, full, tm):
    B, T, D = x.shape
    grid = (B, T // tm)
    row = lambda w: pl.BlockSpec((None, tm, w), lambda b, t: (b, t, 0))
    vec = pl.BlockSpec((None, 1, D), lambda b, t: (b, 0, 0))
    tab = pl.BlockSpec((tm, LANES), lambda b, t: (t, 0))
    if full:
        names = ("wq", "wk", "wv", "wgq", "wgk", "wgv", "wgr", "wga", "wgg", "wlr")
        ins = [x, sh, sc, norm1, tabs["cos"], tabs["sin"], wts["qn"], wts["kn"], wts["wal"], wts["bal"]]
        specs = [row(D), vec, vec, _full((1, D)), tab, tab, _full((1, LANES)), _full((1, LANES)),
                 _full(wts["wal"].shape), _full(wts["bal"].shape)]
        out_w = (ATT_W, ATT_KV_W, ATT_KV_W, GLA_K_W, GLA_K_W, GLA_V_W, GLA_V_W, D, D)
    else:
        names = ("wk", "wv", "wgk", "wgv", "wlr")
        ins = [x, sh, sc, norm1, wts["kn"], wts["wal"], wts["bal"]]
        specs = [row(D), vec, vec, _full((1, D)), _full((1, LANES)),
                 _full(wts["wal"].shape), _full(wts["bal"].shape)]
        out_w = (ATT_KV_W, ATT_KV_W, GLA_K_W, GLA_V_W)
    ins += [wts[n] for n in names]
    specs += [_full(wts[n].shape) for n in names]
    out_shape = [jax.ShapeDtypeStruct((B, T, w), BF16) for w in out_w]
    out_shape.append(jax.ShapeDtypeStruct((B, T, 2 * GLA_K_W), F32))
    out_specs = [row(w) for w in out_w] + [row(2 * GLA_K_W)]
    return pl.pallas_call(
        functools.partial(_inproj_kernel, rope=rope, full=full),
        out_shape=out_shape, grid=grid, in_specs=specs, out_specs=out_specs,
        compiler_params=_cparams(("parallel", "arbitrary")),
        name="inproj_full" if full else "inproj_ctx",
    )(*ins)


def _attn_kernel(*refs, seq, nsb):
    sink_ref, band_ref, q_ref = refs[:3]
    kblocks = refs[3:nsb + 5]
    kx_ref = refs[nsb + 5]
    vblocks = refs[nsb + 6:2 * nsb + 8]
    vx_ref, o_ref = refs[2 * nsb + 8:]
    n = pl.program_id(1)
    blk = ATT_BLOCK
    nb = seq // blk
    nslab = ATT_W // LANES
    rows = nslab * blk
    lane = lax.broadcasted_iota(jnp.int32, (blk, LANES), 1)
    lo = lane < HEAD_DIM
    hrow = lax.broadcasted_iota(jnp.int32, (rows, 1), 0) // blk
    band = band_ref[...]
    for sb in range(nsb):
        kcat = jnp.concatenate([r[...] for r in kblocks[sb:sb + 3]] + [kx_ref[...]], axis=0)
        vcat = jnp.concatenate([r[...] for r in vblocks[sb:sb + 3]] + [vx_ref[...]], axis=0)
        first = nsb * n + sb - 1
        q = q_ref[sb * blk:(sb + 1) * blk, :]
        outs = []
        for kv in range(ATT_KV_HEADS):
            keep = lo if kv == 0 else jnp.logical_not(lo)
            qs = jnp.concatenate([jnp.where(keep, q[:, m * LANES:(m + 1) * LANES], jnp.zeros((blk, LANES), BF16))
                                  for m in range(nslab)], axis=0)
            s = lax.dot_general(qs, kcat, (((1,), (1,)), ((), ())), preferred_element_type=F32) + band
            s = jnp.concatenate([jnp.where(first >= 0, s[:, :blk], NEG), s[:, blk:2 * blk],
                                 jnp.where(first + 2 < nb, s[:, 2 * blk:3 * blk], NEG), s[:, 3 * blk:]], axis=1)
            snk = jnp.zeros((rows, 1), F32)
            for m in range(nslab):
                snk = jnp.where(hrow == m, sink_ref[kv * ATT_GROUP + m], snk)
            mx = jnp.maximum(jnp.max(s, axis=-1, keepdims=True), snk)
            p = jnp.exp(s - mx)
            den = jnp.sum(p, axis=-1, keepdims=True) + jnp.exp(snk - mx)
            outs.append(jnp.dot(p.astype(BF16), vcat, preferred_element_type=F32) / den)
        for m in range(nslab):
            o_ref[sb * blk:(sb + 1) * blk, m * LANES:(m + 1) * LANES] = jnp.where(
                lo, outs[0][m * blk:(m + 1) * blk], outs[1][m * blk:(m + 1) * blk]).astype(BF16)


def _attention(sink, aq, ak, av, cak, cav):
    B, T, _ = aq.shape
    lc = cak.shape[1]
    blk = ATT_BLOCK
    nb = T // blk
    nsb = 4 if nb % 4 == 0 else 2
    assert nb % nsb == 0
    kvspec = lambda off: pl.BlockSpec((None, blk, ATT_KV_W),
                                      lambda b, n: (b, jnp.clip(nsb * n + off, 0, nb - 1), 0))
    cspec = pl.BlockSpec((None, lc, ATT_KV_W), lambda b, n: (b, 0, 0))
    qspec = pl.BlockSpec((None, nsb * blk, ATT_W), lambda b, n: (b, n, 0))
    kvs = [kvspec(off) for off in range(-1, nsb + 1)]
    rows = (ATT_W // LANES) * blk
    qi = np.arange(rows)[:, None] % blk
    kj = np.arange(3 * blk + lc)[None, :]
    band = jnp.asarray(np.where((np.abs(kj - blk - qi) <= WINDOW) | (kj >= 3 * blk), 0.0, NEG), F32)
    return pl.pallas_call(
        functools.partial(_attn_kernel, seq=T, nsb=nsb),
        out_shape=jax.ShapeDtypeStruct((B, T, ATT_W), BF16),
        grid=(B, nb // nsb),
        in_specs=[pl.BlockSpec(memory_space=pltpu.SMEM), _full(band.shape), qspec] + kvs + [cspec] + kvs + [cspec],
        out_specs=qspec,
        compiler_params=_cparams(("parallel", "arbitrary")),
        name="attn",
    )(sink, band, aq, *([ak] * (nsb + 2)), cak, *([av] * (nsb + 2)), cav)


def _gla_constants():
    C, sub, L = GLA_C, GLA_SUB, GLA_LEVELS
    i = np.arange(C)[:, None]
    t = np.arange(C)[None, :]
    tabs = [t <= i]
    rowq, same = [], []
    for l in range(L):
        s = C >> l
        mid = (i // s) * s + s // 2
        rowq.append(np.broadcast_to(i >= mid, (C, C)))
        same.append((i // s) == (t // s))
    shifts, dmask, dvalid = [], [t == i], []
    for d in range(1, sub):
        ok = (i % sub) >= d
        shifts.append(ok & (t == i - d))
        dmask.append(ok & (t == i - d))
        dvalid.append(np.broadcast_to(ok, (C, C)))
    flip = lambda a: a[::-1, ::-1]
    tile = lambda a: np.tile(a, (1, GLA_HEADS))

    def both(xs, lanes):
        f = (lambda a: tile(a)) if lanes else (lambda a: a)
        return np.stack([np.concatenate([f(a) for a in xs], 0),
                         np.concatenate([f(flip(a)) for a in xs], 0)]).astype(np.float32)

    hk = np.arange(GLA_K_W) // GLA_DK
    hv = np.arange(GLA_V_W) // GLA_DV
    ind = (hk[:, None] == hk[None, :]).astype(np.float32)
    bdv = (hk[:, None] == hv[None, :]).astype(np.float32)
    return (both(tabs, False), both(shifts, False), both(rowq, True), both(same, True),
            both(dmask, True), ind, bdv, np.ascontiguousarray(bdv.T), both(dvalid, True))


def _gla_chunk(q_b, k_b, v_b, la, cst, d):
    tri_ref, shm_ref, lv_ref, sm_ref, dm_ref, ind_ref, bdv_ref, hm_ref, dv_ref = cst
    C = GLA_C
    kw = GLA_K_W
    q = q_b.astype(F32)
    k = k_b.astype(F32)
    hi = la.astype(BF16)
    r1 = la - hi.astype(F32)
    mid = r1.astype(BF16)
    lo = (r1 - mid.astype(F32)).astype(BF16)
    b3 = jnp.dot(tri_ref[d], jnp.concatenate([hi, mid, lo], axis=1), preferred_element_type=F32)
    b = b3[:, :kw] + b3[:, kw:2 * kw] + b3[:, 2 * kw:]
    last = b[C - 1:C] if d == 0 else b[0:1]

    qt = (q * jnp.exp(b)).astype(BF16)
    kt = (k * jnp.exp(last - b)).astype(BF16)
    gamma = jnp.exp(last)

    ind = ind_ref[...]
    a = None
    for l in range(GLA_LEVELS):
        s = C >> l
        off = s // 2 - 1 if d == 0 else s // 2
        bref = jnp.concatenate([jnp.broadcast_to(b[st + off:st + off + 1], (s, kw)) for st in range(0, C, s)], axis=0)
        rq = lv_ref[d, l * C:(l + 1) * C, :]
        el = jnp.exp((b - bref) * (2.0 * rq - 1.0))
        qh = (q * (el * rq)).astype(BF16)
        kh = (k * (el * (1.0 - rq))).astype(BF16)
        bdk = jnp.concatenate([kh] * GLA_HEADS, axis=0) * ind
        al = lax.dot_general(qh, bdk, (((1,), (1,)), ((), ())), preferred_element_type=F32)
        a = al if l == 0 else a + al * sm_ref[d, l * C:(l + 1) * C, :]
    ksh = jnp.dot(shm_ref[d], k_b, preferred_element_type=F32)
    ps = [q * k]
    for j in range(1, GLA_SUB):
        bsh = pltpu.roll(b, j if d == 0 else C - j, 0)
        ej = jnp.exp((b - bsh) * dv_ref[d, (j - 1) * C:j * C, :])
        ps.append(q * ksh[(j - 1) * C:j * C] * ej)
    w = jnp.dot(jnp.concatenate(ps, axis=0).astype(BF16), ind, preferred_element_type=F32)
    for j in range(GLA_SUB):
        a = a + w[j * C:(j + 1) * C] * dm_ref[d, j * C:(j + 1) * C, :]

    bdv = jnp.concatenate([v_b] * GLA_HEADS, axis=0) * bdv_ref[...]
    o_intra = jnp.dot(a.astype(BF16), bdv, preferred_element_type=F32)
    upd = lax.dot_general(v_b, kt, (((0,), (0,)), ((), ())), preferred_element_type=F32) * hm_ref[...]
    return o_intra, qt, upd, gamma


def _gla_kernel(qf_ref, kf_ref, vf_ref, laf_ref, qb_ref, kb_ref, vb_ref, lab_ref,
                ckf_ref, cvf_ref, claf_ref, ckb_ref, cvb_ref, clab_ref,
                tri_ref, shm_ref, lv_ref, sm_ref, dm_ref, ind_ref, bdv_ref, hm_ref, dv_ref,
                of_ref, ob_ref, st_ref, *, n_ctx_steps):
    s = pl.program_id(1)
    C = GLA_C
    cst = (tri_ref, shm_ref, lv_ref, sm_ref, dm_ref, ind_ref, bdv_ref, hm_ref, dv_ref)

    @pl.when(s == 0)
    def _():
        st_ref[...] = jnp.zeros_like(st_ref)

    is_ctx = s < n_ctx_steps
    dirs = ((0, qf_ref, kf_ref, vf_ref, laf_ref, ckf_ref, cvf_ref, claf_ref, of_ref),
            (1, qb_ref, kb_ref, vb_ref, lab_ref, ckb_ref, cvb_ref, clab_ref, ob_ref))
    states = [st_ref[0], st_ref[1]]
    for idx in range(2):
        for d, q_ref, k_ref, v_ref, la_ref, ck_ref, cv_ref, cla_ref, o_ref in dirs:
            c = idx if d == 0 else 1 - idx
            rows = slice(c * C, (c + 1) * C)
            k_b = jnp.where(is_ctx, ck_ref[rows, :], k_ref[rows, :])
            v_b = jnp.where(is_ctx, cv_ref[rows, :], v_ref[rows, :])
            la = jnp.where(is_ctx, cla_ref[rows, :], la_ref[rows, :])
            o_intra, qt, upd, gamma = _gla_chunk(q_ref[rows, :], k_b, v_b, la, cst, d)
            st = states[d]
            o_ref[rows, :] = o_intra + lax.dot_general(qt, st.astype(BF16), (((1,), (1,)), ((), ())),
                                                       preferred_element_type=F32)
            states[d] = st * gamma + upd
    st_ref[0] = states[0]
    st_ref[1] = states[1]


def _gla(gq, gk, gv, la, cgk, cgv, cla):
    B, T, _ = gq.shape
    lc = cgk.shape[1]
    R = 2 * GLA_C
    assert lc % R == 0 and T % R == 0
    n_ctx, n_lat = lc // R, T // R
    consts = _gla_constants()
    tri, shm = jnp.asarray(consts[0], BF16), jnp.asarray(consts[1], BF16)
    lv, sm, dm = [jnp.asarray(c) for c in consts[2:5]]
    ind, bdv = jnp.asarray(consts[5], BF16), jnp.asarray(consts[6], BF16)
    hm, dv = jnp.asarray(consts[7]), jnp.asarray(consts[8])

    def lat(s, d):
        j = jnp.maximum(s - n_ctx, 0)
        return j if d == 0 else n_lat - 1 - j

    def ctx(s, d):
        j = jnp.minimum(s, n_ctx - 1)
        return j if d == 0 else n_ctx - 1 - j

    lspec = lambda w, d, c=0: pl.BlockSpec((None, R, w), lambda b, s: (b, lat(s, d), c))
    cspec = lambda w, d, c=0: pl.BlockSpec((None, R, w), lambda b, s: (b, ctx(s, d), c))
    lat_specs = lambda d: [lspec(GLA_K_W, d), lspec(GLA_K_W, d), lspec(GLA_V_W, d), lspec(GLA_K_W, d, d)]
    ctx_specs = lambda d: [cspec(GLA_K_W, d), cspec(GLA_V_W, d), cspec(GLA_K_W, d, d)]
    cs = [tri, shm, lv, sm, dm, ind, bdv, hm, dv]
    return pl.pallas_call(
        functools.partial(_gla_kernel, n_ctx_steps=n_ctx),
        out_shape=[jax.ShapeDtypeStruct((B, T, GLA_V_W), F32)] * 2,
        grid=(B, n_ctx + n_lat),
        in_specs=lat_specs(0) + lat_specs(1) + ctx_specs(0) + ctx_specs(1) + [_full(c.shape) for c in cs],
        out_specs=[lspec(GLA_V_W, 0), lspec(GLA_V_W, 1)],
        scratch_shapes=[pltpu.VMEM((2, GLA_V_W, GLA_K_W), F32)],
        compiler_params=_cparams(("parallel", "arbitrary")),
        name="gla",
    )(gq, gk, gv, la, gq, gk, gv, la, cgk, cgv, cla, cgk, cgv, cla, *cs)


def _merge_kernel(x_ref, at_ref, of_ref, ob_ref, gr_ref, ga_ref, gg_ref, gt1_ref, sc2_ref, sh2_ref,
                  n2_ref, gn_ref, ltri_ref, wba_ref, wbg_ref, wo_ref, wrh_ref, wrl_ref, br_ref,
                  xn_ref, h2_ref, ti_ref, rk_ref, tw_ref, cnt_ref, carry_ref):
    tm = x_ref.shape[0]

    @pl.when((pl.program_id(0) == 0) & (pl.program_id(1) == 0))
    def _():
        carry_ref[...] = jnp.zeros_like(carry_ref)

    go = of_ref[...] + ob_ref[...]
    parts = []
    for h in range(GLA_HEADS):
        gh = go[:, h * GLA_DV:(h + 1) * GLA_DV]
        ms = jnp.mean(gh * gh, axis=-1, keepdims=True)
        parts.append(gh * lax.rsqrt(ms + EPS))
    o = jnp.concatenate(parts, axis=1) * gn_ref[...] * gr_ref[...].astype(F32)
    ya = jnp.dot(at_ref[...], wba_ref[...], preferred_element_type=F32)
    yg = jnp.dot(o.astype(BF16), wbg_ref[...], preferred_element_type=F32)
    y = ga_ref[...].astype(F32) * ya + gg_ref[...].astype(F32) * yg
    z = jnp.dot(y.astype(BF16), wo_ref[...], preferred_element_type=F32)
    xn = x_ref[...] + gt1_ref[...] * z
    xn_ref[...] = xn
    ms = jnp.mean(xn * xn, axis=-1, keepdims=True)
    h2 = (xn * lax.rsqrt(ms + EPS) * n2_ref[...]) * (1.0 + sc2_ref[...]) + sh2_ref[...]
    hh = h2.astype(BF16)
    hl = (h2 - hh.astype(F32)).astype(BF16)
    h2_ref[...] = h2
    logits = (jnp.dot(hh, wrh_ref[...], preferred_element_type=F32)
              + jnp.dot(hl, wrh_ref[...], preferred_element_type=F32)
              + jnp.dot(hh, wrl_ref[...], preferred_element_type=F32)) + br_ref[...]
    lane = lax.broadcasted_iota(jnp.int32, (tm, LANES), 1).astype(F32)
    vals, idxs = [], []
    l = logits
    for _ in range(TOP_K):
        m = jnp.max(l, axis=-1, keepdims=True)
        ix = jnp.min(jnp.where(l == m, lane, float(LANES)), axis=-1, keepdims=True)
        vals.append(m)
        idxs.append(ix)
        l = jnp.where(lane == ix, -3.0e38, l)
    ex = [jnp.exp(v - vals[0]) for v in vals]
    den = ex[0] + ex[1] + ex[2] + ex[3]
    mh = jnp.zeros((tm, LANES), F32)
    for j in range(TOP_K):
        mh = mh + jnp.where(lane == idxs[j], 1.0, 0.0)
    pc = jnp.dot(ltri_ref[...], mh.astype(BF16), preferred_element_type=F32) + carry_ref[...]
    ti = jnp.zeros((tm, LANES), F32)
    rk = jnp.zeros((tm, LANES), F32)
    tw = jnp.zeros((tm, LANES), F32)
    for j in range(TOP_K):
        rj = jnp.sum(jnp.where(lane == idxs[j], pc, 0.0), axis=-1, keepdims=True)
        ti = jnp.where(lane == float(j), idxs[j], ti)
        rk = jnp.where(lane == float(j), rj, rk)
        tw = jnp.where(lane == float(j), ex[j] / den, tw)
    ti_ref[...] = ti.astype(jnp.int32)
    rk_ref[...] = rk.astype(jnp.int32)
    tw_ref[...] = tw
    total = carry_ref[...] + jnp.sum(mh, axis=0, keepdims=True)
    carry_ref[...] = total
    cnt_ref[...] = total.astype(jnp.int32)


def _merge(x, attn_o, gla_o, gr, ga, gg, gt1, sc2, sh2, norm2, wts, *, tm):
    B, T, D = x.shape
    row = lambda w: pl.BlockSpec((None, tm, w), lambda b, t: (b, t, 0))
    vec = pl.BlockSpec((None, 1, D), lambda b, t: (b, 0, 0))
    names = ("wba", "wbg", "wo", "wrh", "wrl", "br")
    ltri = jnp.asarray(np.tril(np.ones((tm, tm), np.float32), -1), BF16)
    return pl.pallas_call(
        _merge_kernel,
        out_shape=[jax.ShapeDtypeStruct((B, T, D), F32), jax.ShapeDtypeStruct((B, T, D), F32),
                   jax.ShapeDtypeStruct((B, T, LANES), jnp.int32), jax.ShapeDtypeStruct((B, T, LANES), jnp.int32),
                   jax.ShapeDtypeStruct((B, T, LANES), F32), jax.ShapeDtypeStruct((1, LANES), jnp.int32)],
        grid=(B, T // tm),
        in_specs=[row(D), row(ATT_W), row(GLA_V_W), row(GLA_V_W),
                  row(GLA_V_W), row(D), row(D), vec, vec, vec,
                  _full((1, D)), _full((1, GLA_V_W)), _full((tm, tm))] + [_full(wts[n].shape) for n in names],
        out_specs=[row(D), row(D), row(LANES), row(LANES), row(LANES), _full((1, LANES))],
        scratch_shapes=[pltpu.VMEM((1, LANES), F32)],
        compiler_params=_cparams(("arbitrary", "arbitrary")),
        name="merge",
    )(x, attn_o, gla_o[0], gla_o[1], gr, ga, gg, gt1, sc2, sh2, norm2, wts["gn"], ltri, *[wts[n] for n in names])


def _dest_kernel(ti_ref, rk_ref, tw_ref, ps_ref, o_ref, w_ref):
    tm = ti_ref.shape[0]
    lane = lax.broadcasted_iota(jnp.int32, (tm, LANES), 1)
    ti = ti_ref[...]
    ps = ps_ref[...].astype(F32)
    out = jnp.where(lane < TOP_K, rk_ref[...], 0).astype(F32)
    for k in range(TOP_K):
        start = jnp.sum(jnp.where(lane == ti[:, k:k + 1], ps, 0.0), axis=-1, keepdims=True)
        out = out + jnp.where(lane == k, start, 0.0)
    o_ref[...] = jnp.transpose(out)[0:8, :].astype(jnp.int32)
    w_ref[...] = jnp.transpose(tw_ref[...])[0:8, :]


def _dest(ti, rk, tw, pad_start, *, tm):
    n = ti.shape[0]
    ps = jnp.zeros((1, LANES), jnp.int32).at[0, :N_EXPERTS].set(pad_start)
    row = pl.BlockSpec((tm, LANES), lambda i: (i, 0))
    col = pl.BlockSpec((8, tm), lambda i: (0, i))
    return pl.pallas_call(
        _dest_kernel,
        out_shape=[jax.ShapeDtypeStruct((8, n), jnp.int32), jax.ShapeDtypeStruct((8, n), F32)],
        grid=(n // tm,), in_specs=[row, row, row, _full((1, LANES))], out_specs=[col, col],
        compiler_params=_cparams(("arbitrary",)), name="dest",
    )(ti, rk, tw, ps)


def _dispatch_kernel(d_ref, zs_ref, nu_ref, h_ref, xs_ref, buf, zbuf, isem, sem, zsem,
                     *, tb, nsteps):
    s = pl.program_id(0)
    nblk = xs_ref.shape[0] // MOE_STEP
    slot = s % 3
    nxt = (s + 1) % 3

    def loads(step, sl):
        r0 = pl.multiple_of(step * tb, tb)
        return [pltpu.make_async_copy(h_ref.at[pl.ds(r0, tb), pl.ds(j * LANES, LANES)], buf.at[sl, :, j, :],
                                      isem.at[sl]) for j in range(ROW_TILE)]

    def wait_rows(sl):
        for _ in range(TOP_K):
            pltpu.make_async_copy(buf.at[sl], xs_ref.at[pl.ds(0, tb)], sem.at[sl]).wait()

    @pl.when(s == 0)
    def _():
        zbuf[...] = jnp.zeros_like(zbuf)

        def zstart(e, c):
            z0 = pl.multiple_of(zs_ref[e], MOE_STEP)
            pltpu.make_async_copy(zbuf, xs_ref.at[pl.ds(z0, MOE_STEP)], zsem).start()
            return c

        def zwait(e, c):
            pltpu.make_async_copy(zbuf, xs_ref.at[pl.ds(0, MOE_STEP)], zsem).wait()
            return c

        lax.fori_loop(0, N_EXPERTS, zstart, 0)
        lax.fori_loop(0, N_EXPERTS, zwait, 0)

        def tstart(j, c):
            pltpu.make_async_copy(zbuf, xs_ref.at[pl.ds(pl.multiple_of(j * MOE_STEP, MOE_STEP), MOE_STEP)],
                                  zsem).start()
            return c

        lax.fori_loop(nu_ref[0], nblk, tstart, 0)
        lax.fori_loop(nu_ref[0], nblk, zwait, 0)
        for c in loads(0, 0):
            c.start()

    for c in loads(s, slot):
        c.wait()

    @pl.when(s + 1 < nsteps)
    def _():
        @pl.when(s >= 2)
        def _():
            wait_rows(nxt)
        for c in loads(s + 1, nxt):
            c.start()

    def issue(r, c):
        for k in range(TOP_K):
            d = d_ref[0, r * TOP_K + k]
            pltpu.make_async_copy(buf.at[slot, r], xs_ref.at[d], sem.at[slot]).start(priority=k % 2)
        return c

    lax.fori_loop(0, tb, issue, 0, unroll=8)

    @pl.when(s == nsteps - 1)
    def _():
        wait_rows(slot)
        if nsteps >= 2:
            wait_rows((s + 2) % 3)
        if nsteps >= 3:
            wait_rows(nxt)


def _dispatch(dest4, zstart, n_used, h2, cap, *, tb):
    n, D = h2.shape
    nsteps = n // tb
    idx = pl.BlockSpec((None, 1, tb * TOP_K), lambda s: (s, 0, 0), memory_space=pltpu.SMEM)
    smem = pl.BlockSpec(memory_space=pltpu.SMEM)
    anyspec = pl.BlockSpec(memory_space=pl.ANY)
    return pl.pallas_call(
        functools.partial(_dispatch_kernel, tb=tb, nsteps=nsteps),
        out_shape=jax.ShapeDtypeStruct((cap, ROW_TILE, LANES), F32),
        grid=(nsteps,),
        in_specs=[idx, smem, smem, anyspec],
        out_specs=anyspec,
        scratch_shapes=[pltpu.VMEM((3, tb, ROW_TILE, LANES), F32), pltpu.VMEM((MOE_STEP, ROW_TILE, LANES), F32),
                        pltpu.SemaphoreType.DMA((3,)), pltpu.SemaphoreType.DMA((3,)), pltpu.SemaphoreType.DMA(())],
        compiler_params=_cparams(("arbitrary",)),
        name="dispatch",
    )(dest4, zstart, n_used, h2)


def _expert_kernel(be_ref, nv_ref, fs_ref, nx_ref, pr_ref, xs_ref, w1_ref, b1_ref, w2_ref, b2_ref, ys_ref,
                   w1b, w2b, w1f, w2f, xin, yout, isem, osem, wsem, *, nsteps):
    i = pl.program_id(0)
    slot = i % 2

    def wloads(ex, sl):
        return [pltpu.make_async_copy(w1_ref.at[ex], w1f.at[sl], wsem.at[sl]),
                pltpu.make_async_copy(w2_ref.at[ex], w2f.at[sl], wsem.at[sl])]

    def loads(step, sl):
        r0 = pl.multiple_of(step * MOE_STEP, MOE_STEP)
        return [pltpu.make_async_copy(xs_ref.at[pl.ds(r0, MOE_STEP), j, :],
                                      xin.at[sl, :, pl.ds(j * LANES, LANES)], isem.at[sl]) for j in range(ROW_TILE)]

    def stores(step, sl):
        r0 = pl.multiple_of(step * MOE_STEP, MOE_STEP)
        return [pltpu.make_async_copy(yout.at[sl, :, pl.ds(j * LANES, LANES)],
                                      ys_ref.at[pl.ds(r0, MOE_STEP), j, :], osem.at[sl]) for j in range(ROW_TILE)]

    @pl.when(i == 0)
    def _():
        for c in loads(0, 0):
            c.start()
        for c in wloads(be_ref[0], 0):
            c.start()

    @pl.when(fs_ref[i] == 1)
    def _():
        par = pr_ref[i]
        for c in wloads(be_ref[i], par):
            c.wait()
        w1b[...] = w1f[par].astype(BF16)
        w2b[...] = w2f[par].astype(BF16)

        @pl.when(nx_ref[i] >= 0)
        def _():
            for c in wloads(nx_ref[i], 1 - par):
                c.start(priority=1)

    for c in loads(i, slot):
        c.wait()

    @pl.when(i + 1 < nsteps)
    def _():
        for c in loads(i + 1, 1 - slot):
            c.start()

    @pl.when(i >= 2)
    def _():
        for c in stores(i - 2, slot):
            c.wait()

    def mlp(rows):
        xb = xin[slot, 0:rows, :].astype(BF16)
        y = jnp.zeros((rows, D_MODEL), F32)
        fh = D_FF // 2
        for h in range(2):
            g = jnp.dot(xb, w1b[:, h * fh:(h + 1) * fh], preferred_element_type=F32) + b1_ref[:, h * fh:(h + 1) * fh]
            u = (jnp.dot(xb, w1b[:, D_FF + h * fh:D_FF + (h + 1) * fh], preferred_element_type=F32)
                 + b1_ref[:, D_FF + h * fh:D_FF + (h + 1) * fh])
            gate = jnp.minimum(g, SWIGLU_LIMIT)
            up = jnp.clip(u, -SWIGLU_LIMIT, SWIGLU_LIMIT)
            act = gate * (1.0 / (1.0 + jnp.exp(-SWIGLU_ALPHA * gate))) * (up + 1.0)
            y = y + jnp.dot(act.astype(BF16), w2b[h * fh:(h + 1) * fh, :], preferred_element_type=F32)
        yout[slot, 0:rows, :] = y + b2_ref[...]

    nv = nv_ref[i]
    quarter = MOE_STEP // 4
    for j in range(1, 5):
        rows = j * quarter

        @pl.when((nv > rows - quarter) & (nv <= rows))
        def _():
            mlp(rows)
            if rows < MOE_STEP:
                yout[slot, rows:, :] = jnp.zeros((MOE_STEP - rows, D_MODEL), F32)

    @pl.when(nv == 0)
    def _():
        yout[slot] = jnp.zeros((MOE_STEP, D_MODEL), F32)

    for c in stores(i, slot):
        c.start()

    @pl.when(i == nsteps - 1)
    def _():
        for c in stores(i, slot):
            c.wait()
        if nsteps >= 2:
            for c in stores(i - 1, 1 - slot):
                c.wait()


def _experts(blk_e, nv, first, nxt_e, parity, xs, w1, b1, w2, b2):
    cap = xs.shape[0]
    n_blk = cap // MOE_STEP
    ne = w1.shape[0]
    anyspec = pl.BlockSpec(memory_space=pl.ANY)
    bias = lambda w: pl.BlockSpec((None, 1, w), lambda i, be, *_: (be[i], 0, 0))
    gs = pltpu.PrefetchScalarGridSpec(
        num_scalar_prefetch=5, grid=(n_blk,),
        in_specs=[anyspec, anyspec, bias(2 * D_FF), anyspec, bias(D_MODEL)],
        out_specs=anyspec,
        scratch_shapes=[pltpu.VMEM((D_MODEL, 2 * D_FF), BF16), pltpu.VMEM((D_FF, D_MODEL), BF16),
                        pltpu.VMEM((2, D_MODEL, 2 * D_FF), F32), pltpu.VMEM((2, D_FF, D_MODEL), F32),
                        pltpu.VMEM((2, MOE_STEP, D_MODEL), F32), pltpu.VMEM((2, MOE_STEP, D_MODEL), F32),
                        pltpu.SemaphoreType.DMA((2,)), pltpu.SemaphoreType.DMA((2,)),
                        pltpu.SemaphoreType.DMA((2,))])
    return pl.pallas_call(
        functools.partial(_expert_kernel, nsteps=n_blk), grid_spec=gs,
        out_shape=jax.ShapeDtypeStruct((cap, ROW_TILE, LANES), F32),
        compiler_params=_cparams(("arbitrary",)),
        name="experts",
    )(blk_e, nv, first, nxt_e, parity, xs, w1, b1.reshape(ne, 1, 2 * D_FF), w2, b2.reshape(ne, 1, D_MODEL))


def _combine_kernel(dc_ref, dn_ref, tw_ref, gt2_ref, xn_ref, ys_ref, o_ref,
                    gbuf, xt, ot, gsem, xsem, osem, *, tb, nsteps):
    s = pl.program_id(0)
    slot = s % 2
    other = 1 - slot

    def xloads(step, sl):
        r0 = pl.multiple_of(step * tb, tb)
        return [pltpu.make_async_copy(xn_ref.at[pl.ds(r0, tb), pl.ds(j * LANES, LANES)], xt.at[sl, :, j, :],
                                      xsem.at[sl]) for j in range(ROW_TILE)]

    def ostores(step, sl):
        r0 = pl.multiple_of(step * tb, tb)
        return [pltpu.make_async_copy(ot.at[sl, :, j, :], o_ref.at[pl.ds(r0, tb), pl.ds(j * LANES, LANES)],
                                      osem.at[sl]) for j in range(ROW_TILE)]

    def gather_row(d_ref, sl, r):
        for k in range(TOP_K):
            d = d_ref[0, r * TOP_K + k]
            pltpu.make_async_copy(ys_ref.at[d], gbuf.at[sl, k, r], gsem.at[sl]).start(priority=k % 2)

    def wait_gathers(sl):
        for k in range(TOP_K):
            pltpu.make_async_copy(ys_ref.at[pl.ds(0, tb)], gbuf.at[sl, k], gsem.at[sl]).wait()

    @pl.when(s == 0)
    def _():
        def issue(r, c):
            gather_row(dc_ref, 0, r)
            return c

        lax.fori_loop(0, tb, issue, 0, unroll=8)
        for c in xloads(0, 0):
            c.start()

    wait_gathers(slot)
    for c in xloads(s, slot):
        c.wait()

    @pl.when(s >= 2)
    def _():
        for c in ostores(s - 2, slot):
            c.wait()

    @pl.when(s + 1 < nsteps)
    def _():
        def issue(r, c):
            gather_row(dn_ref, other, r)
            return c

        lax.fori_loop(0, tb, issue, 0, unroll=8)
        for c in xloads(s + 1, other):
            c.start()

    g2 = gt2_ref[...]

    def wsum(r, c):
        acc = tw_ref[0, r * TOP_K] * gbuf[slot, 0, r]
        for k in range(1, TOP_K):
            acc = acc + tw_ref[0, r * TOP_K + k] * gbuf[slot, k, r]
        ot[slot, r] = xt[slot, r] + g2 * acc
        return c

    lax.fori_loop(0, tb, wsum, 0, unroll=8)
    for c in ostores(s, slot):
        c.start()

    @pl.when(s == nsteps - 1)
    def _():
        for c in ostores(s, slot):
            c.wait()
        if nsteps >= 2:
            for c in ostores(s - 1, other):
                c.wait()


def _combine(dest4, tw4, gt2t, xn, ys, *, tb, seq):
    n, D = xn.shape
    nsteps = n // tb
    cur = lambda s: (s, 0, 0)
    nxt = lambda s: (jnp.minimum(s + 1, nsteps - 1), 0, 0)
    idx = lambda f: pl.BlockSpec((None, 1, tb * TOP_K), f, memory_space=pltpu.SMEM)
    anyspec = pl.BlockSpec(memory_space=pl.ANY)
    tile = (tb, ROW_TILE, LANES)
    return pl.pallas_call(
        functools.partial(_combine_kernel, tb=tb, nsteps=nsteps),
        out_shape=jax.ShapeDtypeStruct((n, D), F32),
        grid=(nsteps,),
        in_specs=[idx(cur), idx(nxt), idx(cur),
                  pl.BlockSpec((None, ROW_TILE, LANES), lambda s: ((s * tb) // seq, 0, 0)), anyspec, anyspec],
        out_specs=anyspec,
        scratch_shapes=[pltpu.VMEM((2, TOP_K) + tile, F32), pltpu.VMEM((2,) + tile, F32),
                        pltpu.VMEM((2,) + tile, F32), pltpu.SemaphoreType.DMA((2,)),
                        pltpu.SemaphoreType.DMA((2,)), pltpu.SemaphoreType.DMA((2,))],
        compiler_params=_cparams(("arbitrary",)),
        name="combine",
    )(dest4, dest4, tw4, gt2t, xn, ys)


def _rope_tables(T):
    rows = T // GRID_W
    row = jnp.repeat(jnp.arange(rows, dtype=F32), GRID_W)
    col = jnp.tile(jnp.arange(GRID_W, dtype=F32), rows)
    inv = ROPE_BASE ** (-jnp.arange(0, AXIS_ROT, 2, dtype=F32) / AXIS_ROT)
    ang_r, ang_c = row[:, None] * inv, col[:, None] * inv
    m = AXIS_ROT // 2
    ang = jnp.concatenate([ang_r, ang_r, ang_c, ang_c], axis=1)
    sign = jnp.tile(jnp.concatenate([-jnp.ones((m,), F32), jnp.ones((m,), F32)]), 2)
    cos = jnp.tile(jnp.cos(ang), (1, LANES // HEAD_DIM))
    sin = jnp.tile(jnp.sin(ang) * sign, (1, LANES // HEAD_DIM))
    return cos, sin


def _head_perm():
    order = []
    for m in range(ATT_GROUP):
        for kv in range(ATT_KV_HEADS):
            h = kv * ATT_GROUP + m
            order.extend(range(h * HEAD_DIM, (h + 1) * HEAD_DIM))
    return np.asarray(order)


def kernel(x, c, ctx, c_ctx, w_mod, b_mod, norm1, norm2, w_in, q_norm, k_norm, attn_sink,
           w_alpha_f, b_alpha_f, w_alpha_b, b_alpha_b, gla_norm, w_branch_attn, w_branch_gla,
           w_out, w_router, b_router, w_exp_in, b_exp_in, w_exp_out, b_exp_out):
    B, T, D = x.shape
    depth = w_mod.shape[0]
    assert depth == 1, "single-layer kernel: the context stream update only feeds later layers"
    l = 0
    perm = _head_perm()

    rows = ((B + 1 + 7) // 8) * 8
    c_all = jnp.zeros((rows, D), F32).at[:B].set(c).at[B].set(c_ctx)
    mod = _modulation(c_all, w_mod[l], b_mod[l])
    sh1, sc1, gt1, sh2, sc2, gt2 = [mod[:B, j * D:(j + 1) * D].reshape(B, 1, D) for j in range(6)]
    csh1, csc1 = [jnp.broadcast_to(mod[B, j * D:(j + 1) * D].reshape(1, 1, D), (B, 1, D)) for j in range(2)]

    offs = np.concatenate([[0], np.cumsum(IN_SPLITS)])
    cols = lambda j: w_in[l][:, offs[j]:offs[j + 1]]
    wal = jnp.zeros((2 * GLA_RANK, 2 * GLA_K_W), F32)
    wal = wal.at[:GLA_RANK, :GLA_K_W].set(w_alpha_f[l]).at[GLA_RANK:, GLA_K_W:].set(w_alpha_b[l])
    pw = {
        "wq": cols(0)[:, perm].astype(BF16), "wk": cols(1).astype(BF16), "wv": cols(2).astype(BF16),
        "wgq": cols(3).astype(BF16), "wgk": cols(4).astype(BF16), "wgv": cols(5).astype(BF16),
        "wgr": cols(6).astype(BF16), "wga": cols(9).astype(BF16), "wgg": cols(10).astype(BF16),
        "wlr": jnp.concatenate([cols(7), cols(8)], axis=1).astype(BF16),
        "qn": jnp.tile(q_norm[l], LANES // HEAD_DIM).reshape(1, LANES),
        "kn": jnp.tile(k_norm[l], LANES // HEAD_DIM).reshape(1, LANES),
        "wal": wal.astype(BF16),
        "bal": jnp.concatenate([b_alpha_f[l], b_alpha_b[l]]).reshape(1, 2 * GLA_K_W),
    }
    cos, sin = _rope_tables(T)
    n1 = norm1[l].reshape(1, D)
    tm = min(512, T)
    aq, ak, av, gq, gk, gv, gr, ga, gg, la = _inproj(
        x, sh1, sc1, n1, {"cos": cos, "sin": sin}, pw, rope=True, full=True, tm=tm)
    cak, cav, cgk, cgv, cla = _inproj(
        ctx, csh1, csc1, n1, None, pw, rope=False, full=False, tm=min(256, ctx.shape[1]))

    attn_o = _attention(attn_sink[l], aq, ak, av, cak, cav)
    gla_o = _gla(gq, gk, gv, la, cgk, cgv, cla)

    wr = jnp.zeros((D, LANES), F32).at[:, :N_EXPERTS].set(w_router[l])
    wrh = wr.astype(BF16)
    mw = {
        "gn": jnp.tile(gla_norm[l], GLA_HEADS).reshape(1, GLA_V_W),
        "wba": w_branch_attn[l][perm, :].astype(BF16), "wbg": w_branch_gla[l].astype(BF16),
        "wo": w_out[l].astype(BF16), "wrh": wrh, "wrl": (wr - wrh.astype(F32)).astype(BF16),
        "br": jnp.full((1, LANES), NEG, F32).at[0, :N_EXPERTS].set(b_router[l]),
    }
    xn, h2, ti, rk, tw, cnt = _merge(x, attn_o, gla_o, gr, ga, gg, gt1, sc2, sh2, norm2[l].reshape(1, D), mw, tm=tm)

    n = B * T
    counts = cnt[0, :N_EXPERTS]
    padded = (counts + MOE_STEP - 1) // MOE_STEP * MOE_STEP
    pad_end = jnp.cumsum(padded)
    pad_start = (pad_end - padded).astype(jnp.int32)
    zstart = jnp.maximum(pad_end - MOE_STEP, 0).astype(jnp.int32)
    cap = (n * TOP_K + N_EXPERTS * (MOE_STEP - 1)) // MOE_STEP * MOE_STEP
    n_blk = cap // MOE_STEP
    row0 = jnp.arange(n_blk, dtype=jnp.int32) * MOE_STEP
    blk_e = jnp.minimum(jnp.sum((pad_end[None, :] <= row0[:, None]).astype(jnp.int32), axis=1), N_EXPERTS - 1)
    onehot = (blk_e[:, None] == jnp.arange(N_EXPERTS, dtype=jnp.int32)[None, :]).astype(jnp.int32)
    valid_end = jnp.sum(onehot * (pad_start + counts)[None, :], axis=1)
    nv = jnp.clip(valid_end - row0, 0, MOE_STEP).astype(jnp.int32)
    n_used = (pad_end[-1] // MOE_STEP).astype(jnp.int32).reshape(1)
    changed = jnp.concatenate([jnp.ones((1,), bool), blk_e[1:] != blk_e[:-1]])
    first = (changed & (nv > 0)).astype(jnp.int32)
    parity = ((jnp.cumsum(first) - 1) % 2).astype(jnp.int32)
    eid = jnp.arange(N_EXPERTS, dtype=jnp.int32)
    later = (eid[None, :] > eid[:, None]) & (counts[None, :] > 0)
    nxt_of = jnp.where(jnp.any(later, axis=1), jnp.argmax(later, axis=1), -1).astype(jnp.int32)
    nxt_e = jnp.sum(onehot * nxt_of[None, :], axis=1).astype(jnp.int32)

    tb = min(256, T)
    dest_t, tw_t = _dest(ti.reshape(n, LANES), rk.reshape(n, LANES), tw.reshape(n, LANES), pad_start,
                         tm=min(2048, n))
    per_block = lambda a: a[:TOP_K].reshape(TOP_K, n // tb, tb).transpose(1, 2, 0).reshape(n // tb, 1, tb * TOP_K)
    dest4, tw4 = per_block(dest_t), per_block(tw_t)
    xs = _dispatch(dest4, zstart, n_used, h2.reshape(n, D), cap, tb=tb)
    ys = _experts(blk_e, nv, first, nxt_e, parity, xs, w_exp_in[l], b_exp_in[l], w_exp_out[l], b_exp_out[l])
    out = _combine(dest4, tw4, gt2.reshape(B, ROW_TILE, LANES), xn.reshape(n, D), ys, tb=tb, seq=T)
    return out.reshape(B, T, D)
```

```python
import functools

import numpy as np
import jax
import jax.numpy as jnp
from jax import lax
from jax.experimental import pallas as pl
from jax.experimental.pallas import tpu as pltpu

F32 = jnp.float32
BF16 = jnp.bfloat16

D_MODEL = 1024
GRID_W = 64
EPS = 1e-6
ATT_HEADS = 8
ATT_KV_HEADS = 2
ATT_GROUP = ATT_HEADS // ATT_KV_HEADS
HEAD_DIM = 64
WINDOW = 128
ATT_BLOCK = 128
ROPE_BASE = 10000.0
AXIS_ROT = HEAD_DIM // 2
GLA_HEADS = 4
GLA_DK = 64
GLA_DV = 128
GLA_RANK = 16
GLA_TAU = 16.0
N_EXPERTS = 32
TOP_K = 4
D_FF = D_MODEL
SWIGLU_ALPHA = 1.702
SWIGLU_LIMIT = 7.0
MOE_STEP = 512

ATT_W = ATT_HEADS * HEAD_DIM
ATT_KV_W = ATT_KV_HEADS * HEAD_DIM
GLA_K_W = GLA_HEADS * GLA_DK
GLA_V_W = GLA_HEADS * GLA_DV
IN_SPLITS = (ATT_W, ATT_KV_W, ATT_KV_W, GLA_K_W, GLA_K_W, GLA_V_W, GLA_V_W, GLA_RANK, GLA_RANK, D_MODEL, D_MODEL)

LANES = 128
ROW_TILE = D_MODEL // LANES
VMEM_LIMIT = 56 * 1024 * 1024
NEG = -1e30

GLA_C = 64
GLA_SUB = 4
GLA_LEVELS = 4


def _cparams(sem):
    return pltpu.CompilerParams(dimension_semantics=sem, vmem_limit_bytes=VMEM_LIMIT)


def _full(shape):
    n = len(shape)
    return pl.BlockSpec(shape, lambda *_: (0,) * n)


def _mod_kernel(c_ref, w_ref, b_ref, o_ref):
    c = c_ref[...]
    s = c * (1.0 / (1.0 + jnp.exp(-c)))
    o_ref[...] = jnp.dot(s, w_ref[...], preferred_element_type=F32,
                         precision=lax.Precision.HIGHEST) + b_ref[...]


def _modulation(c_all, w_mod, b_mod):
    rows = c_all.shape[0]
    n = w_mod.shape[1]
    tn = 1536
    return pl.pallas_call(
        _mod_kernel,
        out_shape=jax.ShapeDtypeStruct((rows, n), F32),
        grid=(n // tn,),
        in_specs=[pl.BlockSpec((rows, D_MODEL), lambda j: (0, 0)),
                  pl.BlockSpec((D_MODEL, tn), lambda j: (0, j)),
                  pl.BlockSpec((1, tn), lambda j: (0, j))],
        out_specs=pl.BlockSpec((rows, tn), lambda j: (0, j)),
        compiler_params=_cparams(("arbitrary",)),
        name="mod",
    )(c_all, w_mod, b_mod.reshape(1, n))


def _pair_norm(a, g, lo):
    s = a * a
    tot = jnp.sum(s, axis=-1, keepdims=True)
    slo = jnp.sum(jnp.where(lo, s, 0.0), axis=-1, keepdims=True)
    ms = jnp.where(lo, slo, tot - slo) * (1.0 / HEAD_DIM)
    return a * lax.rsqrt(ms + EPS) * g


def _rope(y, cos, sin, first):
    up = pltpu.roll(y, LANES - AXIS_ROT // 2, 1)
    dn = pltpu.roll(y, AXIS_ROT // 2, 1)
    return y * cos + jnp.where(first, up, dn) * sin


def _inproj_kernel(*refs, rope, full):
    if full:
        (x_ref, sh_ref, sc_ref, n1_ref, cos_ref, sin_ref, qn_ref, kn_ref, wal_ref, bal_ref,
         wq, wk, wv, wgq, wgk, wgv, wgr, wga, wgg, wlr,
         oq, ok, ov, ogq, ogk, ogv, ogr, oga, ogg, ola) = refs
    else:
        (x_ref, sh_ref, sc_ref, n1_ref, kn_ref, wal_ref, bal_ref,
         wk, wv, wgk, wgv, wlr,
         ok, ov, ogk, ogv, ola) = refs
    rows_total = x_ref.shape[0]
    nsplit = 2 if rows_total % 512 == 0 else 1
    tm = rows_total // nsplit
    lane = lax.broadcasted_iota(jnp.int32, (tm, LANES), 1)
    lo = lane < HEAD_DIM
    first = (lane % AXIS_ROT) < (AXIS_ROT // 2)
    for part in range(nsplit):
        r = slice(part * tm, (part + 1) * tm)
        x = x_ref[r, :]
        ms = jnp.mean(x * x, axis=-1, keepdims=True)
        h = (x * lax.rsqrt(ms + EPS) * n1_ref[...]) * (1.0 + sc_ref[...]) + sh_ref[...]
        hb = h.astype(BF16)

        def proj(w_ref):
            return jnp.dot(hb, w_ref[...], preferred_element_type=F32)

        if rope:
            cos = cos_ref[r, :]
            sin = sin_ref[r, :]

        k = _pair_norm(proj(wk), kn_ref[...], lo)
        if rope:
            k = _rope(k, cos, sin, first)
        ok[r, :] = k.astype(BF16)
        ov[r, :] = proj(wv).astype(BF16)
        ogk[r, :] = proj(wgk).astype(BF16)
        ogv[r, :] = proj(wgv).astype(BF16)
        lr = proj(wlr).astype(BF16)
        z = jnp.dot(lr, wal_ref[...], preferred_element_type=F32) + bal_ref[...]
        ola[r, :] = (jnp.minimum(z, 0.0) - jnp.log(1.0 + jnp.exp(-jnp.abs(z)))) * (1.0 / GLA_TAU)
        if full:
            q = proj(wq)
            for p in range(ATT_W // LANES):
                y = _pair_norm(q[:, p * LANES:(p + 1) * LANES], qn_ref[...], lo)
                if rope:
                    y = _rope(y, cos, sin, first)
                oq[r, p * LANES:(p + 1) * LANES] = (y * HEAD_DIM ** -0.5).astype(BF16)
            ogq[r, :] = (proj(wgq) * GLA_DK ** -0.5).astype(BF16)
            sigmoid = lambda t: 0.5 * jnp.tanh(0.5 * t) + 0.5
            g = proj(wgr)
            ogr[r, :] = (g * sigmoid(g)).astype(BF16)
            oga[r, :] = sigmoid(proj(wga)).astype(BF16)
            ogg[r, :] = sigmoid(proj(wgg)).astype(BF16)


def _inproj(x, sh, sc, norm1, tabs, wts, *, rope, full, tm):
    B, T, D = x.shape
    grid = (B, T // tm)
    row = lambda w: pl.BlockSpec((None, tm, w), lambda b, t: (b, t, 0))
    vec = pl.BlockSpec((None, 1, D), lambda b, t: (b, 0, 0))
    tab = pl.BlockSpec((tm, LANES), lambda b, t: (t, 0))
    if full:
        names = ("wq", "wk", "wv", "wgq", "wgk", "wgv", "wgr", "wga", "wgg", "wlr")
        ins = [x, sh, sc, norm1, tabs["cos"], tabs["sin"], wts["qn"], wts["kn"], wts["wal"], wts["bal"]]
        specs = [row(D), vec, vec, _full((1, D)), tab, tab, _full((1, LANES)), _full((1, LANES)),
                 _full(wts["wal"].shape), _full(wts["bal"].shape)]
        out_w = (ATT_W, ATT_KV_W, ATT_KV_W, GLA_K_W, GLA_K_W, GLA_V_W, GLA_V_W, D, D)
    else:
        names = ("wk", "wv", "wgk", "wgv", "wlr")
        ins = [x, sh, sc, norm1, wts["kn"], wts["wal"], wts["bal"]]
        specs = [row(D), vec, vec, _full((1, D)), _full((1, LANES)),
                 _full(wts["wal"].shape), _full(wts["bal"].shape)]
        out_w = (ATT_KV_W, ATT_KV_W, GLA_K_W, GLA_V_W)
    ins += [wts[n] for n in names]
    specs += [_full(wts[n].shape) for n in names]
    out_shape = [jax.ShapeDtypeStruct((B, T, w), BF16) for w in out_w]
    out_shape.append(jax.ShapeDtypeStruct((B, T, 2 * GLA_K_W), F32))
    out_specs = [row(w) for w in out_w] + [row(2 * GLA_K_W)]
    return pl.pallas_call(
        functools.partial(_inproj_kernel, rope=rope, full=full),
        out_shape=out_shape, grid=grid, in_specs=specs, out_specs=out_specs,
        compiler_params=_cparams(("parallel", "arbitrary")),
        name="inproj_full" if full else "inproj_ctx",
    )(*ins)


def _attn_kernel(*refs, seq, nsb):
    sink_ref, band_ref, q_ref = refs[:3]
    kblocks = refs[3:nsb + 5]
    kx_ref = refs[nsb + 5]
    vblocks = refs[nsb + 6:2 * nsb + 8]
    vx_ref, o_ref = refs[2 * nsb + 8:]
    n = pl.program_id(1)
    blk = ATT_BLOCK
    nb = seq // blk
    nslab = ATT_W // LANES
    rows = nslab * blk
    lane = lax.broadcasted_iota(jnp.int32, (blk, LANES), 1)
    lo = lane < HEAD_DIM
    hrow = lax.broadcasted_iota(jnp.int32, (rows, 1), 0) // blk
    band = band_ref[...]
    for sb in range(nsb):
        kcat = jnp.concatenate([r[...] for r in kblocks[sb:sb + 3]] + [kx_ref[...]], axis=0)
        vcat = jnp.concatenate([r[...] for r in vblocks[sb:sb + 3]] + [vx_ref[...]], axis=0)
        first = nsb * n + sb - 1
        q = q_ref[sb * blk:(sb + 1) * blk, :]
        outs = []
        for kv in range(ATT_KV_HEADS):
            keep = lo if kv == 0 else jnp.logical_not(lo)
            qs = jnp.concatenate([jnp.where(keep, q[:, m * LANES:(m + 1) * LANES], jnp.zeros((blk, LANES), BF16))
                                  for m in range(nslab)], axis=0)
            s = lax.dot_general(qs, kcat, (((1,), (1,)), ((), ())), preferred_element_type=F32) + band
            s = jnp.concatenate([jnp.where(first >= 0, s[:, :blk], NEG), s[:, blk:2 * blk],
                                 jnp.where(first + 2 < nb, s[:, 2 * blk:3 * blk], NEG), s[:, 3 * blk:]], axis=1)
            snk = jnp.zeros((rows, 1), F32)
            for m in range(nslab):
                snk = jnp.where(hrow == m, sink_ref[kv * ATT_GROUP + m], snk)
            mx = jnp.maximum(jnp.max(s, axis=-1, keepdims=True), snk)
            p = jnp.exp(s - mx)
            den = jnp.sum(p, axis=-1, keepdims=True) + jnp.exp(snk - mx)
            outs.append(jnp.dot(p.astype(BF16), vcat, preferred_element_type=F32) / den)
        for m in range(nslab):
            o_ref[sb * blk:(sb + 1) * blk, m * LANES:(m + 1) * LANES] = jnp.where(
                lo, outs[0][m * blk:(m + 1) * blk], outs[1][m * blk:(m + 1) * blk]).astype(BF16)


def _attention(sink, aq, ak, av, cak, cav):
    B, T, _ = aq.shape
    lc = cak.shape[1]
    blk = ATT_BLOCK
    nb = T // blk
    nsb = 4 if nb % 4 == 0 else 2
    assert nb % nsb == 0
    kvspec = lambda off: pl.BlockSpec((None, blk, ATT_KV_W),
                                      lambda b, n: (b, jnp.clip(nsb * n + off, 0, nb - 1), 0))
    cspec = pl.BlockSpec((None, lc, ATT_KV_W), lambda b, n: (b, 0, 0))
    qspec = pl.BlockSpec((None, nsb * blk, ATT_W), lambda b, n: (b, n, 0))
    kvs = [kvspec(off) for off in range(-1, nsb + 1)]
    rows = (ATT_W // LANES) * blk
    qi = np.arange(rows)[:, None] % blk
    kj = np.arange(3 * blk + lc)[None, :]
    band = jnp.asarray(np.where((np.abs(kj - blk - qi) <= WINDOW) | (kj >= 3 * blk), 0.0, NEG), F32)
    return pl.pallas_call(
        functools.partial(_attn_kernel, seq=T, nsb=nsb),
        out_shape=jax.ShapeDtypeStruct((B, T, ATT_W), BF16),
        grid=(B, nb // nsb),
        in_specs=[pl.BlockSpec(memory_space=pltpu.SMEM), _full(band.shape), qspec] + kvs + [cspec] + kvs + [cspec],
        out_specs=qspec,
        compiler_params=_cparams(("parallel", "arbitrary")),
        name="attn",
    )(sink, band, aq, *([ak] * (nsb + 2)), cak, *([av] * (nsb + 2)), cav)


def _gla_constants():
    C, sub, L = GLA_C, GLA_SUB, GLA_LEVELS
    i = np.arange(C)[:, None]
    t = np.arange(C)[None, :]
    tabs = [t <= i]
    rowq, same = [], []
    for l in range(L):
        s = C >> l
        mid = (i // s) * s + s // 2
        rowq.append(np.broadcast_to(i >= mid, (C, C)))
        same.append((i // s) == (t // s))
    shifts, dmask, dvalid = [], [t == i], []
    for d in range(1, sub):
        ok = (i % sub) >= d
        shifts.append(ok & (t == i - d))
        dmask.append(ok & (t == i - d))
        dvalid.append(np.broadcast_to(ok, (C, C)))
    flip = lambda a: a[::-1, ::-1]
    tile = lambda a: np.tile(a, (1, GLA_HEADS))

    def both(xs, lanes):
        f = (lambda a: tile(a)) if lanes else (lambda a: a)
        return np.stack([np.concatenate([f(a) for a in xs], 0),
                         np.concatenate([f(flip(a)) for a in xs], 0)]).astype(np.float32)

    hk = np.arange(GLA_K_W) // GLA_DK
    hv = np.arange(GLA_V_W) // GLA_DV
    ind = (hk[:, None] == hk[None, :]).astype(np.float32)
    bdv = (hk[:, None] == hv[None, :]).astype(np.float32)
    return (both(tabs, False), both(shifts, False), both(rowq, True), both(same, True),
            both(dmask, True), ind, bdv, np.ascontiguousarray(bdv.T), both(dvalid, True))


def _gla_chunk(q_b, k_b, v_b, la, cst, d):
    tri_ref, shm_ref, lv_ref, sm_ref, dm_ref, ind_ref, bdv_ref, hm_ref, dv_ref = cst
    C = GLA_C
    kw = GLA_K_W
    q = q_b.astype(F32)
    k = k_b.astype(F32)
    hi = la.astype(BF16)
    r1 = la - hi.astype(F32)
    mid = r1.astype(BF16)
    lo = (r1 - mid.astype(F32)).astype(BF16)
    b3 = jnp.dot(tri_ref[d], jnp.concatenate([hi, mid, lo], axis=1), preferred_element_type=F32)
    b = b3[:, :kw] + b3[:, kw:2 * kw] + b3[:, 2 * kw:]
    last = b[C - 1:C] if d == 0 else b[0:1]

    qt = (q * jnp.exp(b)).astype(BF16)
    kt = (k * jnp.exp(last - b)).astype(BF16)
    gamma = jnp.exp(last)

    ind = ind_ref[...]
    a = None
    for l in range(GLA_LEVELS):
        s = C >> l
        off = s // 2 - 1 if d == 0 else s // 2
        bref = jnp.concatenate([jnp.broadcast_to(b[st + off:st + off + 1], (s, kw)) for st in range(0, C, s)], axis=0)
        rq = lv_ref[d, l * C:(l + 1) * C, :]
        el = jnp.exp((b - bref) * (2.0 * rq - 1.0))
        qh = (q * (el * rq)).astype(BF16)
        kh = (k * (el * (1.0 - rq))).astype(BF16)
        bdk = jnp.concatenate([kh] * GLA_HEADS, axis=0) * ind
        al = lax.dot_general(qh, bdk, (((1,), (1,)), ((), ())), preferred_element_type=F32)
        a = al if l == 0 else a + al * sm_ref[d, l * C:(l + 1) * C, :]
    ksh = jnp.dot(shm_ref[d], k_b, preferred_element_type=F32)
    ps = [q * k]
    for j in range(1, GLA_SUB):
        bsh = pltpu.roll(b, j if d == 0 else C - j, 0)
        ej = jnp.exp((b - bsh) * dv_ref[d, (j - 1) * C:j * C, :])
        ps.append(q * ksh[(j - 1) * C:j * C] * ej)
    w = jnp.dot(jnp.concatenate(ps, axis=0).astype(BF16), ind, preferred_element_type=F32)
    for j in range(GLA_SUB):
        a = a + w[j * C:(j + 1) * C] * dm_ref[d, j * C:(j + 1) * C, :]

    bdv = jnp.concatenate([v_b] * GLA_HEADS, axis=0) * bdv_ref[...]
    o_intra = jnp.dot(a.astype(BF16), bdv, preferred_element_type=F32)
    upd = lax.dot_general(v_b, kt, (((0,), (0,)), ((), ())), preferred_element_type=F32) * hm_ref[...]
    return o_intra, qt, upd, gamma


def _gla_kernel(qf_ref, kf_ref, vf_ref, laf_ref, qb_ref, kb_ref, vb_ref, lab_ref,
                ckf_ref, cvf_ref, claf_ref, ckb_ref, cvb_ref, clab_ref,
                tri_ref, shm_ref, lv_ref, sm_ref, dm_ref, ind_ref, bdv_ref, hm_ref, dv_ref,
                of_ref, ob_ref, st_ref, *, n_ctx_steps):
    s = pl.program_id(1)
    C = GLA_C
    cst = (tri_ref, shm_ref, lv_ref, sm_ref, dm_ref, ind_ref, bdv_ref, hm_ref, dv_ref)

    @pl.when(s == 0)
    def _():
        st_ref[...] = jnp.zeros_like(st_ref)

    is_ctx = s < n_ctx_steps
    dirs = ((0, qf_ref, kf_ref, vf_ref, laf_ref, ckf_ref, cvf_ref, claf_ref, of_ref),
            (1, qb_ref, kb_ref, vb_ref, lab_ref, ckb_ref, cvb_ref, clab_ref, ob_ref))
    nbat = qf_ref.shape[0]
    states = [[st_ref[bi, 0], st_ref[bi, 1]] for bi in range(nbat)]
    for idx in range(2):
        for bi in range(nbat):
            for d, q_ref, k_ref, v_ref, la_ref, ck_ref, cv_ref, cla_ref, o_ref in dirs:
                c = idx if d == 0 else 1 - idx
                rows = slice(c * C, (c + 1) * C)
                k_b = jnp.where(is_ctx, ck_ref[bi, rows, :], k_ref[bi, rows, :])
                v_b = jnp.where(is_ctx, cv_ref[bi, rows, :], v_ref[bi, rows, :])
                la = jnp.where(is_ctx, cla_ref[bi, rows, :], la_ref[bi, rows, :])
                o_intra, qt, upd, gamma = _gla_chunk(q_ref[bi, rows, :], k_b, v_b, la, cst, d)
                st = states[bi][d]
                o_ref[bi, rows, :] = o_intra + lax.dot_general(qt, st.astype(BF16), (((1,), (1,)), ((), ())),
                                                               preferred_element_type=F32)
                states[bi][d] = st * gamma + upd
    for bi in range(nbat):
        st_ref[bi, 0] = states[bi][0]
        st_ref[bi, 1] = states[bi][1]


def _gla(gq, gk, gv, la, cgk, cgv, cla):
    B, T, _ = gq.shape
    lc = cgk.shape[1]
    R = 2 * GLA_C
    assert lc % R == 0 and T % R == 0
    n_ctx, n_lat = lc // R, T // R
    consts = _gla_constants()
    tri, shm = jnp.asarray(consts[0], BF16), jnp.asarray(consts[1], BF16)
    lv, sm, dm = [jnp.asarray(c) for c in consts[2:5]]
    ind, bdv = jnp.asarray(consts[5], BF16), jnp.asarray(consts[6], BF16)
    hm, dv = jnp.asarray(consts[7]), jnp.asarray(consts[8])

    def lat(s, d):
        j = jnp.maximum(s - n_ctx, 0)
        return j if d == 0 else n_lat - 1 - j

    def ctx(s, d):
        j = jnp.minimum(s, n_ctx - 1)
        return j if d == 0 else n_ctx - 1 - j

    nbat = 2 if B % 2 == 0 else 1
    lspec = lambda w, d, c=0: pl.BlockSpec((nbat, R, w), lambda b, s: (b, lat(s, d), c))
    cspec = lambda w, d, c=0: pl.BlockSpec((nbat, R, w), lambda b, s: (b, ctx(s, d), c))
    lat_specs = lambda d: [lspec(GLA_K_W, d), lspec(GLA_K_W, d), lspec(GLA_V_W, d), lspec(GLA_K_W, d, d)]
    ctx_specs = lambda d: [cspec(GLA_K_W, d), cspec(GLA_V_W, d), cspec(GLA_K_W, d, d)]
    cs = [tri, shm, lv, sm, dm, ind, bdv, hm, dv]
    return pl.pallas_call(
        functools.partial(_gla_kernel, n_ctx_steps=n_ctx),
        out_shape=[jax.ShapeDtypeStruct((B, T, GLA_V_W), F32)] * 2,
        grid=(B // nbat, n_ctx + n_lat),
        in_specs=lat_specs(0) + lat_specs(1) + ctx_specs(0) + ctx_specs(1) + [_full(c.shape) for c in cs],
        out_specs=[lspec(GLA_V_W, 0), lspec(GLA_V_W, 1)],
        scratch_shapes=[pltpu.VMEM((nbat, 2, GLA_V_W, GLA_K_W), F32)],
        compiler_params=_cparams(("parallel", "arbitrary")),
        name="gla",
    )(gq, gk, gv, la, gq, gk, gv, la, cgk, cgv, cla, cgk, cgv, cla, *cs)


def _merge_kernel(x_ref, at_ref, of_ref, ob_ref, gr_ref, ga_ref, gg_ref, gt1_ref, sc2_ref, sh2_ref,
                  n2_ref, gn_ref, ltri_ref, wba_ref, wbg_ref, wo_ref, wrh_ref, wrl_ref, br_ref,
                  xn_ref, h2_ref, ti_ref, rk_ref, tw_ref, cnt_ref, carry_ref):
    tm = x_ref.shape[0]

    @pl.when((pl.program_id(0) == 0) & (pl.program_id(1) == 0))
    def _():
        carry_ref[...] = jnp.zeros_like(carry_ref)

    go = of_ref[...] + ob_ref[...]
    parts = []
    for h in range(GLA_HEADS):
        gh = go[:, h * GLA_DV:(h + 1) * GLA_DV]
        ms = jnp.mean(gh * gh, axis=-1, keepdims=True)
        parts.append(gh * lax.rsqrt(ms + EPS))
    o = jnp.concatenate(parts, axis=1) * gn_ref[...] * gr_ref[...].astype(F32)
    ya = jnp.dot(at_ref[...], wba_ref[...], preferred_element_type=F32)
    yg = jnp.dot(o.astype(BF16), wbg_ref[...], preferred_element_type=F32)
    y = ga_ref[...].astype(F32) * ya + gg_ref[...].astype(F32) * yg
    z = jnp.dot(y.astype(BF16), wo_ref[...], preferred_element_type=F32)
    xn = x_ref[...] + gt1_ref[...] * z
    xn_ref[...] = xn
    ms = jnp.mean(xn * xn, axis=-1, keepdims=True)
    h2 = (xn * lax.rsqrt(ms + EPS) * n2_ref[...]) * (1.0 + sc2_ref[...]) + sh2_ref[...]
    hh = h2.astype(BF16)
    hl = (h2 - hh.astype(F32)).astype(BF16)
    h2_ref[...] = h2
    logits = (jnp.dot(hh, wrh_ref[...], preferred_element_type=F32)
              + jnp.dot(hl, wrh_ref[...], preferred_element_type=F32)
              + jnp.dot(hh, wrl_ref[...], preferred_element_type=F32)) + br_ref[...]
    lane = lax.broadcasted_iota(jnp.int32, (tm, LANES), 1).astype(F32)
    vals, idxs = [], []
    l = logits
    for _ in range(TOP_K):
        m = jnp.max(l, axis=-1, keepdims=True)
        ix = jnp.min(jnp.where(l == m, lane, float(LANES)), axis=-1, keepdims=True)
        vals.append(m)
        idxs.append(ix)
        l = jnp.where(lane == ix, -3.0e38, l)
    ex = [jnp.exp(v - vals[0]) for v in vals]
    den = ex[0] + ex[1] + ex[2] + ex[3]
    mh = jnp.zeros((tm, LANES), F32)
    for j in range(TOP_K):
        mh = mh + jnp.where(lane == idxs[j], 1.0, 0.0)
    pc = jnp.dot(ltri_ref[...], mh.astype(BF16), preferred_element_type=F32) + carry_ref[...]
    ti = jnp.zeros((tm, LANES), F32)
    rk = jnp.zeros((tm, LANES), F32)
    tw = jnp.zeros((tm, LANES), F32)
    for j in range(TOP_K):
        rj = jnp.sum(jnp.where(lane == idxs[j], pc, 0.0), axis=-1, keepdims=True)
        ti = jnp.where(lane == float(j), idxs[j], ti)
        rk = jnp.where(lane == float(j), rj, rk)
        tw = jnp.where(lane == float(j), ex[j] / den, tw)
    ti_ref[...] = ti.astype(jnp.int32)
    rk_ref[...] = rk.astype(jnp.int32)
    tw_ref[...] = tw
    total = carry_ref[...] + jnp.sum(mh, axis=0, keepdims=True)
    carry_ref[...] = total
    cnt_ref[...] = total.astype(jnp.int32)


def _merge(x, attn_o, gla_o, gr, ga, gg, gt1, sc2, sh2, norm2, wts, *, tm):
    B, T, D = x.shape
    row = lambda w: pl.BlockSpec((None, tm, w), lambda b, t: (b, t, 0))
    vec = pl.BlockSpec((None, 1, D), lambda b, t: (b, 0, 0))
    names = ("wba", "wbg", "wo", "wrh", "wrl", "br")
    ltri = jnp.asarray(np.tril(np.ones((tm, tm), np.float32), -1), BF16)
    return pl.pallas_call(
        _merge_kernel,
        out_shape=[jax.ShapeDtypeStruct((B, T, D), F32), jax.ShapeDtypeStruct((B, T, D), F32),
                   jax.ShapeDtypeStruct((B, T, LANES), jnp.int32), jax.ShapeDtypeStruct((B, T, LANES), jnp.int32),
                   jax.ShapeDtypeStruct((B, T, LANES), F32), jax.ShapeDtypeStruct((1, LANES), jnp.int32)],
        grid=(B, T // tm),
        in_specs=[row(D), row(ATT_W), row(GLA_V_W), row(GLA_V_W),
                  row(GLA_V_W), row(D), row(D), vec, vec, vec,
                  _full((1, D)), _full((1, GLA_V_W)), _full((tm, tm))] + [_full(wts[n].shape) for n in names],
        out_specs=[row(D), row(D), row(LANES), row(LANES), row(LANES), _full((1, LANES))],
        scratch_shapes=[pltpu.VMEM((1, LANES), F32)],
        compiler_params=_cparams(("arbitrary", "arbitrary")),
        name="merge",
    )(x, attn_o, gla_o[0], gla_o[1], gr, ga, gg, gt1, sc2, sh2, norm2, wts["gn"], ltri, *[wts[n] for n in names])


def _dest_kernel(ti_ref, rk_ref, tw_ref, ps_ref, o_ref, w_ref):
    tm = ti_ref.shape[0]
    lane = lax.broadcasted_iota(jnp.int32, (tm, LANES), 1)
    ti = ti_ref[...]
    ps = ps_ref[...].astype(F32)
    out = jnp.where(lane < TOP_K, rk_ref[...], 0).astype(F32)
    for k in range(TOP_K):
        start = jnp.sum(jnp.where(lane == ti[:, k:k + 1], ps, 0.0), axis=-1, keepdims=True)
        out = out + jnp.where(lane == k, start, 0.0)
    o_ref[...] = jnp.transpose(out)[0:8, :].astype(jnp.int32)
    w_ref[...] = jnp.transpose(tw_ref[...])[0:8, :]


def _dest(ti, rk, tw, pad_start, *, tm):
    n = ti.shape[0]
    ps = jnp.zeros((1, LANES), jnp.int32).at[0, :N_EXPERTS].set(pad_start)
    row = pl.BlockSpec((tm, LANES), lambda i: (i, 0))
    col = pl.BlockSpec((8, tm), lambda i: (0, i))
    return pl.pallas_call(
        _dest_kernel,
        out_shape=[jax.ShapeDtypeStruct((8, n), jnp.int32), jax.ShapeDtypeStruct((8, n), F32)],
        grid=(n // tm,), in_specs=[row, row, row, _full((1, LANES))], out_specs=[col, col],
        compiler_params=_cparams(("arbitrary",)), name="dest",
    )(ti, rk, tw, ps)


def _dispatch_kernel(d_ref, zs_ref, nu_ref, h_ref, xs_ref, buf, zbuf, isem, sem, zsem,
                     *, tb, nsteps):
    s = pl.program_id(0)
    nblk = xs_ref.shape[0] // MOE_STEP
    slot = s % 3
    nxt = (s + 1) % 3

    def loads(step, sl):
        r0 = pl.multiple_of(step * tb, tb)
        return [pltpu.make_async_copy(h_ref.at[pl.ds(r0, tb), pl.ds(j * LANES, LANES)], buf.at[sl, :, j, :],
                                      isem.at[sl]) for j in range(ROW_TILE)]

    def wait_rows(sl):
        for _ in range(TOP_K):
            pltpu.make_async_copy(buf.at[sl], xs_ref.at[pl.ds(0, tb)], sem.at[sl]).wait()

    @pl.when(s == 0)
    def _():
        zbuf[...] = jnp.zeros_like(zbuf)

        def zstart(e, c):
            z0 = pl.multiple_of(zs_ref[e], MOE_STEP)
            pltpu.make_async_copy(zbuf, xs_ref.at[pl.ds(z0, MOE_STEP)], zsem).start()
            return c

        def zwait(e, c):
            pltpu.make_async_copy(zbuf, xs_ref.at[pl.ds(0, MOE_STEP)], zsem).wait()
            return c

        lax.fori_loop(0, N_EXPERTS, zstart, 0)
        lax.fori_loop(0, N_EXPERTS, zwait, 0)

        def tstart(j, c):
            pltpu.make_async_copy(zbuf, xs_ref.at[pl.ds(pl.multiple_of(j * MOE_STEP, MOE_STEP), MOE_STEP)],
                                  zsem).start()
            return c

        lax.fori_loop(nu_ref[0], nblk, tstart, 0)
        lax.fori_loop(nu_ref[0], nblk, zwait, 0)
        for c in loads(0, 0):
            c.start()

    for c in loads(s, slot):
        c.wait()

    @pl.when(s + 1 < nsteps)
    def _():
        @pl.when(s >= 2)
        def _():
            wait_rows(nxt)
        for c in loads(s + 1, nxt):
            c.start()

    def issue(r, c):
        for k in range(TOP_K):
            d = d_ref[0, r * TOP_K + k]
            pltpu.make_async_copy(buf.at[slot, r], xs_ref.at[d], sem.at[slot]).start(priority=k % 2)
        return c

    lax.fori_loop(0, tb, issue, 0, unroll=8)

    @pl.when(s == nsteps - 1)
    def _():
        wait_rows(slot)
        if nsteps >= 2:
            wait_rows((s + 2) % 3)
        if nsteps >= 3:
            wait_rows(nxt)


def _dispatch(dest4, zstart, n_used, h2, cap, *, tb):
    n, D = h2.shape
    nsteps = n // tb
    idx = pl.BlockSpec((None, 1, tb * TOP_K), lambda s: (s, 0, 0), memory_space=pltpu.SMEM)
    smem = pl.BlockSpec(memory_space=pltpu.SMEM)
    anyspec = pl.BlockSpec(memory_space=pl.ANY)
    return pl.pallas_call(
        functools.partial(_dispatch_kernel, tb=tb, nsteps=nsteps),
        out_shape=jax.ShapeDtypeStruct((cap, ROW_TILE, LANES), F32),
        grid=(nsteps,),
        in_specs=[idx, smem, smem, anyspec],
        out_specs=anyspec,
        scratch_shapes=[pltpu.VMEM((3, tb, ROW_TILE, LANES), F32), pltpu.VMEM((MOE_STEP, ROW_TILE, LANES), F32),
                        pltpu.SemaphoreType.DMA((3,)), pltpu.SemaphoreType.DMA((3,)), pltpu.SemaphoreType.DMA(())],
        compiler_params=_cparams(("arbitrary",)),
        name="dispatch",
    )(dest4, zstart, n_used, h2)


def _expert_kernel(be_ref, nv_ref, fs_ref, nx_ref, pr_ref, xs_ref, w1_ref, b1_ref, w2_ref, b2_ref, ys_ref,
                   w1b, w2b, w1f, w2f, xin, yout, isem, osem, wsem, *, nsteps):
    i = pl.program_id(0)
    slot = i % 2

    def wloads(ex, sl):
        return [pltpu.make_async_copy(w1_ref.at[ex], w1f.at[sl], wsem.at[sl]),
                pltpu.make_async_copy(w2_ref.at[ex], w2f.at[sl], wsem.at[sl])]

    def loads(step, sl):
        r0 = pl.multiple_of(step * MOE_STEP, MOE_STEP)
        return [pltpu.make_async_copy(xs_ref.at[pl.ds(r0, MOE_STEP), j, :],
                                      xin.at[sl, :, pl.ds(j * LANES, LANES)], isem.at[sl]) for j in range(ROW_TILE)]

    def stores(step, sl):
        r0 = pl.multiple_of(step * MOE_STEP, MOE_STEP)
        return [pltpu.make_async_copy(yout.at[sl, :, pl.ds(j * LANES, LANES)],
                                      ys_ref.at[pl.ds(r0, MOE_STEP), j, :], osem.at[sl]) for j in range(ROW_TILE)]

    @pl.when(i == 0)
    def _():
        for c in loads(0, 0):
            c.start()
        for c in wloads(be_ref[0], 0):
            c.start()

    @pl.when(fs_ref[i] == 1)
    def _():
        par = pr_ref[i]
        for c in wloads(be_ref[i], par):
            c.wait()
        w1b[...] = w1f[par].astype(BF16)
        w2b[...] = w2f[par].astype(BF16)

        @pl.when(nx_ref[i] >= 0)
        def _():
            for c in wloads(nx_ref[i], 1 - par):
                c.start(priority=1)

    for c in loads(i, slot):
        c.wait()

    @pl.when(i + 1 < nsteps)
    def _():
        for c in loads(i + 1, 1 - slot):
            c.start()

    @pl.when(i >= 2)
    def _():
        for c in stores(i - 2, slot):
            c.wait()

    def mlp(rows):
        xb = xin[slot, 0:rows, :].astype(BF16)
        y = jnp.zeros((rows, D_MODEL), F32)
        fh = D_FF // 2
        for h in range(2):
            g = jnp.dot(xb, w1b[:, h * fh:(h + 1) * fh], preferred_element_type=F32) + b1_ref[:, h * fh:(h + 1) * fh]
            u = (jnp.dot(xb, w1b[:, D_FF + h * fh:D_FF + (h + 1) * fh], preferred_element_type=F32)
                 + b1_ref[:, D_FF + h * fh:D_FF + (h + 1) * fh])
            gate = jnp.minimum(g, SWIGLU_LIMIT)
            up = jnp.clip(u, -SWIGLU_LIMIT, SWIGLU_LIMIT)
            act = gate * (1.0 / (1.0 + jnp.exp(-SWIGLU_ALPHA * gate))) * (up + 1.0)
            y = y + jnp.dot(act.astype(BF16), w2b[h * fh:(h + 1) * fh, :], preferred_element_type=F32)
        yout[slot, 0:rows, :] = y + b2_ref[...]

    nv = nv_ref[i]
    quarter = MOE_STEP // 4
    for j in range(1, 5):
        rows = j * quarter

        @pl.when((nv > rows - quarter) & (nv <= rows))
        def _():
            mlp(rows)
            if rows < MOE_STEP:
                yout[slot, rows:, :] = jnp.zeros((MOE_STEP - rows, D_MODEL), F32)

    @pl.when(nv == 0)
    def _():
        yout[slot] = jnp.zeros((MOE_STEP, D_MODEL), F32)

    for c in stores(i, slot):
        c.start()

    @pl.when(i == nsteps - 1)
    def _():
        for c in stores(i, slot):
            c.wait()
        if nsteps >= 2:
            for c in stores(i - 1, 1 - slot):
                c.wait()


def _experts(blk_e, nv, first, nxt_e, parity, xs, w1, b1, w2, b2):
    cap = xs.shape[0]
    n_blk = cap // MOE_STEP
    ne = w1.shape[0]
    anyspec = pl.BlockSpec(memory_space=pl.ANY)
    bias = lambda w: pl.BlockSpec((None, 1, w), lambda i, be, *_: (be[i], 0, 0))
    gs = pltpu.PrefetchScalarGridSpec(
        num_scalar_prefetch=5, grid=(n_blk,),
        in_specs=[anyspec, anyspec, bias(2 * D_FF), anyspec, bias(D_MODEL)],
        out_specs=anyspec,
        scratch_shapes=[pltpu.VMEM((D_MODEL, 2 * D_FF), BF16), pltpu.VMEM((D_FF, D_MODEL), BF16),
                        pltpu.VMEM((2, D_MODEL, 2 * D_FF), F32), pltpu.VMEM((2, D_FF, D_MODEL), F32),
                        pltpu.VMEM((2, MOE_STEP, D_MODEL), F32), pltpu.VMEM((2, MOE_STEP, D_MODEL), F32),
                        pltpu.SemaphoreType.DMA((2,)), pltpu.SemaphoreType.DMA((2,)),
                        pltpu.SemaphoreType.DMA((2,))])
    return pl.pallas_call(
        functools.partial(_expert_kernel, nsteps=n_blk), grid_spec=gs,
        out_shape=jax.ShapeDtypeStruct((cap, ROW_TILE, LANES), F32),
        compiler_params=_cparams(("arbitrary",)),
        name="experts",
    )(blk_e, nv, first, nxt_e, parity, xs, w1, b1.reshape(ne, 1, 2 * D_FF), w2, b2.reshape(ne, 1, D_MODEL))


def _combine_kernel(dc_ref, dn_ref, tw_ref, gt2_ref, xn_ref, ys_ref, o_ref,
                    gbuf, xt, ot, gsem, xsem, osem, *, tb, nsteps):
    s = pl.program_id(0)
    slot = s % 2
    other = 1 - slot

    def xloads(step, sl):
        r0 = pl.multiple_of(step * tb, tb)
        return [pltpu.make_async_copy(xn_ref.at[pl.ds(r0, tb), pl.ds(j * LANES, LANES)], xt.at[sl, :, j, :],
                                      xsem.at[sl]) for j in range(ROW_TILE)]

    def ostores(step, sl):
        r0 = pl.multiple_of(step * tb, tb)
        return [pltpu.make_async_copy(ot.at[sl, :, j, :], o_ref.at[pl.ds(r0, tb), pl.ds(j * LANES, LANES)],
                                      osem.at[sl]) for j in range(ROW_TILE)]

    def gather_row(d_ref, sl, r):
        for k in range(TOP_K):
            d = d_ref[0, r * TOP_K + k]
            pltpu.make_async_copy(ys_ref.at[d], gbuf.at[sl, k, r], gsem.at[sl]).start(priority=k % 2)

    def wait_gathers(sl):
        for k in range(TOP_K):
            pltpu.make_async_copy(ys_ref.at[pl.ds(0, tb)], gbuf.at[sl, k], gsem.at[sl]).wait()

    @pl.when(s == 0)
    def _():
        def issue(r, c):
            gather_row(dc_ref, 0, r)
            return c

        lax.fori_loop(0, tb, issue, 0, unroll=8)
        for c in xloads(0, 0):
            c.start()

    wait_gathers(slot)
    for c in xloads(s, slot):
        c.wait()

    @pl.when(s >= 2)
    def _():
        for c in ostores(s - 2, slot):
            c.wait()

    @pl.when(s + 1 < nsteps)
    def _():
        def issue(r, c):
            gather_row(dn_ref, other, r)
            return c

        lax.fori_loop(0, tb, issue, 0, unroll=8)
        for c in xloads(s + 1, other):
            c.start()

    g2 = gt2_ref[...]

    def wsum(r, c):
        acc = tw_ref[0, r * TOP_K] * gbuf[slot, 0, r]
        for k in range(1, TOP_K):
            acc = acc + tw_ref[0, r * TOP_K + k] * gbuf[slot, k, r]
        ot[slot, r] = xt[slot, r] + g2 * acc
        return c

    lax.fori_loop(0, tb, wsum, 0, unroll=8)
    for c in ostores(s, slot):
        c.start()

    @pl.when(s == nsteps - 1)
    def _():
        for c in ostores(s, slot):
            c.wait()
        if nsteps >= 2:
            for c in ostores(s - 1, other):
                c.wait()


def _combine(dest4, tw4, gt2t, xn, ys, *, tb, seq):
    n, D = xn.shape
    nsteps = n // tb
    cur = lambda s: (s, 0, 0)
    nxt = lambda s: (jnp.minimum(s + 1, nsteps - 1), 0, 0)
    idx = lambda f: pl.BlockSpec((None, 1, tb * TOP_K), f, memory_space=pltpu.SMEM)
    anyspec = pl.BlockSpec(memory_space=pl.ANY)
    tile = (tb, ROW_TILE, LANES)
    return pl.pallas_call(
        functools.partial(_combine_kernel, tb=tb, nsteps=nsteps),
        out_shape=jax.ShapeDtypeStruct((n, D), F32),
        grid=(nsteps,),
        in_specs=[idx(cur), idx(nxt), idx(cur),
                  pl.BlockSpec((None, ROW_TILE, LANES), lambda s: ((s * tb) // seq, 0, 0)), anyspec, anyspec],
        out_specs=anyspec,
        scratch_shapes=[pltpu.VMEM((2, TOP_K) + tile, F32), pltpu.VMEM((2,) + tile, F32),
                        pltpu.VMEM((2,) + tile, F32), pltpu.SemaphoreType.DMA((2,)),
                        pltpu.SemaphoreType.DMA((2,)), pltpu.SemaphoreType.DMA((2,))],
        compiler_params=_cparams(("arbitrary",)),
        name="combine",
    )(dest4, dest4, tw4, gt2t, xn, ys)


def _rope_tables(T):
    rows = T // GRID_W
    row = jnp.repeat(jnp.arange(rows, dtype=F32), GRID_W)
    col = jnp.tile(jnp.arange(GRID_W, dtype=F32), rows)
    inv = ROPE_BASE ** (-jnp.arange(0, AXIS_ROT, 2, dtype=F32) / AXIS_ROT)
    ang_r, ang_c = row[:, None] * inv, col[:, None] * inv
    m = AXIS_ROT // 2
    ang = jnp.concatenate([ang_r, ang_r, ang_c, ang_c], axis=1)
    sign = jnp.tile(jnp.concatenate([-jnp.ones((m,), F32), jnp.ones((m,), F32)]), 2)
    cos = jnp.tile(jnp.cos(ang), (1, LANES // HEAD_DIM))
    sin = jnp.tile(jnp.sin(ang) * sign, (1, LANES // HEAD_DIM))
    return cos, sin


def _head_perm():
    order = []
    for m in range(ATT_GROUP):
        for kv in range(ATT_KV_HEADS):
            h = kv * ATT_GROUP + m
            order.extend(range(h * HEAD_DIM, (h + 1) * HEAD_DIM))
    return np.asarray(order)


def kernel(x, c, ctx, c_ctx, w_mod, b_mod, norm1, norm2, w_in, q_norm, k_norm, attn_sink,
           w_alpha_f, b_alpha_f, w_alpha_b, b_alpha_b, gla_norm, w_branch_attn, w_branch_gla,
           w_out, w_router, b_router, w_exp_in, b_exp_in, w_exp_out, b_exp_out):
    B, T, D = x.shape
    depth = w_mod.shape[0]
    assert depth == 1, "single-layer kernel: the context stream update only feeds later layers"
    l = 0
    perm = _head_perm()

    rows = ((B + 1 + 7) // 8) * 8
    c_all = jnp.zeros((rows, D), F32).at[:B].set(c).at[B].set(c_ctx)
    mod = _modulation(c_all, w_mod[l], b_mod[l])
    sh1, sc1, gt1, sh2, sc2, gt2 = [mod[:B, j * D:(j + 1) * D].reshape(B, 1, D) for j in range(6)]
    csh1, csc1 = [jnp.broadcast_to(mod[B, j * D:(j + 1) * D].reshape(1, 1, D), (B, 1, D)) for j in range(2)]

    offs = np.concatenate([[0], np.cumsum(IN_SPLITS)])
    cols = lambda j: w_in[l][:, offs[j]:offs[j + 1]]
    wal = jnp.zeros((2 * GLA_RANK, 2 * GLA_K_W), F32)
    wal = wal.at[:GLA_RANK, :GLA_K_W].set(w_alpha_f[l]).at[GLA_RANK:, GLA_K_W:].set(w_alpha_b[l])
    pw = {
        "wq": cols(0)[:, perm].astype(BF16), "wk": cols(1).astype(BF16), "wv": cols(2).astype(BF16),
        "wgq": cols(3).astype(BF16), "wgk": cols(4).astype(BF16), "wgv": cols(5).astype(BF16),
        "wgr": cols(6).astype(BF16), "wga": cols(9).astype(BF16), "wgg": cols(10).astype(BF16),
        "wlr": jnp.concatenate([cols(7), cols(8)], axis=1).astype(BF16),
        "qn": jnp.tile(q_norm[l], LANES // HEAD_DIM).reshape(1, LANES),
        "kn": jnp.tile(k_norm[l], LANES // HEAD_DIM).reshape(1, LANES),
        "wal": wal.astype(BF16),
        "bal": jnp.concatenate([b_alpha_f[l], b_alpha_b[l]]).reshape(1, 2 * GLA_K_W),
    }
    cos, sin = _rope_tables(T)
    n1 = norm1[l].reshape(1, D)
    tm = min(512, T)
    aq, ak, av, gq, gk, gv, gr, ga, gg, la = _inproj(
        x, sh1, sc1, n1, {"cos": cos, "sin": sin}, pw, rope=True, full=True, tm=tm)
    cak, cav, cgk, cgv, cla = _inproj(
        ctx, csh1, csc1, n1, None, pw, rope=False, full=False, tm=min(256, ctx.shape[1]))

    attn_o = _attention(attn_sink[l], aq, ak, av, cak, cav)
    gla_o = _gla(gq, gk, gv, la, cgk, cgv, cla)

    wr = jnp.zeros((D, LANES), F32).at[:, :N_EXPERTS].set(w_router[l])
    wrh = wr.astype(BF16)
    mw = {
        "gn": jnp.tile(gla_norm[l], GLA_HEADS).reshape(1, GLA_V_W),
        "wba": w_branch_attn[l][perm, :].astype(BF16), "wbg": w_branch_gla[l].astype(BF16),
        "wo": w_out[l].astype(BF16), "wrh": wrh, "wrl": (wr - wrh.astype(F32)).astype(BF16),
        "br": jnp.full((1, LANES), NEG, F32).at[0, :N_EXPERTS].set(b_router[l]),
    }
    xn, h2, ti, rk, tw, cnt = _merge(x, attn_o, gla_o, gr, ga, gg, gt1, sc2, sh2, norm2[l].reshape(1, D), mw, tm=tm)

    n = B * T
    counts = cnt[0, :N_EXPERTS]
    padded = (counts + MOE_STEP - 1) // MOE_STEP * MOE_STEP
    pad_end = jnp.cumsum(padded)
    pad_start = (pad_end - padded).astype(jnp.int32)
    zstart = jnp.maximum(pad_end - MOE_STEP, 0).astype(jnp.int32)
    cap = (n * TOP_K + N_EXPERTS * (MOE_STEP - 1)) // MOE_STEP * MOE_STEP
    n_blk = cap // MOE_STEP
    row0 = jnp.arange(n_blk, dtype=jnp.int32) * MOE_STEP
    blk_e = jnp.minimum(jnp.sum((pad_end[None, :] <= row0[:, None]).astype(jnp.int32), axis=1), N_EXPERTS - 1)
    onehot = (blk_e[:, None] == jnp.arange(N_EXPERTS, dtype=jnp.int32)[None, :]).astype(jnp.int32)
    valid_end = jnp.sum(onehot * (pad_start + counts)[None, :], axis=1)
    nv = jnp.clip(valid_end - row0, 0, MOE_STEP).astype(jnp.int32)
    n_used = (pad_end[-1] // MOE_STEP).astype(jnp.int32).reshape(1)
    changed = jnp.concatenate([jnp.ones((1,), bool), blk_e[1:] != blk_e[:-1]])
    first = (changed & (nv > 0)).astype(jnp.int32)
    parity = ((jnp.cumsum(first) - 1) % 2).astype(jnp.int32)
    eid = jnp.arange(N_EXPERTS, dtype=jnp.int32)
    later = (eid[None, :] > eid[:, None]) & (counts[None, :] > 0)
    nxt_of = jnp.where(jnp.any(later, axis=1), jnp.argmax(later, axis=1), -1).astype(jnp.int32)
    nxt_e = jnp.sum(onehot * nxt_of[None, :], axis=1).astype(jnp.int32)

    tb = min(256, T)
    dest_t, tw_t = _dest(ti.reshape(n, LANES), rk.reshape(n, LANES), tw.reshape(n, LANES), pad_start,
                         tm=min(2048, n))
    per_block = lambda a: a[:TOP_K].reshape(TOP_K, n // tb, tb).transpose(1, 2, 0).reshape(n // tb, 1, tb * TOP_K)
    dest4, tw4 = per_block(dest_t), per_block(tw_t)
    xs = _dispatch(dest4, zstart, n_used, h2.reshape(n, D), cap, tb=tb)
    ys = _experts(blk_e, nv, first, nxt_e, parity, xs, w_exp_in[l], b_exp_in[l], w_exp_out[l], b_exp_out[l])
    out = _combine(dest4, tw4, gt2.reshape(B, ROW_TILE, LANES), xn.reshape(n, D), ys, tb=tb, seq=T)
    return out.reshape(B, T, D)
```

```python
import functools

import numpy as np
import jax
import jax.numpy as jnp
from jax import lax
from jax.experimental import pallas as pl
from jax.experimental.pallas import tpu as pltpu

F32 = jnp.float32
BF16 = jnp.bfloat16

D_MODEL = 1024
GRID_W = 64
EPS = 1e-6
ATT_HEADS = 8
ATT_KV_HEADS = 2
ATT_GROUP = ATT_HEADS // ATT_KV_HEADS
HEAD_DIM = 64
WINDOW = 128
ATT_BLOCK = 128
ROPE_BASE = 10000.0
AXIS_ROT = HEAD_DIM // 2
GLA_HEADS = 4
GLA_DK = 64
GLA_DV = 128
GLA_RANK = 16
GLA_TAU = 16.0
N_EXPERTS = 32
TOP_K = 4
D_FF = D_MODEL
SWIGLU_ALPHA = 1.702
SWIGLU_LIMIT = 7.0
MOE_STEP = 512

ATT_W = ATT_HEADS * HEAD_DIM
ATT_KV_W = ATT_KV_HEADS * HEAD_DIM
GLA_K_W = GLA_HEADS * GLA_DK
GLA_V_W = GLA_HEADS * GLA_DV
IN_SPLITS = (ATT_W, ATT_KV_W, ATT_KV_W, GLA_K_W, GLA_K_W, GLA_V_W, GLA_V_W, GLA_RANK, GLA_RANK, D_MODEL, D_MODEL)

LANES = 128
ROW_TILE = D_MODEL // LANES
VMEM_LIMIT = 56 * 1024 * 1024
NEG = -1e30

GLA_C = 64
GLA_SUB = 4
GLA_LEVELS = 4


def _cparams(sem):
    return pltpu.CompilerParams(dimension_semantics=sem, vmem_limit_bytes=VMEM_LIMIT)


def _full(shape):
    n = len(shape)
    return pl.BlockSpec(shape, lambda *_: (0,) * n)


def _mod_kernel(c_ref, w_ref, b_ref, o_ref):
    c = c_ref[...]
    s = c * (1.0 / (1.0 + jnp.exp(-c)))
    o_ref[...] = jnp.dot(s, w_ref[...], preferred_element_type=F32,
                         precision=lax.Precision.HIGHEST) + b_ref[...]


def _modulation(c_all, w_mod, b_mod):
    rows = c_all.shape[0]
    n = w_mod.shape[1]
    tn = 1536
    return pl.pallas_call(
        _mod_kernel,
        out_shape=jax.ShapeDtypeStruct((rows, n), F32),
        grid=(n // tn,),
        in_specs=[pl.BlockSpec((rows, D_MODEL), lambda j: (0, 0)),
                  pl.BlockSpec((D_MODEL, tn), lambda j: (0, j)),
                  pl.BlockSpec((1, tn), lambda j: (0, j))],
        out_specs=pl.BlockSpec((rows, tn), lambda j: (0, j)),
        compiler_params=_cparams(("arbitrary",)),
        name="mod",
    )(c_all, w_mod, b_mod.reshape(1, n))


def _pair_norm(a, g, lo):
    s = a * a
    tot = jnp.sum(s, axis=-1, keepdims=True)
    slo = jnp.sum(jnp.where(lo, s, 0.0), axis=-1, keepdims=True)
    ms = jnp.where(lo, slo, tot - slo) * (1.0 / HEAD_DIM)
    return a * lax.rsqrt(ms + EPS) * g


def _rope(y, cos, sin, first):
    up = pltpu.roll(y, LANES - AXIS_ROT // 2, 1)
    dn = pltpu.roll(y, AXIS_ROT // 2, 1)
    return y * cos + jnp.where(first, up, dn) * sin


def _inproj_kernel(*refs, rope, full):
    if full:
        (x_ref, sh_ref, sc_ref, n1_ref, cos_ref, sin_ref, qn_ref, kn_ref, wal_ref, bal_ref,
         wq, wk, wv, wgq, wgk, wgv, wgr, wga, wgg, wlr,
         oq, ok, ov, ogq, ogk, ogv, ogr, oga, ogg, ola) = refs
    else:
        (x_ref, sh_ref, sc_ref, n1_ref, kn_ref, wal_ref, bal_ref,
         wk, wv, wgk, wgv, wlr,
         ok, ov, ogk, ogv, ola) = refs
    rows_total = x_ref.shape[0]
    nsplit = 2 if rows_total % 512 == 0 else 1
    tm = rows_total // nsplit
    lane = lax.broadcasted_iota(jnp.int32, (tm, LANES), 1)
    lo = lane < HEAD_DIM
    first = (lane % AXIS_ROT) < (AXIS_ROT // 2)
    for part in range(nsplit):
        r = slice(part * tm, (part + 1) * tm)
        x = x_ref[r, :]
        ms = jnp.mean(x * x, axis=-1, keepdims=True)
        h = (x * lax.rsqrt(ms + EPS) * n1_ref[...]) * (1.0 + sc_ref[...]) + sh_ref[...]
        hb = h.astype(BF16)

        def proj(w_ref):
            return jnp.dot(hb, w_ref[...], preferred_element_type=F32)

        if rope:
            cos = cos_ref[r, :]
            sin = sin_ref[r, :]

        k = _pair_norm(proj(wk), kn_ref[...], lo)
        if rope:
            k = _rope(k, cos, sin, first)
        ok[r, :] = k.astype(BF16)
        ov[r, :] = proj(wv).astype(BF16)
        ogk[r, :] = proj(wgk).astype(BF16)
        ogv[r, :] = proj(wgv).astype(BF16)
        lr = proj(wlr).astype(BF16)
        z = jnp.dot(lr, wal_ref[...], preferred_element_type=F32) + bal_ref[...]
        ola[r, :] = (jnp.minimum(z, 0.0) - jnp.log(1.0 + jnp.exp(-jnp.abs(z)))) * (1.0 / GLA_TAU)
        if full:
            q = proj(wq)
            for p in range(ATT_W // LANES):
                y = _pair_norm(q[:, p * LANES:(p + 1) * LANES], qn_ref[...], lo)
                if rope:
                    y = _rope(y, cos, sin, first)
                oq[r, p * LANES:(p + 1) * LANES] = (y * HEAD_DIM ** -0.5).astype(BF16)
            ogq[r, :] = (proj(wgq) * GLA_DK ** -0.5).astype(BF16)
            sigmoid = lambda t: 0.5 * jnp.tanh(0.5 * t) + 0.5
            g = proj(wgr)
            ogr[r, :] = (g * sigmoid(g)).astype(BF16)
            oga[r, :] = sigmoid(proj(wga)).astype(BF16)
            ogg[r, :] = sigmoid(proj(wgg)).astype(BF16)


def _inproj(x, sh, sc, norm1, tabs, wts, *, rope, full, tm):
    B, T, D = x.shape
    grid = (B, T // tm)
    row = lambda w: pl.BlockSpec((None, tm, w), lambda b, t: (b, t, 0))
    vec = pl.BlockSpec((None, 1, D), lambda b, t: (b, 0, 0))
    tab = pl.BlockSpec((tm, LANES), lambda b, t: (t, 0))
    if full:
        names = ("wq", "wk", "wv", "wgq", "wgk", "wgv", "wgr", "wga", "wgg", "wlr")
        ins = [x, sh, sc, norm1, tabs["cos"], tabs["sin"], wts["qn"], wts["kn"], wts["wal"], wts["bal"]]
        specs = [row(D), vec, vec, _full((1, D)), tab, tab, _full((1, LANES)), _full((1, LANES)),
                 _full(wts["wal"].shape), _full(wts["bal"].shape)]
        out_w = (ATT_W, ATT_KV_W, ATT_KV_W, GLA_K_W, GLA_K_W, GLA_V_W, GLA_V_W, D, D)
    else:
        names = ("wk", "wv", "wgk", "wgv", "wlr")
        ins = [x, sh, sc, norm1, wts["kn"], wts["wal"], wts["bal"]]
        specs = [row(D), vec, vec, _full((1, D)), _full((1, LANES)),
                 _full(wts["wal"].shape), _full(wts["bal"].shape)]
        out_w = (ATT_KV_W, ATT_KV_W, GLA_K_W, GLA_V_W)
    ins += [wts[n] for n in names]
    specs += [_full(wts[n].shape) for n in names]
    out_shape = [jax.ShapeDtypeStruct((B, T, w), BF16) for w in out_w]
    out_shape.append(jax.ShapeDtypeStruct((B, T, 2 * GLA_K_W), F32))
    out_specs = [row(w) for w in out_w] + [row(2 * GLA_K_W)]
    return pl.pallas_call(
        functools.partial(_inproj_kernel, rope=rope, full=full),
        out_shape=out_shape, grid=grid, in_specs=specs, out_specs=out_specs,
        compiler_params=_cparams(("parallel", "arbitrary")),
        name="inproj_full" if full else "inproj_ctx",
    )(*ins)


def _attn_kernel(*refs, seq, nsb):
    sink_ref, band_ref, q_ref = refs[:3]
    kblocks = refs[3:nsb + 5]
    kx_ref = refs[nsb + 5]
    vblocks = refs[nsb + 6:2 * nsb + 8]
    vx_ref, o_ref = refs[2 * nsb + 8:]
    n = pl.program_id(1)
    blk = ATT_BLOCK
    nb = seq // blk
    nslab = ATT_W // LANES
    rows = nslab * blk
    lane = lax.broadcasted_iota(jnp.int32, (blk, LANES), 1)
    lo = lane < HEAD_DIM
    hrow = lax.broadcasted_iota(jnp.int32, (rows, 1), 0) // blk
    band = band_ref[...]
    for sb in range(nsb):
        kcat = jnp.concatenate([r[...] for r in kblocks[sb:sb + 3]] + [kx_ref[...]], axis=0)
        vcat = jnp.concatenate([r[...] for r in vblocks[sb:sb + 3]] + [vx_ref[...]], axis=0)
        first = nsb * n + sb - 1
        q = q_ref[sb * blk:(sb + 1) * blk, :]
        outs = []
        for kv in range(ATT_KV_HEADS):
            keep = lo if kv == 0 else jnp.logical_not(lo)
            qs = jnp.concatenate([jnp.where(keep, q[:, m * LANES:(m + 1) * LANES], jnp.zeros((blk, LANES), BF16))
                                  for m in range(nslab)], axis=0)
            s = lax.dot_general(qs, kcat, (((1,), (1,)), ((), ())), preferred_element_type=F32) + band
            s = jnp.concatenate([jnp.where(first >= 0, s[:, :blk], NEG), s[:, blk:2 * blk],
                                 jnp.where(first + 2 < nb, s[:, 2 * blk:3 * blk], NEG), s[:, 3 * blk:]], axis=1)
            snk = jnp.zeros((rows, 1), F32)
            for m in range(nslab):
                snk = jnp.where(hrow == m, sink_ref[kv * ATT_GROUP + m], snk)
            mx = jnp.maximum(jnp.max(s, axis=-1, keepdims=True), snk)
            p = jnp.exp(s - mx)
            den = jnp.sum(p, axis=-1, keepdims=True) + jnp.exp(snk - mx)
            outs.append(jnp.dot(p.astype(BF16), vcat, preferred_element_type=F32) / den)
        for m in range(nslab):
            o_ref[sb * blk:(sb + 1) * blk, m * LANES:(m + 1) * LANES] = jnp.where(
                lo, outs[0][m * blk:(m + 1) * blk], outs[1][m * blk:(m + 1) * blk]).astype(BF16)


def _attention(sink, aq, ak, av, cak, cav):
    B, T, _ = aq.shape
    lc = cak.shape[1]
    blk = ATT_BLOCK
    nb = T // blk
    nsb = 8 if nb % 8 == 0 else (4 if nb % 4 == 0 else 2)
    assert nb % nsb == 0
    kvspec = lambda off: pl.BlockSpec((None, blk, ATT_KV_W),
                                      lambda b, n: (b, jnp.clip(nsb * n + off, 0, nb - 1), 0))
    cspec = pl.BlockSpec((None, lc, ATT_KV_W), lambda b, n: (b, 0, 0))
    qspec = pl.BlockSpec((None, nsb * blk, ATT_W), lambda b, n: (b, n, 0))
    kvs = [kvspec(off) for off in range(-1, nsb + 1)]
    rows = (ATT_W // LANES) * blk
    qi = np.arange(rows)[:, None] % blk
    kj = np.arange(3 * blk + lc)[None, :]
    band = jnp.asarray(np.where((np.abs(kj - blk - qi) <= WINDOW) | (kj >= 3 * blk), 0.0, NEG), F32)
    return pl.pallas_call(
        functools.partial(_attn_kernel, seq=T, nsb=nsb),
        out_shape=jax.ShapeDtypeStruct((B, T, ATT_W), BF16),
        grid=(B, nb // nsb),
        in_specs=[pl.BlockSpec(memory_space=pltpu.SMEM), _full(band.shape), qspec] + kvs + [cspec] + kvs + [cspec],
        out_specs=qspec,
        compiler_params=_cparams(("parallel", "arbitrary")),
        name="attn",
    )(sink, band, aq, *([ak] * (nsb + 2)), cak, *([av] * (nsb + 2)), cav)


def _gla_constants():
    C, sub, L = GLA_C, GLA_SUB, GLA_LEVELS
    i = np.arange(C)[:, None]
    t = np.arange(C)[None, :]
    tabs = [t <= i]
    rowq, same = [], []
    for l in range(L):
        s = C >> l
        mid = (i // s) * s + s // 2
        rowq.append(np.broadcast_to(i >= mid, (C, C)))
        same.append((i // s) == (t // s))
    shifts, dmask, dvalid = [], [t == i], []
    for d in range(1, sub):
        ok = (i % sub) >= d
        shifts.append(ok & (t == i - d))
        dmask.append(ok & (t == i - d))
        dvalid.append(np.broadcast_to(ok, (C, C)))
    flip = lambda a: a[::-1, ::-1]
    tile = lambda a: np.tile(a, (1, GLA_HEADS))

    def both(xs, lanes):
        f = (lambda a: tile(a)) if lanes else (lambda a: a)
        return np.stack([np.concatenate([f(a) for a in xs], 0),
                         np.concatenate([f(flip(a)) for a in xs], 0)]).astype(np.float32)

    hk = np.arange(GLA_K_W) // GLA_DK
    hv = np.arange(GLA_V_W) // GLA_DV
    ind = (hk[:, None] == hk[None, :]).astype(np.float32)
    bdv = (hk[:, None] == hv[None, :]).astype(np.float32)
    return (both(tabs, False), both(shifts, False), both(rowq, True), both(same, True),
            both(dmask, True), ind, bdv, np.ascontiguousarray(bdv.T), both(dvalid, True))


def _gla_chunk(q_b, k_b, v_b, la, cst, d):
    tri_ref, shm_ref, lv_ref, sm_ref, dm_ref, ind_ref, bdv_ref, hm_ref, dv_ref = cst
    C = GLA_C
    kw = GLA_K_W
    q = q_b.astype(F32)
    k = k_b.astype(F32)
    hi = la.astype(BF16)
    r1 = la - hi.astype(F32)
    mid = r1.astype(BF16)
    lo = (r1 - mid.astype(F32)).astype(BF16)
    b3 = jnp.dot(tri_ref[d], jnp.concatenate([hi, mid, lo], axis=1), preferred_element_type=F32)
    b = b3[:, :kw] + b3[:, kw:2 * kw] + b3[:, 2 * kw:]
    last = b[C - 1:C] if d == 0 else b[0:1]

    qt = (q * jnp.exp(b)).astype(BF16)
    kt = (k * jnp.exp(last - b)).astype(BF16)
    gamma = jnp.exp(last)

    ind = ind_ref[...]
    a = None
    for l in range(GLA_LEVELS):
        s = C >> l
        off = s // 2 - 1 if d == 0 else s // 2
        bref = jnp.concatenate([jnp.broadcast_to(b[st + off:st + off + 1], (s, kw)) for st in range(0, C, s)], axis=0)
        rq = lv_ref[d, l * C:(l + 1) * C, :]
        el = jnp.exp((b - bref) * (2.0 * rq - 1.0))
        qh = (q * (el * rq)).astype(BF16)
        kh = (k * (el * (1.0 - rq))).astype(BF16)
        bdk = jnp.concatenate([kh] * GLA_HEADS, axis=0) * ind
        al = lax.dot_general(qh, bdk, (((1,), (1,)), ((), ())), preferred_element_type=F32)
        a = al if l == 0 else a + al * sm_ref[d, l * C:(l + 1) * C, :]
    ksh = jnp.dot(shm_ref[d], k_b, preferred_element_type=F32)
    ps = [q * k]
    for j in range(1, GLA_SUB):
        bsh = pltpu.roll(b, j if d == 0 else C - j, 0)
        ej = jnp.exp((b - bsh) * dv_ref[d, (j - 1) * C:j * C, :])
        ps.append(q * ksh[(j - 1) * C:j * C] * ej)
    w = jnp.dot(jnp.concatenate(ps, axis=0).astype(BF16), ind, preferred_element_type=F32)
    for j in range(GLA_SUB):
        a = a + w[j * C:(j + 1) * C] * dm_ref[d, j * C:(j + 1) * C, :]

    bdv = jnp.concatenate([v_b] * GLA_HEADS, axis=0) * bdv_ref[...]
    o_intra = jnp.dot(a.astype(BF16), bdv, preferred_element_type=F32)
    upd = lax.dot_general(v_b, kt, (((0,), (0,)), ((), ())), preferred_element_type=F32) * hm_ref[...]
    return o_intra, qt, upd, gamma


def _gla_kernel(qf_ref, kf_ref, vf_ref, laf_ref, qb_ref, kb_ref, vb_ref, lab_ref,
                ckf_ref, cvf_ref, claf_ref, ckb_ref, cvb_ref, clab_ref,
                tri_ref, shm_ref, lv_ref, sm_ref, dm_ref, ind_ref, bdv_ref, hm_ref, dv_ref,
                of_ref, ob_ref, st_ref, *, n_ctx_steps):
    s = pl.program_id(1)
    C = GLA_C
    cst = (tri_ref, shm_ref, lv_ref, sm_ref, dm_ref, ind_ref, bdv_ref, hm_ref, dv_ref)

    @pl.when(s == 0)
    def _():
        st_ref[...] = jnp.zeros_like(st_ref)

    is_ctx = s < n_ctx_steps
    dirs = ((0, qf_ref, kf_ref, vf_ref, laf_ref, ckf_ref, cvf_ref, claf_ref, of_ref),
            (1, qb_ref, kb_ref, vb_ref, lab_ref, ckb_ref, cvb_ref, clab_ref, ob_ref))
    nbat = qf_ref.shape[0]
    states = [[st_ref[bi, 0], st_ref[bi, 1]] for bi in range(nbat)]
    for idx in range(2):
        for bi in range(nbat):
            for d, q_ref, k_ref, v_ref, la_ref, ck_ref, cv_ref, cla_ref, o_ref in dirs:
                c = idx if d == 0 else 1 - idx
                rows = slice(c * C, (c + 1) * C)
                k_b = jnp.where(is_ctx, ck_ref[bi, rows, :], k_ref[bi, rows, :])
                v_b = jnp.where(is_ctx, cv_ref[bi, rows, :], v_ref[bi, rows, :])
                la = jnp.where(is_ctx, cla_ref[bi, rows, :], la_ref[bi, rows, :])
                o_intra, qt, upd, gamma = _gla_chunk(q_ref[bi, rows, :], k_b, v_b, la, cst, d)
                st = states[bi][d]
                o_ref[bi, rows, :] = o_intra + lax.dot_general(qt, st.astype(BF16), (((1,), (1,)), ((), ())),
                                                               preferred_element_type=F32)
                states[bi][d] = st * gamma + upd
    for bi in range(nbat):
        st_ref[bi, 0] = states[bi][0]
        st_ref[bi, 1] = states[bi][1]


def _gla(gq, gk, gv, la, cgk, cgv, cla):
    B, T, _ = gq.shape
    lc = cgk.shape[1]
    R = 2 * GLA_C
    assert lc % R == 0 and T % R == 0
    n_ctx, n_lat = lc // R, T // R
    consts = _gla_constants()
    tri, shm = jnp.asarray(consts[0], BF16), jnp.asarray(consts[1], BF16)
    lv, sm, dm = [jnp.asarray(c) for c in consts[2:5]]
    ind, bdv = jnp.asarray(consts[5], BF16), jnp.asarray(consts[6], BF16)
    hm, dv = jnp.asarray(consts[7]), jnp.asarray(consts[8])

    def lat(s, d):
        j = jnp.maximum(s - n_ctx, 0)
        return j if d == 0 else n_lat - 1 - j

    def ctx(s, d):
        j = jnp.minimum(s, n_ctx - 1)
        return j if d == 0 else n_ctx - 1 - j

    nbat = 4 if B % 4 == 0 else (2 if B % 2 == 0 else 1)
    lspec = lambda w, d, c=0: pl.BlockSpec((nbat, R, w), lambda b, s: (b, lat(s, d), c))
    cspec = lambda w, d, c=0: pl.BlockSpec((nbat, R, w), lambda b, s: (b, ctx(s, d), c))
    lat_specs = lambda d: [lspec(GLA_K_W, d), lspec(GLA_K_W, d), lspec(GLA_V_W, d), lspec(GLA_K_W, d, d)]
    ctx_specs = lambda d: [cspec(GLA_K_W, d), cspec(GLA_V_W, d), cspec(GLA_K_W, d, d)]
    cs = [tri, shm, lv, sm, dm, ind, bdv, hm, dv]
    return pl.pallas_call(
        functools.partial(_gla_kernel, n_ctx_steps=n_ctx),
        out_shape=[jax.ShapeDtypeStruct((B, T, GLA_V_W), F32)] * 2,
        grid=(B // nbat, n_ctx + n_lat),
        in_specs=lat_specs(0) + lat_specs(1) + ctx_specs(0) + ctx_specs(1) + [_full(c.shape) for c in cs],
        out_specs=[lspec(GLA_V_W, 0), lspec(GLA_V_W, 1)],
        scratch_shapes=[pltpu.VMEM((nbat, 2, GLA_V_W, GLA_K_W), F32)],
        compiler_params=_cparams(("parallel", "arbitrary")),
        name="gla",
    )(gq, gk, gv, la, gq, gk, gv, la, cgk, cgv, cla, cgk, cgv, cla, *cs)


def _merge_kernel(x_ref, at_ref, of_ref, ob_ref, gr_ref, ga_ref, gg_ref, gt1_ref, sc2_ref, sh2_ref,
                  n2_ref, gn_ref, ltri_ref, wba_ref, wbg_ref, wo_ref, wrh_ref, wrl_ref, br_ref,
                  xn_ref, h2_ref, ti_ref, rk_ref, tw_ref, cnt_ref, carry_ref):
    tm = ltri_ref.shape[0]
    nsplit = x_ref.shape[0] // tm

    @pl.when((pl.program_id(0) == 0) & (pl.program_id(1) == 0))
    def _():
        carry_ref[...] = jnp.zeros_like(carry_ref)

    lane = lax.broadcasted_iota(jnp.int32, (tm, LANES), 1).astype(F32)
    carry = carry_ref[...]
    for part in range(nsplit):
        r = slice(part * tm, (part + 1) * tm)
        go = of_ref[r, :] + ob_ref[r, :]
        parts = []
        for h in range(GLA_HEADS):
            gh = go[:, h * GLA_DV:(h + 1) * GLA_DV]
            ms = jnp.mean(gh * gh, axis=-1, keepdims=True)
            parts.append(gh * lax.rsqrt(ms + EPS))
        o = jnp.concatenate(parts, axis=1) * gn_ref[...] * gr_ref[r, :].astype(F32)
        ya = jnp.dot(at_ref[r, :], wba_ref[...], preferred_element_type=F32)
        yg = jnp.dot(o.astype(BF16), wbg_ref[...], preferred_element_type=F32)
        y = ga_ref[r, :].astype(F32) * ya + gg_ref[r, :].astype(F32) * yg
        z = jnp.dot(y.astype(BF16), wo_ref[...], preferred_element_type=F32)
        xn = x_ref[r, :] + gt1_ref[...] * z
        xn_ref[r, :] = xn
        ms = jnp.mean(xn * xn, axis=-1, keepdims=True)
        h2 = (xn * lax.rsqrt(ms + EPS) * n2_ref[...]) * (1.0 + sc2_ref[...]) + sh2_ref[...]
        hh = h2.astype(BF16)
        hl = (h2 - hh.astype(F32)).astype(BF16)
        h2_ref[r, :] = h2
        logits = (jnp.dot(hh, wrh_ref[...], preferred_element_type=F32)
                  + jnp.dot(hl, wrh_ref[...], preferred_element_type=F32)
                  + jnp.dot(hh, wrl_ref[...], preferred_element_type=F32)) + br_ref[...]
        vals, idxs = [], []
        l = logits
        for _ in range(TOP_K):
            m = jnp.max(l, axis=-1, keepdims=True)
            ix = jnp.min(jnp.where(l == m, lane, float(LANES)), axis=-1, keepdims=True)
            vals.append(m)
            idxs.append(ix)
            l = jnp.where(lane == ix, -3.0e38, l)
        ex = [jnp.exp(v - vals[0]) for v in vals]
        den = ex[0] + ex[1] + ex[2] + ex[3]
        mh = jnp.zeros((tm, LANES), F32)
        for j in range(TOP_K):
            mh = mh + jnp.where(lane == idxs[j], 1.0, 0.0)
        pc = jnp.dot(ltri_ref[...], mh.astype(BF16), preferred_element_type=F32) + carry
        ti = jnp.zeros((tm, LANES), F32)
        rk = jnp.zeros((tm, LANES), F32)
        tw = jnp.zeros((tm, LANES), F32)
        for j in range(TOP_K):
            rj = jnp.sum(jnp.where(lane == idxs[j], pc, 0.0), axis=-1, keepdims=True)
            ti = jnp.where(lane == float(j), idxs[j], ti)
            rk = jnp.where(lane == float(j), rj, rk)
            tw = jnp.where(lane == float(j), ex[j] / den, tw)
        ti_ref[r, :] = ti.astype(jnp.int32)
        rk_ref[r, :] = rk.astype(jnp.int32)
        tw_ref[r, :] = tw
        carry = carry + jnp.sum(mh, axis=0, keepdims=True)
    carry_ref[...] = carry
    cnt_ref[...] = carry.astype(jnp.int32)


def _merge(x, attn_o, gla_o, gr, ga, gg, gt1, sc2, sh2, norm2, wts, *, tm):
    B, T, D = x.shape
    row = lambda w: pl.BlockSpec((None, tm, w), lambda b, t: (b, t, 0))
    vec = pl.BlockSpec((None, 1, D), lambda b, t: (b, 0, 0))
    names = ("wba", "wbg", "wo", "wrh", "wrl", "br")
    tp = tm
    ltri = jnp.asarray(np.tril(np.ones((tp, tp), np.float32), -1), BF16)
    return pl.pallas_call(
        _merge_kernel,
        out_shape=[jax.ShapeDtypeStruct((B, T, D), F32), jax.ShapeDtypeStruct((B, T, D), F32),
                   jax.ShapeDtypeStruct((B, T, LANES), jnp.int32), jax.ShapeDtypeStruct((B, T, LANES), jnp.int32),
                   jax.ShapeDtypeStruct((B, T, LANES), F32), jax.ShapeDtypeStruct((1, LANES), jnp.int32)],
        grid=(B, T // tm),
        in_specs=[row(D), row(ATT_W), row(GLA_V_W), row(GLA_V_W),
                  row(GLA_V_W), row(D), row(D), vec, vec, vec,
                  _full((1, D)), _full((1, GLA_V_W)), _full((tp, tp))] + [_full(wts[n].shape) for n in names],
        out_specs=[row(D), row(D), row(LANES), row(LANES), row(LANES), _full((1, LANES))],
        scratch_shapes=[pltpu.VMEM((1, LANES), F32)],
        compiler_params=_cparams(("arbitrary", "arbitrary")),
        name="merge",
    )(x, attn_o, gla_o[0], gla_o[1], gr, ga, gg, gt1, sc2, sh2, norm2, wts["gn"], ltri, *[wts[n] for n in names])


def _dest_kernel(ti_ref, rk_ref, tw_ref, ps_ref, o_ref, w_ref):
    tm = ti_ref.shape[0]
    lane = lax.broadcasted_iota(jnp.int32, (tm, LANES), 1)
    ti = ti_ref[...]
    ps = ps_ref[...].astype(F32)
    out = jnp.where(lane < TOP_K, rk_ref[...], 0).astype(F32)
    for k in range(TOP_K):
        start = jnp.sum(jnp.where(lane == ti[:, k:k + 1], ps, 0.0), axis=-1, keepdims=True)
        out = out + jnp.where(lane == k, start, 0.0)
    o_ref[...] = jnp.transpose(out)[0:8, :].astype(jnp.int32)
    w_ref[...] = jnp.transpose(tw_ref[...])[0:8, :]


def _dest(ti, rk, tw, pad_start, *, tm):
    n = ti.shape[0]
    ps = jnp.zeros((1, LANES), jnp.int32).at[0, :N_EXPERTS].set(pad_start)
    row = pl.BlockSpec((tm, LANES), lambda i: (i, 0))
    col = pl.BlockSpec((8, tm), lambda i: (0, i))
    return pl.pallas_call(
        _dest_kernel,
        out_shape=[jax.ShapeDtypeStruct((8, n), jnp.int32), jax.ShapeDtypeStruct((8, n), F32)],
        grid=(n // tm,), in_specs=[row, row, row, _full((1, LANES))], out_specs=[col, col],
        compiler_params=_cparams(("arbitrary",)), name="dest",
    )(ti, rk, tw, ps)


def _dispatch_kernel(d_ref, zs_ref, nu_ref, h_ref, xs_ref, buf, zbuf, isem, sem, zsem,
                     *, tb, nsteps):
    s = pl.program_id(0)
    nblk = xs_ref.shape[0] // MOE_STEP
    slot = s % 3
    nxt = (s + 1) % 3

    def loads(step, sl):
        r0 = pl.multiple_of(step * tb, tb)
        return [pltpu.make_async_copy(h_ref.at[pl.ds(r0, tb), pl.ds(j * LANES, LANES)], buf.at[sl, :, j, :],
                                      isem.at[sl]) for j in range(ROW_TILE)]

    def wait_rows(sl):
        for _ in range(TOP_K):
            pltpu.make_async_copy(buf.at[sl], xs_ref.at[pl.ds(0, tb)], sem.at[sl]).wait()

    @pl.when(s == 0)
    def _():
        zbuf[...] = jnp.zeros_like(zbuf)

        def zstart(e, c):
            z0 = pl.multiple_of(zs_ref[e], MOE_STEP)
            pltpu.make_async_copy(zbuf, xs_ref.at[pl.ds(z0, MOE_STEP)], zsem).start()
            return c

        def zwait(e, c):
            pltpu.make_async_copy(zbuf, xs_ref.at[pl.ds(0, MOE_STEP)], zsem).wait()
            return c

        lax.fori_loop(0, N_EXPERTS, zstart, 0)
        lax.fori_loop(0, N_EXPERTS, zwait, 0)

        def tstart(j, c):
            pltpu.make_async_copy(zbuf, xs_ref.at[pl.ds(pl.multiple_of(j * MOE_STEP, MOE_STEP), MOE_STEP)],
                                  zsem).start()
            return c

        lax.fori_loop(nu_ref[0], nblk, tstart, 0)
        lax.fori_loop(nu_ref[0], nblk, zwait, 0)
        for c in loads(0, 0):
            c.start()

    for c in loads(s, slot):
        c.wait()

    @pl.when(s + 1 < nsteps)
    def _():
        @pl.when(s >= 2)
        def _():
            wait_rows(nxt)
        for c in loads(s + 1, nxt):
            c.start()

    def issue(r, c):
        for k in range(TOP_K):
            d = d_ref[0, r * TOP_K + k]
            pltpu.make_async_copy(buf.at[slot, r], xs_ref.at[d], sem.at[slot]).start(priority=k % 2)
        return c

    lax.fori_loop(0, tb, issue, 0, unroll=8)

    @pl.when(s == nsteps - 1)
    def _():
        wait_rows(slot)
        if nsteps >= 2:
            wait_rows((s + 2) % 3)
        if nsteps >= 3:
            wait_rows(nxt)


def _dispatch(dest4, zstart, n_used, h2, cap, *, tb):
    n, D = h2.shape
    nsteps = n // tb
    idx = pl.BlockSpec((None, 1, tb * TOP_K), lambda s: (s, 0, 0), memory_space=pltpu.SMEM)
    smem = pl.BlockSpec(memory_space=pltpu.SMEM)
    anyspec = pl.BlockSpec(memory_space=pl.ANY)
    return pl.pallas_call(
        functools.partial(_dispatch_kernel, tb=tb, nsteps=nsteps),
        out_shape=jax.ShapeDtypeStruct((cap, ROW_TILE, LANES), F32),
        grid=(nsteps,),
        in_specs=[idx, smem, smem, anyspec],
        out_specs=anyspec,
        scratch_shapes=[pltpu.VMEM((3, tb, ROW_TILE, LANES), F32), pltpu.VMEM((MOE_STEP, ROW_TILE, LANES), F32),
                        pltpu.SemaphoreType.DMA((3,)), pltpu.SemaphoreType.DMA((3,)), pltpu.SemaphoreType.DMA(())],
        compiler_params=_cparams(("arbitrary",)),
        name="dispatch",
    )(dest4, zstart, n_used, h2)


def _expert_kernel(be_ref, nv_ref, fs_ref, nx_ref, pr_ref, xs_ref, w1_ref, b1_ref, w2_ref, b2_ref, ys_ref,
                   w1b, w2b, w1f, w2f, xin, yout, isem, osem, wsem, *, nsteps):
    i = pl.program_id(0)
    slot = i % 2

    def wloads(ex, sl):
        return [pltpu.make_async_copy(w1_ref.at[ex], w1f.at[sl], wsem.at[sl]),
                pltpu.make_async_copy(w2_ref.at[ex], w2f.at[sl], wsem.at[sl])]

    def loads(step, sl):
        r0 = pl.multiple_of(step * MOE_STEP, MOE_STEP)
        return [pltpu.make_async_copy(xs_ref.at[pl.ds(r0, MOE_STEP), j, :],
                                      xin.at[sl, :, pl.ds(j * LANES, LANES)], isem.at[sl]) for j in range(ROW_TILE)]

    def stores(step, sl):
        r0 = pl.multiple_of(step * MOE_STEP, MOE_STEP)
        return [pltpu.make_async_copy(yout.at[sl, :, pl.ds(j * LANES, LANES)],
                                      ys_ref.at[pl.ds(r0, MOE_STEP), j, :], osem.at[sl]) for j in range(ROW_TILE)]

    @pl.when(i == 0)
    def _():
        for c in loads(0, 0):
            c.start()
        for c in wloads(be_ref[0], 0):
            c.start()

    @pl.when(fs_ref[i] == 1)
    def _():
        par = pr_ref[i]
        for c in wloads(be_ref[i], par):
            c.wait()
        w1b[...] = w1f[par].astype(BF16)
        w2b[...] = w2f[par].astype(BF16)

        @pl.when(nx_ref[i] >= 0)
        def _():
            for c in wloads(nx_ref[i], 1 - par):
                c.start(priority=1)

    for c in loads(i, slot):
        c.wait()

    @pl.when(i + 1 < nsteps)
    def _():
        for c in loads(i + 1, 1 - slot):
            c.start()

    @pl.when(i >= 2)
    def _():
        for c in stores(i - 2, slot):
            c.wait()

    def mlp(rows):
        xb = xin[slot, 0:rows, :].astype(BF16)
        y = jnp.zeros((rows, D_MODEL), F32)
        fh = D_FF // 2
        for h in range(2):
            g = jnp.dot(xb, w1b[:, h * fh:(h + 1) * fh], preferred_element_type=F32) + b1_ref[:, h * fh:(h + 1) * fh]
            u = (jnp.dot(xb, w1b[:, D_FF + h * fh:D_FF + (h + 1) * fh], preferred_element_type=F32)
                 + b1_ref[:, D_FF + h * fh:D_FF + (h + 1) * fh])
            gate = jnp.minimum(g, SWIGLU_LIMIT)
            up = jnp.clip(u, -SWIGLU_LIMIT, SWIGLU_LIMIT)
            act = gate * (1.0 / (1.0 + jnp.exp(-SWIGLU_ALPHA * gate))) * (up + 1.0)
            y = y + jnp.dot(act.astype(BF16), w2b[h * fh:(h + 1) * fh, :], preferred_element_type=F32)
        yout[slot, 0:rows, :] = y + b2_ref[...]

    nv = nv_ref[i]
    quarter = MOE_STEP // 4
    for j in range(1, 5):
        rows = j * quarter

        @pl.when((nv > rows - quarter) & (nv <= rows))
        def _():
            mlp(rows)
            if rows < MOE_STEP:
                yout[slot, rows:, :] = jnp.zeros((MOE_STEP - rows, D_MODEL), F32)

    @pl.when(nv == 0)
    def _():
        yout[slot] = jnp.zeros((MOE_STEP, D_MODEL), F32)

    for c in stores(i, slot):
        c.start()

    @pl.when(i == nsteps - 1)
    def _():
        for c in stores(i, slot):
            c.wait()
        if nsteps >= 2:
            for c in stores(i - 1, 1 - slot):
                c.wait()


def _experts(blk_e, nv, first, nxt_e, parity, xs, w1, b1, w2, b2):
    cap = xs.shape[0]
    n_blk = cap // MOE_STEP
    ne = w1.shape[0]
    anyspec = pl.BlockSpec(memory_space=pl.ANY)
    bias = lambda w: pl.BlockSpec((None, 1, w), lambda i, be, *_: (be[i], 0, 0))
    gs = pltpu.PrefetchScalarGridSpec(
        num_scalar_prefetch=5, grid=(n_blk,),
        in_specs=[anyspec, anyspec, bias(2 * D_FF), anyspec, bias(D_MODEL)],
        out_specs=anyspec,
        scratch_shapes=[pltpu.VMEM((D_MODEL, 2 * D_FF), BF16), pltpu.VMEM((D_FF, D_MODEL), BF16),
                        pltpu.VMEM((2, D_MODEL, 2 * D_FF), F32), pltpu.VMEM((2, D_FF, D_MODEL), F32),
                        pltpu.VMEM((2, MOE_STEP, D_MODEL), F32), pltpu.VMEM((2, MOE_STEP, D_MODEL), F32),
                        pltpu.SemaphoreType.DMA((2,)), pltpu.SemaphoreType.DMA((2,)),
                        pltpu.SemaphoreType.DMA((2,))])
    return pl.pallas_call(
        functools.partial(_expert_kernel, nsteps=n_blk), grid_spec=gs,
        out_shape=jax.ShapeDtypeStruct((cap, ROW_TILE, LANES), F32),
        compiler_params=_cparams(("arbitrary",)),
        name="experts",
    )(blk_e, nv, first, nxt_e, parity, xs, w1, b1.reshape(ne, 1, 2 * D_FF), w2, b2.reshape(ne, 1, D_MODEL))


def _combine_kernel(dc_ref, dn_ref, tw_ref, gt2_ref, xn_ref, ys_ref, o_ref,
                    gbuf, xt, ot, gsem, xsem, osem, *, tb, nsteps):
    s = pl.program_id(0)
    slot = s % 2
    other = 1 - slot

    def xloads(step, sl):
        r0 = pl.multiple_of(step * tb, tb)
        return [pltpu.make_async_copy(xn_ref.at[pl.ds(r0, tb), pl.ds(j * LANES, LANES)], xt.at[sl, :, j, :],
                                      xsem.at[sl]) for j in range(ROW_TILE)]

    def ostores(step, sl):
        r0 = pl.multiple_of(step * tb, tb)
        return [pltpu.make_async_copy(ot.at[sl, :, j, :], o_ref.at[pl.ds(r0, tb), pl.ds(j * LANES, LANES)],
                                      osem.at[sl]) for j in range(ROW_TILE)]

    def gather_row(d_ref, sl, r):
        for k in range(TOP_K):
            d = d_ref[0, r * TOP_K + k]
            pltpu.make_async_copy(ys_ref.at[d], gbuf.at[sl, k, r], gsem.at[sl]).start(priority=k % 2)

    def wait_gathers(sl):
        for k in range(TOP_K):
            pltpu.make_async_copy(ys_ref.at[pl.ds(0, tb)], gbuf.at[sl, k], gsem.at[sl]).wait()

    @pl.when(s == 0)
    def _():
        def issue(r, c):
            gather_row(dc_ref, 0, r)
            return c

        lax.fori_loop(0, tb, issue, 0, unroll=8)
        for c in xloads(0, 0):
            c.start()

    wait_gathers(slot)
    for c in xloads(s, slot):
        c.wait()

    @pl.when(s >= 2)
    def _():
        for c in ostores(s - 2, slot):
            c.wait()

    @pl.when(s + 1 < nsteps)
    def _():
        def issue(r, c):
            gather_row(dn_ref, other, r)
            return c

        lax.fori_loop(0, tb, issue, 0, unroll=8)
        for c in xloads(s + 1, other):
            c.start()

    g2 = gt2_ref[...]

    def wsum(r, c):
        acc = tw_ref[0, r * TOP_K] * gbuf[slot, 0, r]
        for k in range(1, TOP_K):
            acc = acc + tw_ref[0, r * TOP_K + k] * gbuf[slot, k, r]
        ot[slot, r] = xt[slot, r] + g2 * acc
        return c

    lax.fori_loop(0, tb, wsum, 0, unroll=8)
    for c in ostores(s, slot):
        c.start()

    @pl.when(s == nsteps - 1)
    def _():
        for c in ostores(s, slot):
            c.wait()
        if nsteps >= 2:
            for c in ostores(s - 1, other):
                c.wait()


def _combine(dest4, tw4, gt2t, xn, ys, *, tb, seq):
    n, D = xn.shape
    nsteps = n // tb
    cur = lambda s: (s, 0, 0)
    nxt = lambda s: (jnp.minimum(s + 1, nsteps - 1), 0, 0)
    idx = lambda f: pl.BlockSpec((None, 1, tb * TOP_K), f, memory_space=pltpu.SMEM)
    anyspec = pl.BlockSpec(memory_space=pl.ANY)
    tile = (tb, ROW_TILE, LANES)
    return pl.pallas_call(
        functools.partial(_combine_kernel, tb=tb, nsteps=nsteps),
        out_shape=jax.ShapeDtypeStruct((n, D), F32),
        grid=(nsteps,),
        in_specs=[idx(cur), idx(nxt), idx(cur),
                  pl.BlockSpec((None, ROW_TILE, LANES), lambda s: ((s * tb) // seq, 0, 0)), anyspec, anyspec],
        out_specs=anyspec,
        scratch_shapes=[pltpu.VMEM((2, TOP_K) + tile, F32), pltpu.VMEM((2,) + tile, F32),
                        pltpu.VMEM((2,) + tile, F32), pltpu.SemaphoreType.DMA((2,)),
                        pltpu.SemaphoreType.DMA((2,)), pltpu.SemaphoreType.DMA((2,))],
        compiler_params=_cparams(("arbitrary",)),
        name="combine",
    )(dest4, dest4, tw4, gt2t, xn, ys)


def _rope_tables(T):
    rows = T // GRID_W
    row = jnp.repeat(jnp.arange(rows, dtype=F32), GRID_W)
    col = jnp.tile(jnp.arange(GRID_W, dtype=F32), rows)
    inv = ROPE_BASE ** (-jnp.arange(0, AXIS_ROT, 2, dtype=F32) / AXIS_ROT)
    ang_r, ang_c = row[:, None] * inv, col[:, None] * inv
    m = AXIS_ROT // 2
    ang = jnp.concatenate([ang_r, ang_r, ang_c, ang_c], axis=1)
    sign = jnp.tile(jnp.concatenate([-jnp.ones((m,), F32), jnp.ones((m,), F32)]), 2)
    cos = jnp.tile(jnp.cos(ang), (1, LANES // HEAD_DIM))
    sin = jnp.tile(jnp.sin(ang) * sign, (1, LANES // HEAD_DIM))
    return cos, sin


def _head_perm():
    order = []
    for m in range(ATT_GROUP):
        for kv in range(ATT_KV_HEADS):
            h = kv * ATT_GROUP + m
            order.extend(range(h * HEAD_DIM, (h + 1) * HEAD_DIM))
    return np.asarray(order)


def kernel(x, c, ctx, c_ctx, w_mod, b_mod, norm1, norm2, w_in, q_norm, k_norm, attn_sink,
           w_alpha_f, b_alpha_f, w_alpha_b, b_alpha_b, gla_norm, w_branch_attn, w_branch_gla,
           w_out, w_router, b_router, w_exp_in, b_exp_in, w_exp_out, b_exp_out):
    B, T, D = x.shape
    depth = w_mod.shape[0]
    assert depth == 1, "single-layer kernel: the context stream update only feeds later layers"
    l = 0
    perm = _head_perm()

    rows = ((B + 1 + 7) // 8) * 8
    c_all = jnp.zeros((rows, D), F32).at[:B].set(c).at[B].set(c_ctx)
    mod = _modulation(c_all, w_mod[l], b_mod[l])
    sh1, sc1, gt1, sh2, sc2, gt2 = [mod[:B, j * D:(j + 1) * D].reshape(B, 1, D) for j in range(6)]
    csh1, csc1 = [jnp.broadcast_to(mod[B, j * D:(j + 1) * D].reshape(1, 1, D), (B, 1, D)) for j in range(2)]

    offs = np.concatenate([[0], np.cumsum(IN_SPLITS)])
    cols = lambda j: w_in[l][:, offs[j]:offs[j + 1]]
    wal = jnp.zeros((2 * GLA_RANK, 2 * GLA_K_W), F32)
    wal = wal.at[:GLA_RANK, :GLA_K_W].set(w_alpha_f[l]).at[GLA_RANK:, GLA_K_W:].set(w_alpha_b[l])
    pw = {
        "wq": cols(0)[:, perm].astype(BF16), "wk": cols(1).astype(BF16), "wv": cols(2).astype(BF16),
        "wgq": cols(3).astype(BF16), "wgk": cols(4).astype(BF16), "wgv": cols(5).astype(BF16),
        "wgr": cols(6).astype(BF16), "wga": cols(9).astype(BF16), "wgg": cols(10).astype(BF16),
        "wlr": jnp.concatenate([cols(7), cols(8)], axis=1).astype(BF16),
        "qn": jnp.tile(q_norm[l], LANES // HEAD_DIM).reshape(1, LANES),
        "kn": jnp.tile(k_norm[l], LANES // HEAD_DIM).reshape(1, LANES),
        "wal": wal.astype(BF16),
        "bal": jnp.concatenate([b_alpha_f[l], b_alpha_b[l]]).reshape(1, 2 * GLA_K_W),
    }
    cos, sin = _rope_tables(T)
    n1 = norm1[l].reshape(1, D)
    tm = min(512, T)
    aq, ak, av, gq, gk, gv, gr, ga, gg, la = _inproj(
        x, sh1, sc1, n1, {"cos": cos, "sin": sin}, pw, rope=True, full=True, tm=tm)
    cak, cav, cgk, cgv, cla = _inproj(
        ctx, csh1, csc1, n1, None, pw, rope=False, full=False, tm=min(256, ctx.shape[1]))

    attn_o = _attention(attn_sink[l], aq, ak, av, cak, cav)
    gla_o = _gla(gq, gk, gv, la, cgk, cgv, cla)

    wr = jnp.zeros((D, LANES), F32).at[:, :N_EXPERTS].set(w_router[l])
    wrh = wr.astype(BF16)
    mw = {
        "gn": jnp.tile(gla_norm[l], GLA_HEADS).reshape(1, GLA_V_W),
        "wba": w_branch_attn[l][perm, :].astype(BF16), "wbg": w_branch_gla[l].astype(BF16),
        "wo": w_out[l].astype(BF16), "wrh": wrh, "wrl": (wr - wrh.astype(F32)).astype(BF16),
        "br": jnp.full((1, LANES), NEG, F32).at[0, :N_EXPERTS].set(b_router[l]),
    }
    xn, h2, ti, rk, tw, cnt = _merge(x, attn_o, gla_o, gr, ga, gg, gt1, sc2, sh2, norm2[l].reshape(1, D), mw, tm=tm)

    n = B * T
    counts = cnt[0, :N_EXPERTS]
    padded = (counts + MOE_STEP - 1) // MOE_STEP * MOE_STEP
    pad_end = jnp.cumsum(padded)
    pad_start = (pad_end - padded).astype(jnp.int32)
    zstart = jnp.maximum(pad_end - MOE_STEP, 0).astype(jnp.int32)
    cap = (n * TOP_K + N_EXPERTS * (MOE_STEP - 1)) // MOE_STEP * MOE_STEP
    n_blk = cap // MOE_STEP
    row0 = jnp.arange(n_blk, dtype=jnp.int32) * MOE_STEP
    blk_e = jnp.minimum(jnp.sum((pad_end[None, :] <= row0[:, None]).astype(jnp.int32), axis=1), N_EXPERTS - 1)
    onehot = (blk_e[:, None] == jnp.arange(N_EXPERTS, dtype=jnp.int32)[None, :]).astype(jnp.int32)
    valid_end = jnp.sum(onehot * (pad_start + counts)[None, :], axis=1)
    nv = jnp.clip(valid_end - row0, 0, MOE_STEP).astype(jnp.int32)
    n_used = (pad_end[-1] // MOE_STEP).astype(jnp.int32).reshape(1)
    changed = jnp.concatenate([jnp.ones((1,), bool), blk_e[1:] != blk_e[:-1]])
    first = (changed & (nv > 0)).astype(jnp.int32)
    parity = ((jnp.cumsum(first) - 1) % 2).astype(jnp.int32)
    eid = jnp.arange(N_EXPERTS, dtype=jnp.int32)
    later = (eid[None, :] > eid[:, None]) & (counts[None, :] > 0)
    nxt_of = jnp.where(jnp.any(later, axis=1), jnp.argmax(later, axis=1), -1).astype(jnp.int32)
    nxt_e = jnp.sum(onehot * nxt_of[None, :], axis=1).astype(jnp.int32)

    tb = min(256, T)
    dest_t, tw_t = _dest(ti.reshape(n, LANES), rk.reshape(n, LANES), tw.reshape(n, LANES), pad_start,
                         tm=min(2048, n))
    per_block = lambda a: a[:TOP_K].reshape(TOP_K, n // tb, tb).transpose(1, 2, 0).reshape(n // tb, 1, tb * TOP_K)
    dest4, tw4 = per_block(dest_t), per_block(tw_t)
    xs = _dispatch(dest4, zstart, n_used, h2.reshape(n, D), cap, tb=tb)
    ys = _experts(blk_e, nv, first, nxt_e, parity, xs, w_exp_in[l], b_exp_in[l], w_exp_out[l], b_exp_out[l])
    out = _combine(dest4, tw4, gt2.reshape(B, ROW_TILE, LANES), xn.reshape(n, D), ys, tb=tb, seq=T)
    return out.reshape(B, T, D)
```

```python
import functools

import numpy as np
import jax
import jax.numpy as jnp
from jax import lax
from jax.experimental import pallas as pl
from jax.experimental.pallas import tpu as pltpu

F32 = jnp.float32
BF16 = jnp.bfloat16

D_MODEL = 1024
GRID_W = 64
EPS = 1e-6
ATT_HEADS = 8
ATT_KV_HEADS = 2
ATT_GROUP = ATT_HEADS // ATT_KV_HEADS
HEAD_DIM = 64
WINDOW = 128
ATT_BLOCK = 128
ROPE_BASE = 10000.0
AXIS_ROT = HEAD_DIM // 2
GLA_HEADS = 4
GLA_DK = 64
GLA_DV = 128
GLA_RANK = 16
GLA_TAU = 16.0
N_EXPERTS = 32
TOP_K = 4
D_FF = D_MODEL
SWIGLU_ALPHA = 1.702
SWIGLU_LIMIT = 7.0
MOE_STEP = 512

ATT_W = ATT_HEADS * HEAD_DIM
ATT_KV_W = ATT_KV_HEADS * HEAD_DIM
GLA_K_W = GLA_HEADS * GLA_DK
GLA_V_W = GLA_HEADS * GLA_DV
IN_SPLITS = (ATT_W, ATT_KV_W, ATT_KV_W, GLA_K_W, GLA_K_W, GLA_V_W, GLA_V_W, GLA_RANK, GLA_RANK, D_MODEL, D_MODEL)

LANES = 128
ROW_TILE = D_MODEL // LANES
VMEM_LIMIT = 56 * 1024 * 1024
NEG = -1e30

GLA_C = 64
GLA_SUB = 4
GLA_LEVELS = 4


def _cparams(sem):
    return pltpu.CompilerParams(dimension_semantics=sem, vmem_limit_bytes=VMEM_LIMIT)


def _full(shape):
    n = len(shape)
    return pl.BlockSpec(shape, lambda *_: (0,) * n)


def _mod_kernel(c_ref, w_ref, b_ref, o_ref):
    c = c_ref[...]
    s = c * (1.0 / (1.0 + jnp.exp(-c)))
    o_ref[...] = jnp.dot(s, w_ref[...], preferred_element_type=F32,
                         precision=lax.Precision.HIGHEST) + b_ref[...]


def _modulation(c_all, w_mod, b_mod):
    rows = c_all.shape[0]
    n = w_mod.shape[1]
    tn = 1536
    return pl.pallas_call(
        _mod_kernel,
        out_shape=jax.ShapeDtypeStruct((rows, n), F32),
        grid=(n // tn,),
        in_specs=[pl.BlockSpec((rows, D_MODEL), lambda j: (0, 0)),
                  pl.BlockSpec((D_MODEL, tn), lambda j: (0, j)),
                  pl.BlockSpec((1, tn), lambda j: (0, j))],
        out_specs=pl.BlockSpec((rows, tn), lambda j: (0, j)),
        compiler_params=_cparams(("arbitrary",)),
        name="mod",
    )(c_all, w_mod, b_mod.reshape(1, n))


def _pair_norm(a, g, lo):
    s = a * a
    tot = jnp.sum(s, axis=-1, keepdims=True)
    slo = jnp.sum(jnp.where(lo, s, 0.0), axis=-1, keepdims=True)
    ms = jnp.where(lo, slo, tot - slo) * (1.0 / HEAD_DIM)
    return a * lax.rsqrt(ms + EPS) * g


def _rope(y, cos, sin, first):
    up = pltpu.roll(y, LANES - AXIS_ROT // 2, 1)
    dn = pltpu.roll(y, AXIS_ROT // 2, 1)
    return y * cos + jnp.where(first, up, dn) * sin


def _inproj_kernel(*refs, rope, full):
    if full:
        (x_ref, sh_ref, sc_ref, n1_ref, cos_ref, sin_ref, qn_ref, kn_ref, wal_ref, bal_ref,
         wq, wk, wv, wgq, wgk, wgv, wgr, wga, wgg, wlr,
         oq, ok, ov, ogq, ogk, ogv, ogr, oga, ogg, ola) = refs
    else:
        (x_ref, sh_ref, sc_ref, n1_ref, kn_ref, wal_ref, bal_ref,
         wk, wv, wgk, wgv, wlr,
         ok, ov, ogk, ogv, ola) = refs
    rows_total = x_ref.shape[0]
    nsplit = 2 if rows_total % 512 == 0 else 1
    tm = rows_total // nsplit
    lane = lax.broadcasted_iota(jnp.int32, (tm, LANES), 1)
    lo = lane < HEAD_DIM
    first = (lane % AXIS_ROT) < (AXIS_ROT // 2)
    for part in range(nsplit):
        r = slice(part * tm, (part + 1) * tm)
        x = x_ref[r, :]
        ms = jnp.mean(x * x, axis=-1, keepdims=True)
        h = (x * lax.rsqrt(ms + EPS) * n1_ref[...]) * (1.0 + sc_ref[...]) + sh_ref[...]
        hb = h.astype(BF16)

        def proj(w_ref):
            return jnp.dot(hb, w_ref[...], preferred_element_type=F32)

        if rope:
            cos = cos_ref[r, :]
            sin = sin_ref[r, :]

        k = _pair_norm(proj(wk), kn_ref[...], lo)
        if rope:
            k = _rope(k, cos, sin, first)
        ok[r, :] = k.astype(BF16)
        ov[r, :] = proj(wv).astype(BF16)
        ogk[r, :] = proj(wgk).astype(BF16)
        ogv[r, :] = proj(wgv).astype(BF16)
        lr = proj(wlr).astype(BF16)
        z = jnp.dot(lr, wal_ref[...], preferred_element_type=F32) + bal_ref[...]
        ola[r, :] = (jnp.minimum(z, 0.0) - jnp.log(1.0 + jnp.exp(-jnp.abs(z)))) * (1.0 / GLA_TAU)
        if full:
            q = proj(wq)
            for p in range(ATT_W // LANES):
                y = _pair_norm(q[:, p * LANES:(p + 1) * LANES], qn_ref[...], lo)
                if rope:
                    y = _rope(y, cos, sin, first)
                oq[r, p * LANES:(p + 1) * LANES] = (y * HEAD_DIM ** -0.5).astype(BF16)
            ogq[r, :] = (proj(wgq) * GLA_DK ** -0.5).astype(BF16)
            sigmoid = lambda t: 0.5 * jnp.tanh(0.5 * t) + 0.5
            g = proj(wgr)
            ogr[r, :] = (g * sigmoid(g)).astype(BF16)
            oga[r, :] = sigmoid(proj(wga)).astype(BF16)
            ogg[r, :] = sigmoid(proj(wgg)).astype(BF16)


def _inproj(x, sh, sc, norm1, tabs, wts, *, rope, full, tm):
    B, T, D = x.shape
    grid = (B, T // tm)
    row = lambda w: pl.BlockSpec((None, tm, w), lambda b, t: (b, t, 0))
    vec = pl.BlockSpec((None, 1, D), lambda b, t: (b, 0, 0))
    tab = pl.BlockSpec((tm, LANES), lambda b, t: (t, 0))
    if full:
        names = ("wq", "wk", "wv", "wgq", "wgk", "wgv", "wgr", "wga", "wgg", "wlr")
        ins = [x, sh, sc, norm1, tabs["cos"], tabs["sin"], wts["qn"], wts["kn"], wts["wal"], wts["bal"]]
        specs = [row(D), vec, vec, _full((1, D)), tab, tab, _full((1, LANES)), _full((1, LANES)),
                 _full(wts["wal"].shape), _full(wts["bal"].shape)]
        out_w = (ATT_W, ATT_KV_W, ATT_KV_W, GLA_K_W, GLA_K_W, GLA_V_W, GLA_V_W, D, D)
    else:
        names = ("wk", "wv", "wgk", "wgv", "wlr")
        ins = [x, sh, sc, norm1, wts["kn"], wts["wal"], wts["bal"]]
        specs = [row(D), vec, vec, _full((1, D)), _full((1, LANES)),
                 _full(wts["wal"].shape), _full(wts["bal"].shape)]
        out_w = (ATT_KV_W, ATT_KV_W, GLA_K_W, GLA_V_W)
    ins += [wts[n] for n in names]
    specs += [_full(wts[n].shape) for n in names]
    out_shape = [jax.ShapeDtypeStruct((B, T, w), BF16) for w in out_w]
    out_shape.append(jax.ShapeDtypeStruct((B, T, 2 * GLA_K_W), F32))
    out_specs = [row(w) for w in out_w] + [row(2 * GLA_K_W)]
    return pl.pallas_call(
        functools.partial(_inproj_kernel, rope=rope, full=full),
        out_shape=out_shape, grid=grid, in_specs=specs, out_specs=out_specs,
        compiler_params=_cparams(("parallel", "arbitrary")),
        name="inproj_full" if full else "inproj_ctx",
    )(*ins)


def _attn_kernel(*refs, seq, nsb):
    sink_ref, band_ref, q_ref = refs[:3]
    kblocks = refs[3:nsb + 5]
    kx_ref = refs[nsb + 5]
    vblocks = refs[nsb + 6:2 * nsb + 8]
    vx_ref, o_ref = refs[2 * nsb + 8:]
    n = pl.program_id(1)
    blk = ATT_BLOCK
    nb = seq // blk
    nslab = ATT_W // LANES
    rows = nslab * blk
    lane = lax.broadcasted_iota(jnp.int32, (blk, LANES), 1)
    lo = lane < HEAD_DIM
    hrow = lax.broadcasted_iota(jnp.int32, (rows, 1), 0) // blk
    band = band_ref[...]
    for sb in range(nsb):
        kcat = jnp.concatenate([r[...] for r in kblocks[sb:sb + 3]] + [kx_ref[...]], axis=0)
        vcat = jnp.concatenate([r[...] for r in vblocks[sb:sb + 3]] + [vx_ref[...]], axis=0)
        first = nsb * n + sb - 1
        q = q_ref[sb * blk:(sb + 1) * blk, :]
        outs = []
        for kv in range(ATT_KV_HEADS):
            keep = lo if kv == 0 else jnp.logical_not(lo)
            qs = jnp.concatenate([jnp.where(keep, q[:, m * LANES:(m + 1) * LANES], jnp.zeros((blk, LANES), BF16))
                                  for m in range(nslab)], axis=0)
            s = lax.dot_general(qs, kcat, (((1,), (1,)), ((), ())), preferred_element_type=F32) + band
            s = jnp.concatenate([jnp.where(first >= 0, s[:, :blk], NEG), s[:, blk:2 * blk],
                                 jnp.where(first + 2 < nb, s[:, 2 * blk:3 * blk], NEG), s[:, 3 * blk:]], axis=1)
            snk = jnp.zeros((rows, 1), F32)
            for m in range(nslab):
                snk = jnp.where(hrow == m, sink_ref[kv * ATT_GROUP + m], snk)
            mx = jnp.maximum(jnp.max(s, axis=-1, keepdims=True), snk)
            p = jnp.exp(s - mx)
            den = jnp.sum(p, axis=-1, keepdims=True) + jnp.exp(snk - mx)
            outs.append(jnp.dot(p.astype(BF16), vcat, preferred_element_type=F32) / den)
        for m in range(nslab):
            o_ref[sb * blk:(sb + 1) * blk, m * LANES:(m + 1) * LANES] = jnp.where(
                lo, outs[0][m * blk:(m + 1) * blk], outs[1][m * blk:(m + 1) * blk]).astype(BF16)


def _attention(sink, aq, ak, av, cak, cav):
    B, T, _ = aq.shape
    lc = cak.shape[1]
    blk = ATT_BLOCK
    nb = T // blk
    nsb = 8 if nb % 8 == 0 else (4 if nb % 4 == 0 else 2)
    assert nb % nsb == 0
    kvspec = lambda off: pl.BlockSpec((None, blk, ATT_KV_W),
                                      lambda b, n: (b, jnp.clip(nsb * n + off, 0, nb - 1), 0))
    cspec = pl.BlockSpec((None, lc, ATT_KV_W), lambda b, n: (b, 0, 0))
    qspec = pl.BlockSpec((None, nsb * blk, ATT_W), lambda b, n: (b, n, 0))
    kvs = [kvspec(off) for off in range(-1, nsb + 1)]
    rows = (ATT_W // LANES) * blk
    qi = np.arange(rows)[:, None] % blk
    kj = np.arange(3 * blk + lc)[None, :]
    band = jnp.asarray(np.where((np.abs(kj - blk - qi) <= WINDOW) | (kj >= 3 * blk), 0.0, NEG), F32)
    return pl.pallas_call(
        functools.partial(_attn_kernel, seq=T, nsb=nsb),
        out_shape=jax.ShapeDtypeStruct((B, T, ATT_W), BF16),
        grid=(B, nb // nsb),
        in_specs=[pl.BlockSpec(memory_space=pltpu.SMEM), _full(band.shape), qspec] + kvs + [cspec] + kvs + [cspec],
        out_specs=qspec,
        compiler_params=_cparams(("parallel", "arbitrary")),
        name="attn",
    )(sink, band, aq, *([ak] * (nsb + 2)), cak, *([av] * (nsb + 2)), cav)


def _gla_constants():
    C, sub, L = GLA_C, GLA_SUB, GLA_LEVELS
    i = np.arange(C)[:, None]
    t = np.arange(C)[None, :]
    tabs = [t <= i]
    rowq, same = [], []
    for l in range(L):
        s = C >> l
        mid = (i // s) * s + s // 2
        rowq.append(np.broadcast_to(i >= mid, (C, C)))
        same.append((i // s) == (t // s))
    shifts, dmask, dvalid = [], [t == i], []
    for d in range(1, sub):
        ok = (i % sub) >= d
        shifts.append(ok & (t == i - d))
        dmask.append(ok & (t == i - d))
        dvalid.append(np.broadcast_to(ok, (C, C)))
    flip = lambda a: a[::-1, ::-1]
    tile = lambda a: np.tile(a, (1, GLA_HEADS))

    def both(xs, lanes):
        f = (lambda a: tile(a)) if lanes else (lambda a: a)
        return np.stack([np.concatenate([f(a) for a in xs], 0),
                         np.concatenate([f(flip(a)) for a in xs], 0)]).astype(np.float32)

    hk = np.arange(GLA_K_W) // GLA_DK
    hv = np.arange(GLA_V_W) // GLA_DV
    ind = (hk[:, None] == hk[None, :]).astype(np.float32)
    bdv = (hk[:, None] == hv[None, :]).astype(np.float32)
    return (both(tabs, False), both(shifts, False), both(rowq, True), both(same, True),
            both(dmask, True), ind, bdv, np.ascontiguousarray(bdv.T), both(dvalid, True))


def _gla_chunk(q_b, k_b, v_b, la, cst, d):
    tri_ref, shm_ref, lv_ref, sm_ref, dm_ref, ind_ref, bdv_ref, hm_ref, dv_ref = cst
    C = GLA_C
    kw = GLA_K_W
    q = q_b.astype(F32)
    k = k_b.astype(F32)
    hi = la.astype(BF16)
    r1 = la - hi.astype(F32)
    mid = r1.astype(BF16)
    lo = (r1 - mid.astype(F32)).astype(BF16)
    b3 = jnp.dot(tri_ref[d], jnp.concatenate([hi, mid, lo], axis=1), preferred_element_type=F32)
    b = b3[:, :kw] + b3[:, kw:2 * kw] + b3[:, 2 * kw:]
    last = b[C - 1:C] if d == 0 else b[0:1]

    qt = (q * jnp.exp(b)).astype(BF16)
    kt = (k * jnp.exp(last - b)).astype(BF16)
    gamma = jnp.exp(last)

    ind = ind_ref[...]
    a = None
    for l in range(GLA_LEVELS):
        s = C >> l
        off = s // 2 - 1 if d == 0 else s // 2
        bref = jnp.concatenate([jnp.broadcast_to(b[st + off:st + off + 1], (s, kw)) for st in range(0, C, s)], axis=0)
        rq = lv_ref[d, l * C:(l + 1) * C, :]
        el = jnp.exp((b - bref) * (2.0 * rq - 1.0))
        qh = (q * (el * rq)).astype(BF16)
        kh = (k * (el * (1.0 - rq))).astype(BF16)
        bdk = jnp.concatenate([kh] * GLA_HEADS, axis=0) * ind
        al = lax.dot_general(qh, bdk, (((1,), (1,)), ((), ())), preferred_element_type=F32)
        a = al if l == 0 else a + al * sm_ref[d, l * C:(l + 1) * C, :]
    ksh = jnp.dot(shm_ref[d], k_b, preferred_element_type=F32)
    ps = [q * k]
    for j in range(1, GLA_SUB):
        bsh = pltpu.roll(b, j if d == 0 else C - j, 0)
        ej = jnp.exp((b - bsh) * dv_ref[d, (j - 1) * C:j * C, :])
        ps.append(q * ksh[(j - 1) * C:j * C] * ej)
    w = jnp.dot(jnp.concatenate(ps, axis=0).astype(BF16), ind, preferred_element_type=F32)
    for j in range(GLA_SUB):
        a = a + w[j * C:(j + 1) * C] * dm_ref[d, j * C:(j + 1) * C, :]

    bdv = jnp.concatenate([v_b] * GLA_HEADS, axis=0) * bdv_ref[...]
    o_intra = jnp.dot(a.astype(BF16), bdv, preferred_element_type=F32)
    upd = lax.dot_general(v_b, kt, (((0,), (0,)), ((), ())), preferred_element_type=F32) * hm_ref[...]
    return o_intra, qt, upd, gamma


def _gla_kernel(qf_ref, kf_ref, vf_ref, laf_ref, qb_ref, kb_ref, vb_ref, lab_ref,
                ckf_ref, cvf_ref, claf_ref, ckb_ref, cvb_ref, clab_ref,
                tri_ref, shm_ref, lv_ref, sm_ref, dm_ref, ind_ref, bdv_ref, hm_ref, dv_ref,
                of_ref, ob_ref, st_ref, *, n_ctx_steps):
    s = pl.program_id(1)
    C = GLA_C
    cst = (tri_ref, shm_ref, lv_ref, sm_ref, dm_ref, ind_ref, bdv_ref, hm_ref, dv_ref)

    @pl.when(s == 0)
    def _():
        st_ref[...] = jnp.zeros_like(st_ref)

    is_ctx = s < n_ctx_steps
    dirs = ((0, qf_ref, kf_ref, vf_ref, laf_ref, ckf_ref, cvf_ref, claf_ref, of_ref),
            (1, qb_ref, kb_ref, vb_ref, lab_ref, ckb_ref, cvb_ref, clab_ref, ob_ref))
    nbat = qf_ref.shape[0]
    states = [[st_ref[bi, 0], st_ref[bi, 1]] for bi in range(nbat)]
    for idx in range(2):
        for bi in range(nbat):
            for d, q_ref, k_ref, v_ref, la_ref, ck_ref, cv_ref, cla_ref, o_ref in dirs:
                c = idx if d == 0 else 1 - idx
                rows = slice(c * C, (c + 1) * C)
                k_b = jnp.where(is_ctx, ck_ref[bi, rows, :], k_ref[bi, rows, :])
                v_b = jnp.where(is_ctx, cv_ref[bi, rows, :], v_ref[bi, rows, :])
                la = jnp.where(is_ctx, cla_ref[bi, rows, :], la_ref[bi, rows, :])
                o_intra, qt, upd, gamma = _gla_chunk(q_ref[bi, rows, :], k_b, v_b, la, cst, d)
                st = states[bi][d]
                o_ref[bi, rows, :] = o_intra + lax.dot_general(qt, st.astype(BF16), (((1,), (1,)), ((), ())),
                                                               preferred_element_type=F32)
                states[bi][d] = st * gamma + upd
    for bi in range(nbat):
        st_ref[bi, 0] = states[bi][0]
        st_ref[bi, 1] = states[bi][1]


def _gla(gq, gk, gv, la, cgk, cgv, cla):
    B, T, _ = gq.shape
    lc = cgk.shape[1]
    R = 2 * GLA_C
    assert lc % R == 0 and T % R == 0
    n_ctx, n_lat = lc // R, T // R
    consts = _gla_constants()
    tri, shm = jnp.asarray(consts[0], BF16), jnp.asarray(consts[1], BF16)
    lv, sm, dm = [jnp.asarray(c) for c in consts[2:5]]
    ind, bdv = jnp.asarray(consts[5], BF16), jnp.asarray(consts[6], BF16)
    hm, dv = jnp.asarray(consts[7]), jnp.asarray(consts[8])

    def lat(s, d):
        j = jnp.maximum(s - n_ctx, 0)
        return j if d == 0 else n_lat - 1 - j

    def ctx(s, d):
        j = jnp.minimum(s, n_ctx - 1)
        return j if d == 0 else n_ctx - 1 - j

    nbat = 4 if B % 4 == 0 else (2 if B % 2 == 0 else 1)
    lspec = lambda w, d, c=0: pl.BlockSpec((nbat, R, w), lambda b, s: (b, lat(s, d), c))
    cspec = lambda w, d, c=0: pl.BlockSpec((nbat, R, w), lambda b, s: (b, ctx(s, d), c))
    lat_specs = lambda d: [lspec(GLA_K_W, d), lspec(GLA_K_W, d), lspec(GLA_V_W, d), lspec(GLA_K_W, d, d)]
    ctx_specs = lambda d: [cspec(GLA_K_W, d), cspec(GLA_V_W, d), cspec(GLA_K_W, d, d)]
    cs = [tri, shm, lv, sm, dm, ind, bdv, hm, dv]
    return pl.pallas_call(
        functools.partial(_gla_kernel, n_ctx_steps=n_ctx),
        out_shape=[jax.ShapeDtypeStruct((B, T, GLA_V_W), F32)] * 2,
        grid=(B // nbat, n_ctx + n_lat),
        in_specs=lat_specs(0) + lat_specs(1) + ctx_specs(0) + ctx_specs(1) + [_full(c.shape) for c in cs],
        out_specs=[lspec(GLA_V_W, 0), lspec(GLA_V_W, 1)],
        scratch_shapes=[pltpu.VMEM((nbat, 2, GLA_V_W, GLA_K_W), F32)],
        compiler_params=_cparams(("parallel", "arbitrary")),
        name="gla",
    )(gq, gk, gv, la, gq, gk, gv, la, cgk, cgv, cla, cgk, cgv, cla, *cs)


def _merge_kernel(x_ref, at_ref, of_ref, ob_ref, gr_ref, ga_ref, gg_ref, gt1_ref, sc2_ref, sh2_ref,
                  n2_ref, gn_ref, ltri_ref, wba_ref, wbg_ref, wo_ref, wrh_ref, wrl_ref, br_ref,
                  xn_ref, h2_ref, ti_ref, rk_ref, tw_ref, cnt_ref, carry_ref):
    tm = ltri_ref.shape[0]
    nsplit = x_ref.shape[0] // tm

    @pl.when((pl.program_id(0) == 0) & (pl.program_id(1) == 0))
    def _():
        carry_ref[...] = jnp.zeros_like(carry_ref)

    lane = lax.broadcasted_iota(jnp.int32, (tm, LANES), 1).astype(F32)
    carry = carry_ref[...]
    for part in range(nsplit):
        r = slice(part * tm, (part + 1) * tm)
        go = of_ref[r, :] + ob_ref[r, :]
        parts = []
        for h in range(GLA_HEADS):
            gh = go[:, h * GLA_DV:(h + 1) * GLA_DV]
            ms = jnp.mean(gh * gh, axis=-1, keepdims=True)
            parts.append(gh * lax.rsqrt(ms + EPS))
        o = jnp.concatenate(parts, axis=1) * gn_ref[...] * gr_ref[r, :].astype(F32)
        ya = jnp.dot(at_ref[r, :], wba_ref[...], preferred_element_type=F32)
        yg = jnp.dot(o.astype(BF16), wbg_ref[...], preferred_element_type=F32)
        y = ga_ref[r, :].astype(F32) * ya + gg_ref[r, :].astype(F32) * yg
        z = jnp.dot(y.astype(BF16), wo_ref[...], preferred_element_type=F32)
        xn = x_ref[r, :] + gt1_ref[...] * z
        xn_ref[r, :] = xn
        ms = jnp.mean(xn * xn, axis=-1, keepdims=True)
        h2 = (xn * lax.rsqrt(ms + EPS) * n2_ref[...]) * (1.0 + sc2_ref[...]) + sh2_ref[...]
        hh = h2.astype(BF16)
        hl = (h2 - hh.astype(F32)).astype(BF16)
        h2_ref[r, :] = h2
        logits = (jnp.dot(hh, wrh_ref[...], preferred_element_type=F32)
                  + jnp.dot(hl, wrh_ref[...], preferred_element_type=F32)
                  + jnp.dot(hh, wrl_ref[...], preferred_element_type=F32)) + br_ref[...]
        vals, idxs = [], []
        l = logits
        for _ in range(TOP_K):
            m = jnp.max(l, axis=-1, keepdims=True)
            ix = jnp.min(jnp.where(l == m, lane, float(LANES)), axis=-1, keepdims=True)
            vals.append(m)
            idxs.append(ix)
            l = jnp.where(lane == ix, -3.0e38, l)
        ex = [jnp.exp(v - vals[0]) for v in vals]
        den = ex[0] + ex[1] + ex[2] + ex[3]
        mh = jnp.zeros((tm, LANES), F32)
        for j in range(TOP_K):
            mh = mh + jnp.where(lane == idxs[j], 1.0, 0.0)
        pc = jnp.dot(ltri_ref[...], mh.astype(BF16), preferred_element_type=F32) + carry
        ti = jnp.zeros((tm, LANES), F32)
        rk = jnp.zeros((tm, LANES), F32)
        tw = jnp.zeros((tm, LANES), F32)
        for j in range(TOP_K):
            rj = jnp.sum(jnp.where(lane == idxs[j], pc, 0.0), axis=-1, keepdims=True)
            ti = jnp.where(lane == float(j), idxs[j], ti)
            rk = jnp.where(lane == float(j), rj, rk)
            tw = jnp.where(lane == float(j), ex[j] / den, tw)
        ti_ref[r, :] = ti.astype(jnp.int32)
        rk_ref[r, :] = rk.astype(jnp.int32)
        tw_ref[r, :] = tw
        carry = carry + jnp.sum(mh, axis=0, keepdims=True)
    carry_ref[...] = carry
    cnt_ref[...] = carry.astype(jnp.int32)


def _merge(x, attn_o, gla_o, gr, ga, gg, gt1, sc2, sh2, norm2, wts, *, tm):
    B, T, D = x.shape
    row = lambda w: pl.BlockSpec((None, tm, w), lambda b, t: (b, t, 0))
    vec = pl.BlockSpec((None, 1, D), lambda b, t: (b, 0, 0))
    names = ("wba", "wbg", "wo", "wrh", "wrl", "br")
    tp = tm
    ltri = jnp.asarray(np.tril(np.ones((tp, tp), np.float32), -1), BF16)
    return pl.pallas_call(
        _merge_kernel,
        out_shape=[jax.ShapeDtypeStruct((B, T, D), F32), jax.ShapeDtypeStruct((B, T, D), F32),
                   jax.ShapeDtypeStruct((B, T, LANES), jnp.int32), jax.ShapeDtypeStruct((B, T, LANES), jnp.int32),
                   jax.ShapeDtypeStruct((B, T, LANES), F32), jax.ShapeDtypeStruct((1, LANES), jnp.int32)],
        grid=(B, T // tm),
        in_specs=[row(D), row(ATT_W), row(GLA_V_W), row(GLA_V_W),
                  row(GLA_V_W), row(D), row(D), vec, vec, vec,
                  _full((1, D)), _full((1, GLA_V_W)), _full((tp, tp))] + [_full(wts[n].shape) for n in names],
        out_specs=[row(D), row(D), row(LANES), row(LANES), row(LANES), _full((1, LANES))],
        scratch_shapes=[pltpu.VMEM((1, LANES), F32)],
        compiler_params=_cparams(("arbitrary", "arbitrary")),
        name="merge",
    )(x, attn_o, gla_o[0], gla_o[1], gr, ga, gg, gt1, sc2, sh2, norm2, wts["gn"], ltri, *[wts[n] for n in names])


def _dest_kernel(ti_ref, rk_ref, tw_ref, ps_ref, o_ref, w_ref):
    tm = ti_ref.shape[0]
    lane = lax.broadcasted_iota(jnp.int32, (tm, LANES), 1)
    ti = ti_ref[...]
    ps = ps_ref[...].astype(F32)
    out = jnp.where(lane < TOP_K, rk_ref[...], 0).astype(F32)
    for k in range(TOP_K):
        start = jnp.sum(jnp.where(lane == ti[:, k:k + 1], ps, 0.0), axis=-1, keepdims=True)
        out = out + jnp.where(lane == k, start, 0.0)
    o_ref[...] = jnp.transpose(out)[0:8, :].astype(jnp.int32)
    w_ref[...] = jnp.transpose(tw_ref[...])[0:8, :]


def _dest(ti, rk, tw, pad_start, *, tm):
    n = ti.shape[0]
    ps = jnp.zeros((1, LANES), jnp.int32).at[0, :N_EXPERTS].set(pad_start)
    row = pl.BlockSpec((tm, LANES), lambda i: (i, 0))
    col = pl.BlockSpec((8, tm), lambda i: (0, i))
    return pl.pallas_call(
        _dest_kernel,
        out_shape=[jax.ShapeDtypeStruct((8, n), jnp.int32), jax.ShapeDtypeStruct((8, n), F32)],
        grid=(n // tm,), in_specs=[row, row, row, _full((1, LANES))], out_specs=[col, col],
        compiler_params=_cparams(("arbitrary",)), name="dest",
    )(ti, rk, tw, ps)


def _dispatch_kernel(d_ref, zs_ref, nu_ref, h_ref, xs_ref, buf, zbuf, isem, sem, zsem,
                     *, tb, nsteps):
    s = pl.program_id(0)
    nblk = xs_ref.shape[0] // MOE_STEP
    slot = s % 3
    nxt = (s + 1) % 3

    def loads(step, sl):
        r0 = pl.multiple_of(step * tb, tb)
        return [pltpu.make_async_copy(h_ref.at[pl.ds(r0, tb), pl.ds(j * LANES, LANES)], buf.at[sl, :, j, :],
                                      isem.at[sl]) for j in range(ROW_TILE)]

    def wait_rows(sl):
        for _ in range(TOP_K):
            pltpu.make_async_copy(buf.at[sl], xs_ref.at[pl.ds(0, tb)], sem.at[sl]).wait()

    @pl.when(s == 0)
    def _():
        zbuf[...] = jnp.zeros_like(zbuf)

        def zstart(e, c):
            z0 = pl.multiple_of(zs_ref[e], MOE_STEP)
            pltpu.make_async_copy(zbuf, xs_ref.at[pl.ds(z0, MOE_STEP)], zsem).start()
            return c

        def zwait(e, c):
            pltpu.make_async_copy(zbuf, xs_ref.at[pl.ds(0, MOE_STEP)], zsem).wait()
            return c

        lax.fori_loop(0, N_EXPERTS, zstart, 0)
        lax.fori_loop(0, N_EXPERTS, zwait, 0)

        def tstart(j, c):
            pltpu.make_async_copy(zbuf, xs_ref.at[pl.ds(pl.multiple_of(j * MOE_STEP, MOE_STEP), MOE_STEP)],
                                  zsem).start()
            return c

        lax.fori_loop(nu_ref[0], nblk, tstart, 0)
        lax.fori_loop(nu_ref[0], nblk, zwait, 0)
        for c in loads(0, 0):
            c.start()

    for c in loads(s, slot):
        c.wait()

    @pl.when(s + 1 < nsteps)
    def _():
        @pl.when(s >= 2)
        def _():
            wait_rows(nxt)
        for c in loads(s + 1, nxt):
            c.start()

    def issue(r, c):
        for k in range(TOP_K):
            d = d_ref[0, r * TOP_K + k]
            pltpu.make_async_copy(buf.at[slot, r], xs_ref.at[d], sem.at[slot]).start(priority=k % 2)
        return c

    lax.fori_loop(0, tb, issue, 0, unroll=8)

    @pl.when(s == nsteps - 1)
    def _():
        wait_rows(slot)
        if nsteps >= 2:
            wait_rows((s + 2) % 3)
        if nsteps >= 3:
            wait_rows(nxt)


def _dispatch(dest4, zstart, n_used, h2, cap, *, tb):
    n, D = h2.shape
    nsteps = n // tb
    idx = pl.BlockSpec((None, 1, tb * TOP_K), lambda s: (s, 0, 0), memory_space=pltpu.SMEM)
    smem = pl.BlockSpec(memory_space=pltpu.SMEM)
    anyspec = pl.BlockSpec(memory_space=pl.ANY)
    return pl.pallas_call(
        functools.partial(_dispatch_kernel, tb=tb, nsteps=nsteps),
        out_shape=jax.ShapeDtypeStruct((cap, ROW_TILE, LANES), F32),
        grid=(nsteps,),
        in_specs=[idx, smem, smem, anyspec],
        out_specs=anyspec,
        scratch_shapes=[pltpu.VMEM((3, tb, ROW_TILE, LANES), F32), pltpu.VMEM((MOE_STEP, ROW_TILE, LANES), F32),
                        pltpu.SemaphoreType.DMA((3,)), pltpu.SemaphoreType.DMA((3,)), pltpu.SemaphoreType.DMA(())],
        compiler_params=_cparams(("arbitrary",)),
        name="dispatch",
    )(dest4, zstart, n_used, h2)


def _expert_kernel(be_ref, nv_ref, fs_ref, nx_ref, pr_ref, xs_ref, w1_ref, b1_ref, w2_ref, b2_ref, ys_ref,
                   w1b, w2b, w1f, w2f, xin, yout, isem, osem, wsem, *, nsteps):
    i = pl.program_id(0)
    slot = i % 2

    def wloads(ex, sl):
        return [pltpu.make_async_copy(w1_ref.at[ex], w1f.at[sl], wsem.at[sl]),
                pltpu.make_async_copy(w2_ref.at[ex], w2f.at[sl], wsem.at[sl])]

    def loads(step, sl):
        r0 = pl.multiple_of(step * MOE_STEP, MOE_STEP)
        return [pltpu.make_async_copy(xs_ref.at[pl.ds(r0, MOE_STEP), j, :],
                                      xin.at[sl, :, pl.ds(j * LANES, LANES)], isem.at[sl]) for j in range(ROW_TILE)]

    def stores(step, sl):
        r0 = pl.multiple_of(step * MOE_STEP, MOE_STEP)
        return [pltpu.make_async_copy(yout.at[sl, :, pl.ds(j * LANES, LANES)],
                                      ys_ref.at[pl.ds(r0, MOE_STEP), j, :], osem.at[sl]) for j in range(ROW_TILE)]

    @pl.when(i == 0)
    def _():
        for c in loads(0, 0):
            c.start()
        for c in wloads(be_ref[0], 0):
            c.start()

    @pl.when(fs_ref[i] == 1)
    def _():
        par = pr_ref[i]
        for c in wloads(be_ref[i], par):
            c.wait()
        w1b[...] = w1f[par].astype(BF16)
        w2b[...] = w2f[par].astype(BF16)

        @pl.when(nx_ref[i] >= 0)
        def _():
            for c in wloads(nx_ref[i], 1 - par):
                c.start(priority=1)

    for c in loads(i, slot):
        c.wait()

    @pl.when(i + 1 < nsteps)
    def _():
        for c in loads(i + 1, 1 - slot):
            c.start()

    @pl.when(i >= 2)
    def _():
        for c in stores(i - 2, slot):
            c.wait()

    def mlp(rows):
        xb = xin[slot, 0:rows, :].astype(BF16)
        y = jnp.zeros((rows, D_MODEL), F32)
        fh = D_FF // 2
        for h in range(2):
            g = jnp.dot(xb, w1b[:, h * fh:(h + 1) * fh], preferred_element_type=F32) + b1_ref[:, h * fh:(h + 1) * fh]
            u = (jnp.dot(xb, w1b[:, D_FF + h * fh:D_FF + (h + 1) * fh], preferred_element_type=F32)
                 + b1_ref[:, D_FF + h * fh:D_FF + (h + 1) * fh])
            gate = jnp.minimum(g, SWIGLU_LIMIT)
            up = jnp.clip(u, -SWIGLU_LIMIT, SWIGLU_LIMIT)
            act = gate * (1.0 / (1.0 + jnp.exp(-SWIGLU_ALPHA * gate))) * (up + 1.0)
            y = y + jnp.dot(act.astype(BF16), w2b[h * fh:(h + 1) * fh, :], preferred_element_type=F32)
        yout[slot, 0:rows, :] = y + b2_ref[...]

    nv = nv_ref[i]
    quarter = MOE_STEP // 4
    for j in range(1, 5):
        rows = j * quarter

        @pl.when((nv > rows - quarter) & (nv <= rows))
        def _():
            mlp(rows)
            if rows < MOE_STEP:
                yout[slot, rows:, :] = jnp.zeros((MOE_STEP - rows, D_MODEL), F32)

    @pl.when(nv == 0)
    def _():
        yout[slot] = jnp.zeros((MOE_STEP, D_MODEL), F32)

    for c in stores(i, slot):
        c.start()

    @pl.when(i == nsteps - 1)
    def _():
        for c in stores(i, slot):
            c.wait()
        if nsteps >= 2:
            for c in stores(i - 1, 1 - slot):
                c.wait()


def _experts(blk_e, nv, first, nxt_e, parity, xs, w1, b1, w2, b2):
    cap = xs.shape[0]
    n_blk = cap // MOE_STEP
    ne = w1.shape[0]
    anyspec = pl.BlockSpec(memory_space=pl.ANY)
    bias = lambda w: pl.BlockSpec((None, 1, w), lambda i, be, *_: (be[i], 0, 0))
    gs = pltpu.PrefetchScalarGridSpec(
        num_scalar_prefetch=5, grid=(n_blk,),
        in_specs=[anyspec, anyspec, bias(2 * D_FF), anyspec, bias(D_MODEL)],
        out_specs=anyspec,
        scratch_shapes=[pltpu.VMEM((D_MODEL, 2 * D_FF), BF16), pltpu.VMEM((D_FF, D_MODEL), BF16),
                        pltpu.VMEM((2, D_MODEL, 2 * D_FF), F32), pltpu.VMEM((2, D_FF, D_MODEL), F32),
                        pltpu.VMEM((2, MOE_STEP, D_MODEL), F32), pltpu.VMEM((2, MOE_STEP, D_MODEL), F32),
                        pltpu.SemaphoreType.DMA((2,)), pltpu.SemaphoreType.DMA((2,)),
                        pltpu.SemaphoreType.DMA((2,))])
    return pl.pallas_call(
        functools.partial(_expert_kernel, nsteps=n_blk), grid_spec=gs,
        out_shape=jax.ShapeDtypeStruct((cap, ROW_TILE, LANES), F32),
        compiler_params=_cparams(("arbitrary",)),
        name="experts",
    )(blk_e, nv, first, nxt_e, parity, xs, w1, b1.reshape(ne, 1, 2 * D_FF), w2, b2.reshape(ne, 1, D_MODEL))


def _combine_kernel(dc_ref, dn_ref, tw_ref, gt2_ref, xn_ref, ys_ref, o_ref,
                    gbuf, xt, ot, gsem, xsem, osem, *, tb, nsteps):
    s = pl.program_id(0)
    slot = s % 2
    other = 1 - slot

    def xloads(step, sl):
        r0 = pl.multiple_of(step * tb, tb)
        return [pltpu.make_async_copy(xn_ref.at[pl.ds(r0, tb), pl.ds(j * LANES, LANES)], xt.at[sl, :, j, :],
                                      xsem.at[sl]) for j in range(ROW_TILE)]

    def ostores(step, sl):
        r0 = pl.multiple_of(step * tb, tb)
        return [pltpu.make_async_copy(ot.at[sl, :, j, :], o_ref.at[pl.ds(r0, tb), pl.ds(j * LANES, LANES)],
                                      osem.at[sl]) for j in range(ROW_TILE)]

    def gather_row(d_ref, sl, r):
        for k in range(TOP_K):
            d = d_ref[0, r * TOP_K + k]
            pltpu.make_async_copy(ys_ref.at[d], gbuf.at[sl, k, r], gsem.at[sl]).start(priority=k % 2)

    def wait_gathers(sl):
        for k in range(TOP_K):
            pltpu.make_async_copy(ys_ref.at[pl.ds(0, tb)], gbuf.at[sl, k], gsem.at[sl]).wait()

    @pl.when(s == 0)
    def _():
        def issue(r, c):
            gather_row(dc_ref, 0, r)
            return c

        lax.fori_loop(0, tb, issue, 0, unroll=8)
        for c in xloads(0, 0):
            c.start()

    wait_gathers(slot)
    for c in xloads(s, slot):
        c.wait()

    @pl.when(s >= 2)
    def _():
        for c in ostores(s - 2, slot):
            c.wait()

    @pl.when(s + 1 < nsteps)
    def _():
        def issue(r, c):
            gather_row(dn_ref, other, r)
            return c

        lax.fori_loop(0, tb, issue, 0, unroll=8)
        for c in xloads(s + 1, other):
            c.start()

    g2 = gt2_ref[...]

    def wsum(r, c):
        acc = tw_ref[0, r * TOP_K] * gbuf[slot, 0, r]
        for k in range(1, TOP_K):
            acc = acc + tw_ref[0, r * TOP_K + k] * gbuf[slot, k, r]
        ot[slot, r] = xt[slot, r] + g2 * acc
        return c

    lax.fori_loop(0, tb, wsum, 0, unroll=8)
    for c in ostores(s, slot):
        c.start()

    @pl.when(s == nsteps - 1)
    def _():
        for c in ostores(s, slot):
            c.wait()
        if nsteps >= 2:
            for c in ostores(s - 1, other):
                c.wait()


def _combine(dest4, tw4, gt2t, xn, ys, *, tb, seq):
    n, D = xn.shape
    nsteps = n // tb
    cur = lambda s: (s, 0, 0)
    nxt = lambda s: (jnp.minimum(s + 1, nsteps - 1), 0, 0)
    idx = lambda f: pl.BlockSpec((None, 1, tb * TOP_K), f, memory_space=pltpu.SMEM)
    anyspec = pl.BlockSpec(memory_space=pl.ANY)
    tile = (tb, ROW_TILE, LANES)
    return pl.pallas_call(
        functools.partial(_combine_kernel, tb=tb, nsteps=nsteps),
        out_shape=jax.ShapeDtypeStruct((n, D), F32),
        grid=(nsteps,),
        in_specs=[idx(cur), idx(nxt), idx(cur),
                  pl.BlockSpec((None, ROW_TILE, LANES), lambda s: ((s * tb) // seq, 0, 0)), anyspec, anyspec],
        out_specs=anyspec,
        scratch_shapes=[pltpu.VMEM((2, TOP_K) + tile, F32), pltpu.VMEM((2,) + tile, F32),
                        pltpu.VMEM((2,) + tile, F32), pltpu.SemaphoreType.DMA((2,)),
                        pltpu.SemaphoreType.DMA((2,)), pltpu.SemaphoreType.DMA((2,))],
        compiler_params=_cparams(("arbitrary",)),
        name="combine",
    )(dest4, dest4, tw4, gt2t, xn, ys)


def _rope_tables(T):
    rows = T // GRID_W
    row = jnp.repeat(jnp.arange(rows, dtype=F32), GRID_W)
    col = jnp.tile(jnp.arange(GRID_W, dtype=F32), rows)
    inv = ROPE_BASE ** (-jnp.arange(0, AXIS_ROT, 2, dtype=F32) / AXIS_ROT)
    ang_r, ang_c = row[:, None] * inv, col[:, None] * inv
    m = AXIS_ROT // 2
    ang = jnp.concatenate([ang_r, ang_r, ang_c, ang_c], axis=1)
    sign = jnp.tile(jnp.concatenate([-jnp.ones((m,), F32), jnp.ones((m,), F32)]), 2)
    cos = jnp.tile(jnp.cos(ang), (1, LANES // HEAD_DIM))
    sin = jnp.tile(jnp.sin(ang) * sign, (1, LANES // HEAD_DIM))
    return cos, sin


def _head_perm():
    order = []
    for m in range(ATT_GROUP):
        for kv in range(ATT_KV_HEADS):
            h = kv * ATT_GROUP + m
            order.extend(range(h * HEAD_DIM, (h + 1) * HEAD_DIM))
    return np.asarray(order)


def kernel(x, c, ctx, c_ctx, w_mod, b_mod, norm1, norm2, w_in, q_norm, k_norm, attn_sink,
           w_alpha_f, b_alpha_f, w_alpha_b, b_alpha_b, gla_norm, w_branch_attn, w_branch_gla,
           w_out, w_router, b_router, w_exp_in, b_exp_in, w_exp_out, b_exp_out):
    B, T, D = x.shape
    depth = w_mod.shape[0]
    assert depth == 1, "single-layer kernel: the context stream update only feeds later layers"
    l = 0
    perm = _head_perm()

    rows = ((B + 1 + 7) // 8) * 8
    c_all = jnp.zeros((rows, D), F32).at[:B].set(c).at[B].set(c_ctx)
    mod = _modulation(c_all, w_mod[l], b_mod[l])
    sh1, sc1, gt1, sh2, sc2, gt2 = [mod[:B, j * D:(j + 1) * D].reshape(B, 1, D) for j in range(6)]
    csh1, csc1 = [jnp.broadcast_to(mod[B, j * D:(j + 1) * D].reshape(1, 1, D), (B, 1, D)) for j in range(2)]

    offs = np.concatenate([[0], np.cumsum(IN_SPLITS)])
    cols = lambda j: w_in[l][:, offs[j]:offs[j + 1]]
    wal = jnp.zeros((2 * GLA_RANK, 2 * GLA_K_W), F32)
    wal = wal.at[:GLA_RANK, :GLA_K_W].set(w_alpha_f[l]).at[GLA_RANK:, GLA_K_W:].set(w_alpha_b[l])
    pw = {
        "wq": cols(0)[:, perm].astype(BF16), "wk": cols(1).astype(BF16), "wv": cols(2).astype(BF16),
        "wgq": cols(3).astype(BF16), "wgk": cols(4).astype(BF16), "wgv": cols(5).astype(BF16),
        "wgr": cols(6).astype(BF16), "wga": cols(9).astype(BF16), "wgg": cols(10).astype(BF16),
        "wlr": jnp.concatenate([cols(7), cols(8)], axis=1).astype(BF16),
        "qn": jnp.tile(q_norm[l], LANES // HEAD_DIM).reshape(1, LANES),
        "kn": jnp.tile(k_norm[l], LANES // HEAD_DIM).reshape(1, LANES),
        "wal": wal.astype(BF16),
        "bal": jnp.concatenate([b_alpha_f[l], b_alpha_b[l]]).reshape(1, 2 * GLA_K_W),
    }
    cos, sin = _rope_tables(T)
    n1 = norm1[l].reshape(1, D)
    tm = min(512, T)
    aq, ak, av, gq, gk, gv, gr, ga, gg, la = _inproj(
        x, sh1, sc1, n1, {"cos": cos, "sin": sin}, pw, rope=True, full=True, tm=tm)
    cak, cav, cgk, cgv, cla = _inproj(
        ctx, csh1, csc1, n1, None, pw, rope=False, full=False, tm=min(256, ctx.shape[1]))

    attn_o = _attention(attn_sink[l], aq, ak, av, cak, cav)
    gla_o = _gla(gq, gk, gv, la, cgk, cgv, cla)

    wr = jnp.zeros((D, LANES), F32).at[:, :N_EXPERTS].set(w_router[l])
    wrh = wr.astype(BF16)
    mw = {
        "gn": jnp.tile(gla_norm[l], GLA_HEADS).reshape(1, GLA_V_W),
        "wba": w_branch_attn[l][perm, :].astype(BF16), "wbg": w_branch_gla[l].astype(BF16),
        "wo": w_out[l].astype(BF16), "wrh": wrh, "wrl": (wr - wrh.astype(F32)).astype(BF16),
        "br": jnp.full((1, LANES), NEG, F32).at[0, :N_EXPERTS].set(b_router[l]),
    }
    xn, h2, ti, rk, tw, cnt = _merge(x, attn_o, gla_o, gr, ga, gg, gt1, sc2, sh2, norm2[l].reshape(1, D), mw, tm=tm)

    n = B * T
    counts = cnt[0, :N_EXPERTS]
    padded = (counts + MOE_STEP - 1) // MOE_STEP * MOE_STEP
    pad_end = jnp.cumsum(padded)
    pad_start = (pad_end - padded).astype(jnp.int32)
    zstart = jnp.maximum(pad_end - MOE_STEP, 0).astype(jnp.int32)
    cap = (n * TOP_K + N_EXPERTS * (MOE_STEP - 1)) // MOE_STEP * MOE_STEP
    n_blk = cap // MOE_STEP
    row0 = jnp.arange(n_blk, dtype=jnp.int32) * MOE_STEP
    blk_e = jnp.minimum(jnp.sum((pad_end[None, :] <= row0[:, None]).astype(jnp.int32), axis=1), N_EXPERTS - 1)
    onehot = (blk_e[:, None] == jnp.arange(N_EXPERTS, dtype=jnp.int32)[None, :]).astype(jnp.int32)
    valid_end = jnp.sum(onehot * (pad_start + counts)[None, :], axis=1)
    nv = jnp.clip(valid_end - row0, 0, MOE_STEP).astype(jnp.int32)
    n_used = (pad_end[-1] // MOE_STEP).astype(jnp.int32).reshape(1)
    changed = jnp.concatenate([jnp.ones((1,), bool), blk_e[1:] != blk_e[:-1]])
    first = (changed & (nv > 0)).astype(jnp.int32)
    parity = ((jnp.cumsum(first) - 1) % 2).astype(jnp.int32)
    eid = jnp.arange(N_EXPERTS, dtype=jnp.int32)
    later = (eid[None, :] > eid[:, None]) & (counts[None, :] > 0)
    nxt_of = jnp.where(jnp.any(later, axis=1), jnp.argmax(later, axis=1), -1).astype(jnp.int32)
    nxt_e = jnp.sum(onehot * nxt_of[None, :], axis=1).astype(jnp.int32)

    tb = min(512, T)
    dest_t, tw_t = _dest(ti.reshape(n, LANES), rk.reshape(n, LANES), tw.reshape(n, LANES), pad_start,
                         tm=min(2048, n))
    per_block = lambda a: a[:TOP_K].reshape(TOP_K, n // tb, tb).transpose(1, 2, 0).reshape(n // tb, 1, tb * TOP_K)
    dest4, tw4 = per_block(dest_t), per_block(tw_t)
    xs = _dispatch(dest4, zstart, n_used, h2.reshape(n, D), cap, tb=tb)
    ys = _experts(blk_e, nv, first, nxt_e, parity, xs, w_exp_in[l], b_exp_in[l], w_exp_out[l], b_exp_out[l])
    out = _combine(dest4, tw4, gt2.reshape(B, ROW_TILE, LANES), xn.reshape(n, D), ys, tb=tb, seq=T)
    return out.reshape(B, T, D)
```

```python
import functools

import numpy as np
import jax
import jax.numpy as jnp
from jax import lax
from jax.experimental import pallas as pl
from jax.experimental.pallas import tpu as pltpu

F32 = jnp.float32
BF16 = jnp.bfloat16

D_MODEL = 1024
GRID_W = 64
EPS = 1e-6
ATT_HEADS = 8
ATT_KV_HEADS = 2
ATT_GROUP = ATT_HEADS // ATT_KV_HEADS
HEAD_DIM = 64
WINDOW = 128
ATT_BLOCK = 128
ROPE_BASE = 10000.0
AXIS_ROT = HEAD_DIM // 2
GLA_HEADS = 4
GLA_DK = 64
GLA_DV = 128
GLA_RANK = 16
GLA_TAU = 16.0
N_EXPERTS = 32
TOP_K = 4
D_FF = D_MODEL
SWIGLU_ALPHA = 1.702
SWIGLU_LIMIT = 7.0
MOE_STEP = 512

ATT_W = ATT_HEADS * HEAD_DIM
ATT_KV_W = ATT_KV_HEADS * HEAD_DIM
GLA_K_W = GLA_HEADS * GLA_DK
GLA_V_W = GLA_HEADS * GLA_DV
IN_SPLITS = (ATT_W, ATT_KV_W, ATT_KV_W, GLA_K_W, GLA_K_W, GLA_V_W, GLA_V_W, GLA_RANK, GLA_RANK, D_MODEL, D_MODEL)

LANES = 128
ROW_TILE = D_MODEL // LANES
VMEM_LIMIT = 56 * 1024 * 1024
NEG = -1e30

GLA_C = 64
GLA_SUB = 4
GLA_LEVELS = 4


def _cparams(sem):
    return pltpu.CompilerParams(dimension_semantics=sem, vmem_limit_bytes=VMEM_LIMIT)


def _full(shape):
    n = len(shape)
    return pl.BlockSpec(shape, lambda *_: (0,) * n)


def _mod_kernel(c_ref, w_ref, b_ref, o_ref):
    c = c_ref[...]
    s = c * (1.0 / (1.0 + jnp.exp(-c)))
    o_ref[...] = jnp.dot(s, w_ref[...], preferred_element_type=F32,
                         precision=lax.Precision.HIGHEST) + b_ref[...]


def _modulation(c_all, w_mod, b_mod):
    rows = c_all.shape[0]
    n = w_mod.shape[1]
    tn = 1536
    return pl.pallas_call(
        _mod_kernel,
        out_shape=jax.ShapeDtypeStruct((rows, n), F32),
        grid=(n // tn,),
        in_specs=[pl.BlockSpec((rows, D_MODEL), lambda j: (0, 0)),
                  pl.BlockSpec((D_MODEL, tn), lambda j: (0, j)),
                  pl.BlockSpec((1, tn), lambda j: (0, j))],
        out_specs=pl.BlockSpec((rows, tn), lambda j: (0, j)),
        compiler_params=_cparams(("arbitrary",)),
        name="mod",
    )(c_all, w_mod, b_mod.reshape(1, n))


def _pair_norm(a, g, lo):
    s = a * a
    tot = jnp.sum(s, axis=-1, keepdims=True)
    slo = jnp.sum(jnp.where(lo, s, 0.0), axis=-1, keepdims=True)
    ms = jnp.where(lo, slo, tot - slo) * (1.0 / HEAD_DIM)
    return a * lax.rsqrt(ms + EPS) * g


def _rope(y, cos, sin, first):
    up = pltpu.roll(y, LANES - AXIS_ROT // 2, 1)
    dn = pltpu.roll(y, AXIS_ROT // 2, 1)
    return y * cos + jnp.where(first, up, dn) * sin


def _inproj_kernel(*refs, rope, full):
    if full:
        (x_ref, sh_ref, sc_ref, n1_ref, cos_ref, sin_ref, qn_ref, kn_ref, wal_ref, bal_ref,
         wq, wk, wv, wgq, wgk, wgv, wgr, wga, wgg, wlr,
         oq, ok, ov, ogq, ogk, ogv, ogr, oga, ogg, ola) = refs
    else:
        (x_ref, sh_ref, sc_ref, n1_ref, kn_ref, wal_ref, bal_ref,
         wk, wv, wgk, wgv, wlr,
         ok, ov, ogk, ogv, ola) = refs
    rows_total = x_ref.shape[0]
    nsplit = 2 if rows_total % 512 == 0 else 1
    tm = rows_total // nsplit
    lane = lax.broadcasted_iota(jnp.int32, (tm, LANES), 1)
    lo = lane < HEAD_DIM
    first = (lane % AXIS_ROT) < (AXIS_ROT // 2)
    for part in range(nsplit):
        r = slice(part * tm, (part + 1) * tm)
        x = x_ref[r, :]
        ms = jnp.mean(x * x, axis=-1, keepdims=True)
        h = (x * lax.rsqrt(ms + EPS) * n1_ref[...]) * (1.0 + sc_ref[...]) + sh_ref[...]
        hb = h.astype(BF16)

        def proj(w_ref):
            return jnp.dot(hb, w_ref[...], preferred_element_type=F32)

        if rope:
            cos = cos_ref[r, :]
            sin = sin_ref[r, :]

        k = _pair_norm(proj(wk), kn_ref[...], lo)
        if rope:
            k = _rope(k, cos, sin, first)
        ok[r, :] = k.astype(BF16)
        ov[r, :] = proj(wv).astype(BF16)
        ogk[r, :] = proj(wgk).astype(BF16)
        ogv[r, :] = proj(wgv).astype(BF16)
        lr = proj(wlr).astype(BF16)
        z = jnp.dot(lr, wal_ref[...], preferred_element_type=F32) + bal_ref[...]
        ola[r, :] = (jnp.minimum(z, 0.0) - jnp.log(1.0 + jnp.exp(-jnp.abs(z)))) * (1.0 / GLA_TAU)
        if full:
            q = proj(wq)
            for p in range(ATT_W // LANES):
                y = _pair_norm(q[:, p * LANES:(p + 1) * LANES], qn_ref[...], lo)
                if rope:
                    y = _rope(y, cos, sin, first)
                oq[r, p * LANES:(p + 1) * LANES] = (y * HEAD_DIM ** -0.5).astype(BF16)
            ogq[r, :] = (proj(wgq) * GLA_DK ** -0.5).astype(BF16)
            sigmoid = lambda t: 0.5 * jnp.tanh(0.5 * t) + 0.5
            g = proj(wgr)
            ogr[r, :] = (g * sigmoid(g)).astype(BF16)
            oga[r, :] = sigmoid(proj(wga)).astype(BF16)
            ogg[r, :] = sigmoid(proj(wgg)).astype(BF16)


def _inproj(x, sh, sc, norm1, tabs, wts, *, rope, full, tm):
    B, T, D = x.shape
    grid = (B, T // tm)
    row = lambda w: pl.BlockSpec((None, tm, w), lambda b, t: (b, t, 0))
    vec = pl.BlockSpec((None, 1, D), lambda b, t: (b, 0, 0))
    tab = pl.BlockSpec((tm, LANES), lambda b, t: (t, 0))
    if full:
        names = ("wq", "wk", "wv", "wgq", "wgk", "wgv", "wgr", "wga", "wgg", "wlr")
        ins = [x, sh, sc, norm1, tabs["cos"], tabs["sin"], wts["qn"], wts["kn"], wts["wal"], wts["bal"]]
        specs = [row(D), vec, vec, _full((1, D)), tab, tab, _full((1, LANES)), _full((1, LANES)),
                 _full(wts["wal"].shape), _full(wts["bal"].shape)]
        out_w = (ATT_W, ATT_KV_W, ATT_KV_W, GLA_K_W, GLA_K_W, GLA_V_W, GLA_V_W, D, D)
    else:
        names = ("wk", "wv", "wgk", "wgv", "wlr")
        ins = [x, sh, sc, norm1, wts["kn"], wts["wal"], wts["bal"]]
        specs = [row(D), vec, vec, _full((1, D)), _full((1, LANES)),
                 _full(wts["wal"].shape), _full(wts["bal"].shape)]
        out_w = (ATT_KV_W, ATT_KV_W, GLA_K_W, GLA_V_W)
    ins += [wts[n] for n in names]
    specs += [_full(wts[n].shape) for n in names]
    out_shape = [jax.ShapeDtypeStruct((B, T, w), BF16) for w in out_w]
    out_shape.append(jax.ShapeDtypeStruct((B, T, 2 * GLA_K_W), F32))
    out_specs = [row(w) for w in out_w] + [row(2 * GLA_K_W)]
    return pl.pallas_call(
        functools.partial(_inproj_kernel, rope=rope, full=full),
        out_shape=out_shape, grid=grid, in_specs=specs, out_specs=out_specs,
        compiler_params=_cparams(("parallel", "arbitrary")),
        name="inproj_full" if full else "inproj_ctx",
    )(*ins)


def _attn_kernel(*refs, seq, nsb):
    sink_ref, band_ref, q_ref = refs[:3]
    kblocks = refs[3:nsb + 5]
    kx_ref = refs[nsb + 5]
    vblocks = refs[nsb + 6:2 * nsb + 8]
    vx_ref, o_ref = refs[2 * nsb + 8:]
    n = pl.program_id(1)
    blk = ATT_BLOCK
    nb = seq // blk
    nslab = ATT_W // LANES
    rows = nslab * blk
    lane = lax.broadcasted_iota(jnp.int32, (blk, LANES), 1)
    lo = lane < HEAD_DIM
    hrow = lax.broadcasted_iota(jnp.int32, (rows, 1), 0) // blk
    band = band_ref[...]
    for sb in range(nsb):
        kcat = jnp.concatenate([r[...] for r in kblocks[sb:sb + 3]] + [kx_ref[...]], axis=0)
        vcat = jnp.concatenate([r[...] for r in vblocks[sb:sb + 3]] + [vx_ref[...]], axis=0)
        first = nsb * n + sb - 1
        q = q_ref[sb * blk:(sb + 1) * blk, :]
        outs = []
        for kv in range(ATT_KV_HEADS):
            keep = lo if kv == 0 else jnp.logical_not(lo)
            qs = jnp.concatenate([jnp.where(keep, q[:, m * LANES:(m + 1) * LANES], jnp.zeros((blk, LANES), BF16))
                                  for m in range(nslab)], axis=0)
            s = lax.dot_general(qs, kcat, (((1,), (1,)), ((), ())), preferred_element_type=F32)
            s = jnp.concatenate([jnp.where(first >= 0, s[:, :blk] + band[:, :blk], NEG), s[:, blk:2 * blk],
                                 jnp.where(first + 2 < nb, s[:, 2 * blk:3 * blk] + band[:, 2 * blk:3 * blk], NEG),
                                 s[:, 3 * blk:]], axis=1)
            snk = jnp.zeros((rows, 1), F32)
            for m in range(nslab):
                snk = jnp.where(hrow == m, sink_ref[kv * ATT_GROUP + m], snk)
            mx = jnp.maximum(jnp.max(s, axis=-1, keepdims=True), snk)
            p = jnp.exp(s - mx)
            den = jnp.sum(p, axis=-1, keepdims=True) + jnp.exp(snk - mx)
            outs.append(jnp.dot(p.astype(BF16), vcat, preferred_element_type=F32) / den)
        for m in range(nslab):
            o_ref[sb * blk:(sb + 1) * blk, m * LANES:(m + 1) * LANES] = jnp.where(
                lo, outs[0][m * blk:(m + 1) * blk], outs[1][m * blk:(m + 1) * blk]).astype(BF16)


def _attention(sink, aq, ak, av, cak, cav):
    B, T, _ = aq.shape
    lc = cak.shape[1]
    blk = ATT_BLOCK
    nb = T // blk
    nsb = 8 if nb % 8 == 0 else (4 if nb % 4 == 0 else 2)
    assert nb % nsb == 0
    kvspec = lambda off: pl.BlockSpec((None, blk, ATT_KV_W),
                                      lambda b, n: (b, jnp.clip(nsb * n + off, 0, nb - 1), 0))
    cspec = pl.BlockSpec((None, lc, ATT_KV_W), lambda b, n: (b, 0, 0))
    qspec = pl.BlockSpec((None, nsb * blk, ATT_W), lambda b, n: (b, n, 0))
    kvs = [kvspec(off) for off in range(-1, nsb + 1)]
    rows = (ATT_W // LANES) * blk
    qi = np.arange(rows)[:, None] % blk
    kj = np.arange(3 * blk + lc)[None, :]
    band = jnp.asarray(np.where((np.abs(kj - blk - qi) <= WINDOW) | (kj >= 3 * blk), 0.0, NEG), F32)
    return pl.pallas_call(
        functools.partial(_attn_kernel, seq=T, nsb=nsb),
        out_shape=jax.ShapeDtypeStruct((B, T, ATT_W), BF16),
        grid=(B, nb // nsb),
        in_specs=[pl.BlockSpec(memory_space=pltpu.SMEM), _full(band.shape), qspec] + kvs + [cspec] + kvs + [cspec],
        out_specs=qspec,
        compiler_params=_cparams(("parallel", "arbitrary")),
        name="attn",
    )(sink, band, aq, *([ak] * (nsb + 2)), cak, *([av] * (nsb + 2)), cav)


def _gla_constants():
    C, sub, L = GLA_C, GLA_SUB, GLA_LEVELS
    i = np.arange(C)[:, None]
    t = np.arange(C)[None, :]
    tabs = [t <= i]
    rowq, same = [], []
    for l in range(L):
        s = C >> l
        mid = (i // s) * s + s // 2
        rowq.append(np.broadcast_to(i >= mid, (C, C)))
        same.append((i // s) == (t // s))
    shifts, dmask, dvalid = [], [t == i], []
    for d in range(1, sub):
        ok = (i % sub) >= d
        shifts.append(ok & (t == i - d))
        dmask.append(ok & (t == i - d))
        dvalid.append(np.broadcast_to(ok, (C, C)))
    flip = lambda a: a[::-1, ::-1]
    tile = lambda a: np.tile(a, (1, GLA_HEADS))

    def both(xs, lanes):
        f = (lambda a: tile(a)) if lanes else (lambda a: a)
        return np.stack([np.concatenate([f(a) for a in xs], 0),
                         np.concatenate([f(flip(a)) for a in xs], 0)]).astype(np.float32)

    hk = np.arange(GLA_K_W) // GLA_DK
    hv = np.arange(GLA_V_W) // GLA_DV
    ind = (hk[:, None] == hk[None, :]).astype(np.float32)
    bdv = (hk[:, None] == hv[None, :]).astype(np.float32)
    return (both(tabs, False), both(shifts, False), both(rowq, True), both(same, True),
            both(dmask, True), ind, bdv, np.ascontiguousarray(bdv.T), both(dvalid, True))


def _gla_chunk(q_b, k_b, v_b, la, cst, d):
    tri_ref, shm_ref, lv_ref, sm_ref, dm_ref, ind_ref, bdv_ref, hm_ref, dv_ref = cst
    C = GLA_C
    kw = GLA_K_W
    q = q_b.astype(F32)
    k = k_b.astype(F32)
    hi = la.astype(BF16)
    r1 = la - hi.astype(F32)
    mid = r1.astype(BF16)
    lo = (r1 - mid.astype(F32)).astype(BF16)
    b3 = jnp.dot(tri_ref[d], jnp.concatenate([hi, mid, lo], axis=1), preferred_element_type=F32)
    b = b3[:, :kw] + b3[:, kw:2 * kw] + b3[:, 2 * kw:]
    last = b[C - 1:C] if d == 0 else b[0:1]

    qt = (q * jnp.exp(b)).astype(BF16)
    kt = (k * jnp.exp(last - b)).astype(BF16)
    gamma = jnp.exp(last)

    ind = ind_ref[...]
    a = None
    for l in range(GLA_LEVELS):
        s = C >> l
        off = s // 2 - 1 if d == 0 else s // 2
        bref = jnp.concatenate([jnp.broadcast_to(b[st + off:st + off + 1], (s, kw)) for st in range(0, C, s)], axis=0)
        rq = lv_ref[d, l * C:(l + 1) * C, :]
        el = jnp.exp((b - bref) * (2.0 * rq - 1.0))
        qh = (q * (el * rq)).astype(BF16)
        kh = (k * (el * (1.0 - rq))).astype(BF16)
        bdk = jnp.concatenate([kh] * GLA_HEADS, axis=0) * ind
        al = lax.dot_general(qh, bdk, (((1,), (1,)), ((), ())), preferred_element_type=F32)
        a = al if l == 0 else a + al * sm_ref[d, l * C:(l + 1) * C, :]
    ksh = jnp.dot(shm_ref[d], k_b, preferred_element_type=F32)
    ps = [q * k]
    for j in range(1, GLA_SUB):
        bsh = pltpu.roll(b, j if d == 0 else C - j, 0)
        ej = jnp.exp((b - bsh) * dv_ref[d, (j - 1) * C:j * C, :])
        ps.append(q * ksh[(j - 1) * C:j * C] * ej)
    w = jnp.dot(jnp.concatenate(ps, axis=0).astype(BF16), ind, preferred_element_type=F32)
    for j in range(GLA_SUB):
        a = a + w[j * C:(j + 1) * C] * dm_ref[d, j * C:(j + 1) * C, :]

    bdv = jnp.concatenate([v_b] * GLA_HEADS, axis=0) * bdv_ref[...]
    o_intra = jnp.dot(a.astype(BF16), bdv, preferred_element_type=F32)
    upd = lax.dot_general(v_b, kt, (((0,), (0,)), ((), ())), preferred_element_type=F32) * hm_ref[...]
    return o_intra, qt, upd, gamma


def _gla_kernel(qf_ref, kf_ref, vf_ref, laf_ref, qb_ref, kb_ref, vb_ref, lab_ref,
                ckf_ref, cvf_ref, claf_ref, ckb_ref, cvb_ref, clab_ref,
                tri_ref, shm_ref, lv_ref, sm_ref, dm_ref, ind_ref, bdv_ref, hm_ref, dv_ref,
                of_ref, ob_ref, st_ref, *, n_ctx_steps):
    s = pl.program_id(1)
    C = GLA_C
    cst = (tri_ref, shm_ref, lv_ref, sm_ref, dm_ref, ind_ref, bdv_ref, hm_ref, dv_ref)

    @pl.when(s == 0)
    def _():
        st_ref[...] = jnp.zeros_like(st_ref)

    is_ctx = s < n_ctx_steps
    dirs = ((0, qf_ref, kf_ref, vf_ref, laf_ref, ckf_ref, cvf_ref, claf_ref, of_ref),
            (1, qb_ref, kb_ref, vb_ref, lab_ref, ckb_ref, cvb_ref, clab_ref, ob_ref))
    nbat = qf_ref.shape[0]
    states = [[st_ref[bi, 0], st_ref[bi, 1]] for bi in range(nbat)]
    for idx in range(2):
        for bi in range(nbat):
            for d, q_ref, k_ref, v_ref, la_ref, ck_ref, cv_ref, cla_ref, o_ref in dirs:
                c = idx if d == 0 else 1 - idx
                rows = slice(c * C, (c + 1) * C)
                k_b = jnp.where(is_ctx, ck_ref[bi, rows, :], k_ref[bi, rows, :])
                v_b = jnp.where(is_ctx, cv_ref[bi, rows, :], v_ref[bi, rows, :])
                la = jnp.where(is_ctx, cla_ref[bi, rows, :], la_ref[bi, rows, :])
                o_intra, qt, upd, gamma = _gla_chunk(q_ref[bi, rows, :], k_b, v_b, la, cst, d)
                st = states[bi][d]
                o_ref[bi, rows, :] = o_intra + lax.dot_general(qt, st.astype(BF16), (((1,), (1,)), ((), ())),
                                                               preferred_element_type=F32)
                states[bi][d] = st * gamma + upd
    for bi in range(nbat):
        st_ref[bi, 0] = states[bi][0]
        st_ref[bi, 1] = states[bi][1]


def _gla(gq, gk, gv, la, cgk, cgv, cla):
    B, T, _ = gq.shape
    lc = cgk.shape[1]
    R = 2 * GLA_C
    assert lc % R == 0 and T % R == 0
    n_ctx, n_lat = lc // R, T // R
    consts = _gla_constants()
    tri, shm = jnp.asarray(consts[0], BF16), jnp.asarray(consts[1], BF16)
    lv, sm, dm = [jnp.asarray(c) for c in consts[2:5]]
    ind, bdv = jnp.asarray(consts[5], BF16), jnp.asarray(consts[6], BF16)
    hm, dv = jnp.asarray(consts[7]), jnp.asarray(consts[8])

    def lat(s, d):
        j = jnp.maximum(s - n_ctx, 0)
        return j if d == 0 else n_lat - 1 - j

    def ctx(s, d):
        j = jnp.minimum(s, n_ctx - 1)
        return j if d == 0 else n_ctx - 1 - j

    nbat = 4 if B % 4 == 0 else (2 if B % 2 == 0 else 1)
    lspec = lambda w, d, c=0: pl.BlockSpec((nbat, R, w), lambda b, s: (b, lat(s, d), c))
    cspec = lambda w, d, c=0: pl.BlockSpec((nbat, R, w), lambda b, s: (b, ctx(s, d), c))
    lat_specs = lambda d: [lspec(GLA_K_W, d), lspec(GLA_K_W, d), lspec(GLA_V_W, d), lspec(GLA_K_W, d, d)]
    ctx_specs = lambda d: [cspec(GLA_K_W, d), cspec(GLA_V_W, d), cspec(GLA_K_W, d, d)]
    cs = [tri, shm, lv, sm, dm, ind, bdv, hm, dv]
    return pl.pallas_call(
        functools.partial(_gla_kernel, n_ctx_steps=n_ctx),
        out_shape=[jax.ShapeDtypeStruct((B, T, GLA_V_W), F32)] * 2,
        grid=(B // nbat, n_ctx + n_lat),
        in_specs=lat_specs(0) + lat_specs(1) + ctx_specs(0) + ctx_specs(1) + [_full(c.shape) for c in cs],
        out_specs=[lspec(GLA_V_W, 0), lspec(GLA_V_W, 1)],
        scratch_shapes=[pltpu.VMEM((nbat, 2, GLA_V_W, GLA_K_W), F32)],
        compiler_params=_cparams(("parallel", "arbitrary")),
        name="gla",
    )(gq, gk, gv, la, gq, gk, gv, la, cgk, cgv, cla, cgk, cgv, cla, *cs)


def _merge_kernel(x_ref, at_ref, of_ref, ob_ref, gr_ref, ga_ref, gg_ref, gt1_ref, sc2_ref, sh2_ref,
                  n2_ref, gn_ref, ltri_ref, wba_ref, wbg_ref, wo_ref, wrh_ref, wrl_ref, br_ref,
                  xn_ref, h2_ref, ti_ref, rk_ref, tw_ref, cnt_ref, carry_ref):
    tm = ltri_ref.shape[0]
    nsplit = x_ref.shape[0] // tm

    @pl.when((pl.program_id(0) == 0) & (pl.program_id(1) == 0))
    def _():
        carry_ref[...] = jnp.zeros_like(carry_ref)

    lane = lax.broadcasted_iota(jnp.int32, (tm, LANES), 1).astype(F32)
    carry = carry_ref[...]
    for part in range(nsplit):
        r = slice(part * tm, (part + 1) * tm)
        go = of_ref[r, :] + ob_ref[r, :]
        parts = []
        for h in range(GLA_HEADS):
            gh = go[:, h * GLA_DV:(h + 1) * GLA_DV]
            ms = jnp.mean(gh * gh, axis=-1, keepdims=True)
            parts.append(gh * lax.rsqrt(ms + EPS))
        o = jnp.concatenate(parts, axis=1) * gn_ref[...] * gr_ref[r, :].astype(F32)
        ya = jnp.dot(at_ref[r, :], wba_ref[...], preferred_element_type=F32)
        yg = jnp.dot(o.astype(BF16), wbg_ref[...], preferred_element_type=F32)
        y = ga_ref[r, :].astype(F32) * ya + gg_ref[r, :].astype(F32) * yg
        z = jnp.dot(y.astype(BF16), wo_ref[...], preferred_element_type=F32)
        xn = x_ref[r, :] + gt1_ref[...] * z
        xn_ref[r, :] = xn
        ms = jnp.mean(xn * xn, axis=-1, keepdims=True)
        h2 = (xn * lax.rsqrt(ms + EPS) * n2_ref[...]) * (1.0 + sc2_ref[...]) + sh2_ref[...]
        hh = h2.astype(BF16)
        hl = (h2 - hh.astype(F32)).astype(BF16)
        h2_ref[r, :] = h2
        logits = (jnp.dot(hh, wrh_ref[...], preferred_element_type=F32)
                  + jnp.dot(hl, wrh_ref[...], preferred_element_type=F32)
                  + jnp.dot(hh, wrl_ref[...], preferred_element_type=F32)) + br_ref[...]
        vals, idxs = [], []
        l = logits
        for _ in range(TOP_K):
            m = jnp.max(l, axis=-1, keepdims=True)
            ix = jnp.min(jnp.where(l == m, lane, float(LANES)), axis=-1, keepdims=True)
            vals.append(m)
            idxs.append(ix)
            l = jnp.where(lane == ix, -3.0e38, l)
        ex = [jnp.exp(v - vals[0]) for v in vals]
        den = ex[0] + ex[1] + ex[2] + ex[3]
        mh = jnp.zeros((tm, LANES), F32)
        for j in range(TOP_K):
            mh = mh + jnp.where(lane == idxs[j], 1.0, 0.0)
        pc = jnp.dot(ltri_ref[...], mh.astype(BF16), preferred_element_type=F32) + carry
        ti = jnp.zeros((tm, LANES), F32)
        rk = jnp.zeros((tm, LANES), F32)
        tw = jnp.zeros((tm, LANES), F32)
        for j in range(TOP_K):
            rj = jnp.sum(jnp.where(lane == idxs[j], pc, 0.0), axis=-1, keepdims=True)
            ti = jnp.where(lane == float(j), idxs[j], ti)
            rk = jnp.where(lane == float(j), rj, rk)
            tw = jnp.where(lane == float(j), ex[j] / den, tw)
        ti_ref[r, :] = ti.astype(jnp.int32)
        rk_ref[r, :] = rk.astype(jnp.int32)
        tw_ref[r, :] = tw
        carry = carry + jnp.sum(mh, axis=0, keepdims=True)
    carry_ref[...] = carry
    cnt_ref[...] = carry.astype(jnp.int32)


def _merge(x, attn_o, gla_o, gr, ga, gg, gt1, sc2, sh2, norm2, wts, *, tm):
    B, T, D = x.shape
    row = lambda w: pl.BlockSpec((None, tm, w), lambda b, t: (b, t, 0))
    vec = pl.BlockSpec((None, 1, D), lambda b, t: (b, 0, 0))
    names = ("wba", "wbg", "wo", "wrh", "wrl", "br")
    tp = tm
    ltri = jnp.asarray(np.tril(np.ones((tp, tp), np.float32), -1), BF16)
    return pl.pallas_call(
        _merge_kernel,
        out_shape=[jax.ShapeDtypeStruct((B, T, D), F32), jax.ShapeDtypeStruct((B, T, D), F32),
                   jax.ShapeDtypeStruct((B, T, LANES), jnp.int32), jax.ShapeDtypeStruct((B, T, LANES), jnp.int32),
                   jax.ShapeDtypeStruct((B, T, LANES), F32), jax.ShapeDtypeStruct((1, LANES), jnp.int32)],
        grid=(B, T // tm),
        in_specs=[row(D), row(ATT_W), row(GLA_V_W), row(GLA_V_W),
                  row(GLA_V_W), row(D), row(D), vec, vec, vec,
                  _full((1, D)), _full((1, GLA_V_W)), _full((tp, tp))] + [_full(wts[n].shape) for n in names],
        out_specs=[row(D), row(D), row(LANES), row(LANES), row(LANES), _full((1, LANES))],
        scratch_shapes=[pltpu.VMEM((1, LANES), F32)],
        compiler_params=_cparams(("arbitrary", "arbitrary")),
        name="merge",
    )(x, attn_o, gla_o[0], gla_o[1], gr, ga, gg, gt1, sc2, sh2, norm2, wts["gn"], ltri, *[wts[n] for n in names])


def _dest_kernel(ti_ref, rk_ref, tw_ref, ps_ref, o_ref, w_ref):
    tm = ti_ref.shape[0]
    lane = lax.broadcasted_iota(jnp.int32, (tm, LANES), 1)
    ti = ti_ref[...]
    ps = ps_ref[...].astype(F32)
    out = jnp.where(lane < TOP_K, rk_ref[...], 0).astype(F32)
    for k in range(TOP_K):
        start = jnp.sum(jnp.where(lane == ti[:, k:k + 1], ps, 0.0), axis=-1, keepdims=True)
        out = out + jnp.where(lane == k, start, 0.0)
    o_ref[...] = jnp.transpose(out)[0:8, :].astype(jnp.int32)
    w_ref[...] = jnp.transpose(tw_ref[...])[0:8, :]


def _dest(ti, rk, tw, pad_start, *, tm):
    n = ti.shape[0]
    ps = jnp.zeros((1, LANES), jnp.int32).at[0, :N_EXPERTS].set(pad_start)
    row = pl.BlockSpec((tm, LANES), lambda i: (i, 0))
    col = pl.BlockSpec((8, tm), lambda i: (0, i))
    return pl.pallas_call(
        _dest_kernel,
        out_shape=[jax.ShapeDtypeStruct((8, n), jnp.int32), jax.ShapeDtypeStruct((8, n), F32)],
        grid=(n // tm,), in_specs=[row, row, row, _full((1, LANES))], out_specs=[col, col],
        compiler_params=_cparams(("arbitrary",)), name="dest",
    )(ti, rk, tw, ps)


def _dispatch_kernel(d_ref, zs_ref, nu_ref, h_ref, xs_ref, buf, zbuf, isem, sem, zsem,
                     *, tb, nsteps):
    s = pl.program_id(0)
    nblk = xs_ref.shape[0] // MOE_STEP
    slot = s % 3
    nxt = (s + 1) % 3

    def loads(step, sl):
        r0 = pl.multiple_of(step * tb, tb)
        return [pltpu.make_async_copy(h_ref.at[pl.ds(r0, tb), pl.ds(j * LANES, LANES)], buf.at[sl, :, j, :],
                                      isem.at[sl]) for j in range(ROW_TILE)]

    def wait_rows(sl):
        for _ in range(TOP_K):
            pltpu.make_async_copy(buf.at[sl], xs_ref.at[pl.ds(0, tb)], sem.at[sl]).wait()

    @pl.when(s == 0)
    def _():
        zbuf[...] = jnp.zeros_like(zbuf)

        def zstart(e, c):
            z0 = pl.multiple_of(zs_ref[e], MOE_STEP)
            pltpu.make_async_copy(zbuf, xs_ref.at[pl.ds(z0, MOE_STEP)], zsem).start()
            return c

        def zwait(e, c):
            pltpu.make_async_copy(zbuf, xs_ref.at[pl.ds(0, MOE_STEP)], zsem).wait()
            return c

        lax.fori_loop(0, N_EXPERTS, zstart, 0)
        lax.fori_loop(0, N_EXPERTS, zwait, 0)

        def tstart(j, c):
            pltpu.make_async_copy(zbuf, xs_ref.at[pl.ds(pl.multiple_of(j * MOE_STEP, MOE_STEP), MOE_STEP)],
                                  zsem).start()
            return c

        lax.fori_loop(nu_ref[0], nblk, tstart, 0)
        lax.fori_loop(nu_ref[0], nblk, zwait, 0)
        for c in loads(0, 0):
            c.start()

    for c in loads(s, slot):
        c.wait()

    @pl.when(s + 1 < nsteps)
    def _():
        @pl.when(s >= 2)
        def _():
            wait_rows(nxt)
        for c in loads(s + 1, nxt):
            c.start()

    def issue(r, c):
        for k in range(TOP_K):
            d = d_ref[0, r * TOP_K + k]
            pltpu.make_async_copy(buf.at[slot, r], xs_ref.at[d], sem.at[slot]).start(priority=k % 2)
        return c

    lax.fori_loop(0, tb, issue, 0, unroll=8)

    @pl.when(s == nsteps - 1)
    def _():
        wait_rows(slot)
        if nsteps >= 2:
            wait_rows((s + 2) % 3)
        if nsteps >= 3:
            wait_rows(nxt)


def _dispatch(dest4, zstart, n_used, h2, cap, *, tb):
    n, D = h2.shape
    nsteps = n // tb
    idx = pl.BlockSpec((None, 1, tb * TOP_K), lambda s: (s, 0, 0), memory_space=pltpu.SMEM)
    smem = pl.BlockSpec(memory_space=pltpu.SMEM)
    anyspec = pl.BlockSpec(memory_space=pl.ANY)
    return pl.pallas_call(
        functools.partial(_dispatch_kernel, tb=tb, nsteps=nsteps),
        out_shape=jax.ShapeDtypeStruct((cap, ROW_TILE, LANES), F32),
        grid=(nsteps,),
        in_specs=[idx, smem, smem, anyspec],
        out_specs=anyspec,
        scratch_shapes=[pltpu.VMEM((3, tb, ROW_TILE, LANES), F32), pltpu.VMEM((MOE_STEP, ROW_TILE, LANES), F32),
                        pltpu.SemaphoreType.DMA((3,)), pltpu.SemaphoreType.DMA((3,)), pltpu.SemaphoreType.DMA(())],
        compiler_params=_cparams(("arbitrary",)),
        name="dispatch",
    )(dest4, zstart, n_used, h2)


def _expert_kernel(be_ref, nv_ref, fs_ref, nx_ref, pr_ref, xs_ref, w1_ref, b1_ref, w2_ref, b2_ref, ys_ref,
                   w1b, w2b, w1f, w2f, xin, yout, isem, osem, wsem, *, nsteps):
    i = pl.program_id(0)
    slot = i % 2

    def wloads(ex, sl):
        return [pltpu.make_async_copy(w1_ref.at[ex], w1f.at[sl], wsem.at[sl]),
                pltpu.make_async_copy(w2_ref.at[ex], w2f.at[sl], wsem.at[sl])]

    def loads(step, sl):
        r0 = pl.multiple_of(step * MOE_STEP, MOE_STEP)
        return [pltpu.make_async_copy(xs_ref.at[pl.ds(r0, MOE_STEP), j, :],
                                      xin.at[sl, :, pl.ds(j * LANES, LANES)], isem.at[sl]) for j in range(ROW_TILE)]

    def stores(step, sl):
        r0 = pl.multiple_of(step * MOE_STEP, MOE_STEP)
        return [pltpu.make_async_copy(yout.at[sl, :, pl.ds(j * LANES, LANES)],
                                      ys_ref.at[pl.ds(r0, MOE_STEP), j, :], osem.at[sl]) for j in range(ROW_TILE)]

    @pl.when(i == 0)
    def _():
        for c in loads(0, 0):
            c.start()
        for c in wloads(be_ref[0], 0):
            c.start()

    @pl.when(fs_ref[i] == 1)
    def _():
        par = pr_ref[i]
        for c in wloads(be_ref[i], par):
            c.wait()
        w1b[...] = w1f[par].astype(BF16)
        w2b[...] = w2f[par].astype(BF16)

        @pl.when(nx_ref[i] >= 0)
        def _():
            for c in wloads(nx_ref[i], 1 - par):
                c.start(priority=1)

    for c in loads(i, slot):
        c.wait()

    @pl.when(i + 1 < nsteps)
    def _():
        for c in loads(i + 1, 1 - slot):
            c.start()

    @pl.when(i >= 2)
    def _():
        for c in stores(i - 2, slot):
            c.wait()

    def mlp(rows):
        xb = xin[slot, 0:rows, :].astype(BF16)
        y = jnp.zeros((rows, D_MODEL), F32)
        fh = D_FF // 2
        for h in range(2):
            g = jnp.dot(xb, w1b[:, h * fh:(h + 1) * fh], preferred_element_type=F32) + b1_ref[:, h * fh:(h + 1) * fh]
            u = (jnp.dot(xb, w1b[:, D_FF + h * fh:D_FF + (h + 1) * fh], preferred_element_type=F32)
                 + b1_ref[:, D_FF + h * fh:D_FF + (h + 1) * fh])
            gate = jnp.minimum(g, SWIGLU_LIMIT)
            up = jnp.clip(u, -SWIGLU_LIMIT, SWIGLU_LIMIT)
            act = gate * (1.0 / (1.0 + jnp.exp(-SWIGLU_ALPHA * gate))) * (up + 1.0)
            y = y + jnp.dot(act.astype(BF16), w2b[h * fh:(h + 1) * fh, :], preferred_element_type=F32)
        yout[slot, 0:rows, :] = y + b2_ref[...]

    nv = nv_ref[i]
    quarter = MOE_STEP // 4
    for j in range(1, 5):
        rows = j * quarter

        @pl.when((nv > rows - quarter) & (nv <= rows))
        def _():
            mlp(rows)
            if rows < MOE_STEP:
                yout[slot, rows:, :] = jnp.zeros((MOE_STEP - rows, D_MODEL), F32)

    @pl.when(nv == 0)
    def _():
        yout[slot] = jnp.zeros((MOE_STEP, D_MODEL), F32)

    for c in stores(i, slot):
        c.start()

    @pl.when(i == nsteps - 1)
    def _():
        for c in stores(i, slot):
            c.wait()
        if nsteps >= 2:
            for c in stores(i - 1, 1 - slot):
                c.wait()


def _experts(blk_e, nv, first, nxt_e, parity, xs, w1, b1, w2, b2):
    cap = xs.shape[0]
    n_blk = cap // MOE_STEP
    ne = w1.shape[0]
    anyspec = pl.BlockSpec(memory_space=pl.ANY)
    bias = lambda w: pl.BlockSpec((None, 1, w), lambda i, be, *_: (be[i], 0, 0))
    gs = pltpu.PrefetchScalarGridSpec(
        num_scalar_prefetch=5, grid=(n_blk,),
        in_specs=[anyspec, anyspec, bias(2 * D_FF), anyspec, bias(D_MODEL)],
        out_specs=anyspec,
        scratch_shapes=[pltpu.VMEM((D_MODEL, 2 * D_FF), BF16), pltpu.VMEM((D_FF, D_MODEL), BF16),
                        pltpu.VMEM((2, D_MODEL, 2 * D_FF), F32), pltpu.VMEM((2, D_FF, D_MODEL), F32),
                        pltpu.VMEM((2, MOE_STEP, D_MODEL), F32), pltpu.VMEM((2, MOE_STEP, D_MODEL), F32),
                        pltpu.SemaphoreType.DMA((2,)), pltpu.SemaphoreType.DMA((2,)),
                        pltpu.SemaphoreType.DMA((2,))])
    return pl.pallas_call(
        functools.partial(_expert_kernel, nsteps=n_blk), grid_spec=gs,
        out_shape=jax.ShapeDtypeStruct((cap, ROW_TILE, LANES), F32),
        compiler_params=_cparams(("arbitrary",)),
        name="experts",
    )(blk_e, nv, first, nxt_e, parity, xs, w1, b1.reshape(ne, 1, 2 * D_FF), w2, b2.reshape(ne, 1, D_MODEL))


def _combine_kernel(dc_ref, dn_ref, tw_ref, gt2_ref, xn_ref, ys_ref, o_ref,
                    gbuf, xt, ot, gsem, xsem, osem, *, tb, nsteps):
    s = pl.program_id(0)
    slot = s % 2
    other = 1 - slot

    def xloads(step, sl):
        r0 = pl.multiple_of(step * tb, tb)
        return [pltpu.make_async_copy(xn_ref.at[pl.ds(r0, tb), pl.ds(j * LANES, LANES)], xt.at[sl, :, j, :],
                                      xsem.at[sl]) for j in range(ROW_TILE)]

    def ostores(step, sl):
        r0 = pl.multiple_of(step * tb, tb)
        return [pltpu.make_async_copy(ot.at[sl, :, j, :], o_ref.at[pl.ds(r0, tb), pl.ds(j * LANES, LANES)],
                                      osem.at[sl]) for j in range(ROW_TILE)]

    def gather_row(d_ref, sl, r):
        for k in range(TOP_K):
            d = d_ref[0, r * TOP_K + k]
            pltpu.make_async_copy(ys_ref.at[d], gbuf.at[sl, k, r], gsem.at[sl]).start(priority=k % 2)

    def wait_gathers(sl):
        for k in range(TOP_K):
            pltpu.make_async_copy(ys_ref.at[pl.ds(0, tb)], gbuf.at[sl, k], gsem.at[sl]).wait()

    @pl.when(s == 0)
    def _():
        def issue(r, c):
            gather_row(dc_ref, 0, r)
            return c

        lax.fori_loop(0, tb, issue, 0, unroll=8)
        for c in xloads(0, 0):
            c.start()

    wait_gathers(slot)
    for c in xloads(s, slot):
        c.wait()

    @pl.when(s >= 2)
    def _():
        for c in ostores(s - 2, slot):
            c.wait()

    @pl.when(s + 1 < nsteps)
    def _():
        def issue(r, c):
            gather_row(dn_ref, other, r)
            return c

        lax.fori_loop(0, tb, issue, 0, unroll=8)
        for c in xloads(s + 1, other):
            c.start()

    g2 = gt2_ref[...]

    def wsum(r, c):
        acc = tw_ref[0, r * TOP_K] * gbuf[slot, 0, r]
        for k in range(1, TOP_K):
            acc = acc + tw_ref[0, r * TOP_K + k] * gbuf[slot, k, r]
        ot[slot, r] = xt[slot, r] + g2 * acc
        return c

    lax.fori_loop(0, tb, wsum, 0, unroll=8)
    for c in ostores(s, slot):
        c.start()

    @pl.when(s == nsteps - 1)
    def _():
        for c in ostores(s, slot):
            c.wait()
        if nsteps >= 2:
            for c in ostores(s - 1, other):
                c.wait()


def _combine(dest4, tw4, gt2t, xn, ys, *, tb, seq):
    n, D = xn.shape
    nsteps = n // tb
    cur = lambda s: (s, 0, 0)
    nxt = lambda s: (jnp.minimum(s + 1, nsteps - 1), 0, 0)
    idx = lambda f: pl.BlockSpec((None, 1, tb * TOP_K), f, memory_space=pltpu.SMEM)
    anyspec = pl.BlockSpec(memory_space=pl.ANY)
    tile = (tb, ROW_TILE, LANES)
    return pl.pallas_call(
        functools.partial(_combine_kernel, tb=tb, nsteps=nsteps),
        out_shape=jax.ShapeDtypeStruct((n, D), F32),
        grid=(nsteps,),
        in_specs=[idx(cur), idx(nxt), idx(cur),
                  pl.BlockSpec((None, ROW_TILE, LANES), lambda s: ((s * tb) // seq, 0, 0)), anyspec, anyspec],
        out_specs=anyspec,
        scratch_shapes=[pltpu.VMEM((2, TOP_K) + tile, F32), pltpu.VMEM((2,) + tile, F32),
                        pltpu.VMEM((2,) + tile, F32), pltpu.SemaphoreType.DMA((2,)),
                        pltpu.SemaphoreType.DMA((2,)), pltpu.SemaphoreType.DMA((2,))],
        compiler_params=_cparams(("arbitrary",)),
        name="combine",
    )(dest4, dest4, tw4, gt2t, xn, ys)


def _rope_tables(T):
    rows = T // GRID_W
    row = jnp.repeat(jnp.arange(rows, dtype=F32), GRID_W)
    col = jnp.tile(jnp.arange(GRID_W, dtype=F32), rows)
    inv = ROPE_BASE ** (-jnp.arange(0, AXIS_ROT, 2, dtype=F32) / AXIS_ROT)
    ang_r, ang_c = row[:, None] * inv, col[:, None] * inv
    m = AXIS_ROT // 2
    ang = jnp.concatenate([ang_r, ang_r, ang_c, ang_c], axis=1)
    sign = jnp.tile(jnp.concatenate([-jnp.ones((m,), F32), jnp.ones((m,), F32)]), 2)
    cos = jnp.tile(jnp.cos(ang), (1, LANES // HEAD_DIM))
    sin = jnp.tile(jnp.sin(ang) * sign, (1, LANES // HEAD_DIM))
    return cos, sin


def _head_perm():
    order = []
    for m in range(ATT_GROUP):
        for kv in range(ATT_KV_HEADS):
            h = kv * ATT_GROUP + m
            order.extend(range(h * HEAD_DIM, (h + 1) * HEAD_DIM))
    return np.asarray(order)


def kernel(x, c, ctx, c_ctx, w_mod, b_mod, norm1, norm2, w_in, q_norm, k_norm, attn_sink,
           w_alpha_f, b_alpha_f, w_alpha_b, b_alpha_b, gla_norm, w_branch_attn, w_branch_gla,
           w_out, w_router, b_router, w_exp_in, b_exp_in, w_exp_out, b_exp_out):
    B, T, D = x.shape
    depth = w_mod.shape[0]
    assert depth == 1, "single-layer kernel: the context stream update only feeds later layers"
    l = 0
    perm = _head_perm()

    rows = ((B + 1 + 7) // 8) * 8
    c_all = jnp.zeros((rows, D), F32).at[:B].set(c).at[B].set(c_ctx)
    mod = _modulation(c_all, w_mod[l], b_mod[l])
    sh1, sc1, gt1, sh2, sc2, gt2 = [mod[:B, j * D:(j + 1) * D].reshape(B, 1, D) for j in range(6)]
    csh1, csc1 = [jnp.broadcast_to(mod[B, j * D:(j + 1) * D].reshape(1, 1, D), (B, 1, D)) for j in range(2)]

    offs = np.concatenate([[0], np.cumsum(IN_SPLITS)])
    cols = lambda j: w_in[l][:, offs[j]:offs[j + 1]]
    wal = jnp.zeros((2 * GLA_RANK, 2 * GLA_K_W), F32)
    wal = wal.at[:GLA_RANK, :GLA_K_W].set(w_alpha_f[l]).at[GLA_RANK:, GLA_K_W:].set(w_alpha_b[l])
    pw = {
        "wq": cols(0)[:, perm].astype(BF16), "wk": cols(1).astype(BF16), "wv": cols(2).astype(BF16),
        "wgq": cols(3).astype(BF16), "wgk": cols(4).astype(BF16), "wgv": cols(5).astype(BF16),
        "wgr": cols(6).astype(BF16), "wga": cols(9).astype(BF16), "wgg": cols(10).astype(BF16),
        "wlr": jnp.concatenate([cols(7), cols(8)], axis=1).astype(BF16),
        "qn": jnp.tile(q_norm[l], LANES // HEAD_DIM).reshape(1, LANES),
        "kn": jnp.tile(k_norm[l], LANES // HEAD_DIM).reshape(1, LANES),
        "wal": wal.astype(BF16),
        "bal": jnp.concatenate([b_alpha_f[l], b_alpha_b[l]]).reshape(1, 2 * GLA_K_W),
    }
    cos, sin = _rope_tables(T)
    n1 = norm1[l].reshape(1, D)
    tm = min(512, T)
    aq, ak, av, gq, gk, gv, gr, ga, gg, la = _inproj(
        x, sh1, sc1, n1, {"cos": cos, "sin": sin}, pw, rope=True, full=True, tm=tm)
    cak, cav, cgk, cgv, cla = _inproj(
        ctx, csh1, csc1, n1, None, pw, rope=False, full=False, tm=min(256, ctx.shape[1]))

    attn_o = _attention(attn_sink[l], aq, ak, av, cak, cav)
    gla_o = _gla(gq, gk, gv, la, cgk, cgv, cla)

    wr = jnp.zeros((D, LANES), F32).at[:, :N_EXPERTS].set(w_router[l])
    wrh = wr.astype(BF16)
    mw = {
        "gn": jnp.tile(gla_norm[l], GLA_HEADS).reshape(1, GLA_V_W),
        "wba": w_branch_attn[l][perm, :].astype(BF16), "wbg": w_branch_gla[l].astype(BF16),
        "wo": w_out[l].astype(BF16), "wrh": wrh, "wrl": (wr - wrh.astype(F32)).astype(BF16),
        "br": jnp.full((1, LANES), NEG, F32).at[0, :N_EXPERTS].set(b_router[l]),
    }
    xn, h2, ti, rk, tw, cnt = _merge(x, attn_o, gla_o, gr, ga, gg, gt1, sc2, sh2, norm2[l].reshape(1, D), mw, tm=tm)

    n = B * T
    counts = cnt[0, :N_EXPERTS]
    padded = (counts + MOE_STEP - 1) // MOE_STEP * MOE_STEP
    pad_end = jnp.cumsum(padded)
    pad_start = (pad_end - padded).astype(jnp.int32)
    zstart = jnp.maximum(pad_end - MOE_STEP, 0).astype(jnp.int32)
    cap = (n * TOP_K + N_EXPERTS * (MOE_STEP - 1)) // MOE_STEP * MOE_STEP
    n_blk = cap // MOE_STEP
    row0 = jnp.arange(n_blk, dtype=jnp.int32) * MOE_STEP
    blk_e = jnp.minimum(jnp.sum((pad_end[None, :] <= row0[:, None]).astype(jnp.int32), axis=1), N_EXPERTS - 1)
    onehot = (blk_e[:, None] == jnp.arange(N_EXPERTS, dtype=jnp.int32)[None, :]).astype(jnp.int32)
    valid_end = jnp.sum(onehot * (pad_start + counts)[None, :], axis=1)
    nv = jnp.clip(valid_end - row0, 0, MOE_STEP).astype(jnp.int32)
    n_used = (pad_end[-1] // MOE_STEP).astype(jnp.int32).reshape(1)
    changed = jnp.concatenate([jnp.ones((1,), bool), blk_e[1:] != blk_e[:-1]])
    first = (changed & (nv > 0)).astype(jnp.int32)
    parity = ((jnp.cumsum(first) - 1) % 2).astype(jnp.int32)
    eid = jnp.arange(N_EXPERTS, dtype=jnp.int32)
    later = (eid[None, :] > eid[:, None]) & (counts[None, :] > 0)
    nxt_of = jnp.where(jnp.any(later, axis=1), jnp.argmax(later, axis=1), -1).astype(jnp.int32)
    nxt_e = jnp.sum(onehot * nxt_of[None, :], axis=1).astype(jnp.int32)

    tb = min(512, T)
    dest_t, tw_t = _dest(ti.reshape(n, LANES), rk.reshape(n, LANES), tw.reshape(n, LANES), pad_start,
                         tm=min(2048, n))
    per_block = lambda a: a[:TOP_K].reshape(TOP_K, n // tb, tb).transpose(1, 2, 0).reshape(n // tb, 1, tb * TOP_K)
    dest4, tw4 = per_block(dest_t), per_block(tw_t)
    xs = _dispatch(dest4, zstart, n_used, h2.reshape(n, D), cap, tb=tb)
    ys = _experts(blk_e, nv, first, nxt_e, parity, xs, w_exp_in[l], b_exp_in[l], w_exp_out[l], b_exp_out[l])
    out = _combine(dest4, tw4, gt2.reshape(B, ROW_TILE, LANES), xn.reshape(n, D), ys, tb=tb, seq=T)
    return out.reshape(B, T, D)
```

```python
import functools

import numpy as np
import jax
import jax.numpy as jnp
from jax import lax
from jax.experimental import pallas as pl
from jax.experimental.pallas import tpu as pltpu

F32 = jnp.float32
BF16 = jnp.bfloat16

D_MODEL = 1024
GRID_W = 64
EPS = 1e-6
ATT_HEADS = 8
ATT_KV_HEADS = 2
ATT_GROUP = ATT_HEADS // ATT_KV_HEADS
HEAD_DIM = 64
WINDOW = 128
ATT_BLOCK = 128
ROPE_BASE = 10000.0
AXIS_ROT = HEAD_DIM // 2
GLA_HEADS = 4
GLA_DK = 64
GLA_DV = 128
GLA_RANK = 16
GLA_TAU = 16.0
N_EXPERTS = 32
TOP_K = 4
D_FF = D_MODEL
SWIGLU_ALPHA = 1.702
SWIGLU_LIMIT = 7.0
MOE_STEP = 512

ATT_W = ATT_HEADS * HEAD_DIM
ATT_KV_W = ATT_KV_HEADS * HEAD_DIM
GLA_K_W = GLA_HEADS * GLA_DK
GLA_V_W = GLA_HEADS * GLA_DV
IN_SPLITS = (ATT_W, ATT_KV_W, ATT_KV_W, GLA_K_W, GLA_K_W, GLA_V_W, GLA_V_W, GLA_RANK, GLA_RANK, D_MODEL, D_MODEL)

LANES = 128
ROW_TILE = D_MODEL // LANES
VMEM_LIMIT = 56 * 1024 * 1024
NEG = -1e30

GLA_C = 64
GLA_SUB = 4
GLA_LEVELS = 4


def _cparams(sem):
    return pltpu.CompilerParams(dimension_semantics=sem, vmem_limit_bytes=VMEM_LIMIT)


def _full(shape):
    n = len(shape)
    return pl.BlockSpec(shape, lambda *_: (0,) * n)


def _mod_kernel(c_ref, w_ref, b_ref, o_ref):
    c = c_ref[...]
    s = c * (1.0 / (1.0 + jnp.exp(-c)))
    o_ref[...] = jnp.dot(s, w_ref[...], preferred_element_type=F32,
                         precision=lax.Precision.HIGHEST) + b_ref[...]


def _modulation(c_all, w_mod, b_mod):
    rows = c_all.shape[0]
    n = w_mod.shape[1]
    tn = 1536
    return pl.pallas_call(
        _mod_kernel,
        out_shape=jax.ShapeDtypeStruct((rows, n), F32),
        grid=(n // tn,),
        in_specs=[pl.BlockSpec((rows, D_MODEL), lambda j: (0, 0)),
                  pl.BlockSpec((D_MODEL, tn), lambda j: (0, j)),
                  pl.BlockSpec((1, tn), lambda j: (0, j))],
        out_specs=pl.BlockSpec((rows, tn), lambda j: (0, j)),
        compiler_params=_cparams(("arbitrary",)),
        name="mod",
    )(c_all, w_mod, b_mod.reshape(1, n))


def _pair_norm(a, g, lo):
    s = a * a
    tot = jnp.sum(s, axis=-1, keepdims=True)
    slo = jnp.sum(jnp.where(lo, s, 0.0), axis=-1, keepdims=True)
    ms = jnp.where(lo, slo, tot - slo) * (1.0 / HEAD_DIM)
    return a * lax.rsqrt(ms + EPS) * g


def _rope(y, cos, sin, first):
    up = pltpu.roll(y, LANES - AXIS_ROT // 2, 1)
    dn = pltpu.roll(y, AXIS_ROT // 2, 1)
    return y * cos + jnp.where(first, up, dn) * sin


def _inproj_kernel(*refs, rope, full):
    if full:
        (x_ref, sh_ref, sc_ref, n1_ref, cos_ref, sin_ref, qn_ref, kn_ref, wal_ref, bal_ref,
         wq, wk, wv, wgq, wgk, wgv, wgr, wga, wgg, wlr,
         oq, ok, ov, ogq, ogk, ogv, ogr, oga, ogg, ola) = refs
    else:
        (x_ref, sh_ref, sc_ref, n1_ref, kn_ref, wal_ref, bal_ref,
         wk, wv, wgk, wgv, wlr,
         ok, ov, ogk, ogv, ola) = refs
    rows_total = x_ref.shape[0]
    nsplit = 2 if rows_total % 512 == 0 else 1
    tm = rows_total // nsplit
    lane = lax.broadcasted_iota(jnp.int32, (tm, LANES), 1)
    lo = lane < HEAD_DIM
    first = (lane % AXIS_ROT) < (AXIS_ROT // 2)
    for part in range(nsplit):
        r = slice(part * tm, (part + 1) * tm)
        x = x_ref[r, :]
        ms = jnp.mean(x * x, axis=-1, keepdims=True)
        h = (x * lax.rsqrt(ms + EPS) * n1_ref[...]) * (1.0 + sc_ref[...]) + sh_ref[...]
        hb = h.astype(BF16)

        def proj(w_ref):
            return jnp.dot(hb, w_ref[...], preferred_element_type=F32)

        if rope:
            cos = cos_ref[r, :]
            sin = sin_ref[r, :]

        k = _pair_norm(proj(wk), kn_ref[...], lo)
        if rope:
            k = _rope(k, cos, sin, first)
        ok[r, :] = k.astype(BF16)
        ov[r, :] = proj(wv).astype(BF16)
        ogk[r, :] = proj(wgk).astype(BF16)
        ogv[r, :] = proj(wgv).astype(BF16)
        lr = proj(wlr).astype(BF16)
        z = jnp.dot(lr, wal_ref[...], preferred_element_type=F32) + bal_ref[...]
        ola[r, :] = (jnp.minimum(z, 0.0) - jnp.log(1.0 + jnp.exp(-jnp.abs(z)))) * (1.0 / GLA_TAU)
        if full:
            q = proj(wq)
            for p in range(ATT_W // LANES):
                y = _pair_norm(q[:, p * LANES:(p + 1) * LANES], qn_ref[...], lo)
                if rope:
                    y = _rope(y, cos, sin, first)
                oq[r, p * LANES:(p + 1) * LANES] = (y * HEAD_DIM ** -0.5).astype(BF16)
            ogq[r, :] = (proj(wgq) * GLA_DK ** -0.5).astype(BF16)
            sigmoid = lambda t: 0.5 * jnp.tanh(0.5 * t) + 0.5
            g = proj(wgr)
            ogr[r, :] = (g * sigmoid(g)).astype(BF16)
            oga[r, :] = sigmoid(proj(wga)).astype(BF16)
            ogg[r, :] = sigmoid(proj(wgg)).astype(BF16)


def _inproj(x, sh, sc, norm1, tabs, wts, *, rope, full, tm):
    B, T, D = x.shape
    grid = (B, T // tm)
    row = lambda w: pl.BlockSpec((None, tm, w), lambda b, t: (b, t, 0))
    vec = pl.BlockSpec((None, 1, D), lambda b, t: (b, 0, 0))
    tab = pl.BlockSpec((tm, LANES), lambda b, t: (t, 0))
    if full:
        names = ("wq", "wk", "wv", "wgq", "wgk", "wgv", "wgr", "wga", "wgg", "wlr")
        ins = [x, sh, sc, norm1, tabs["cos"], tabs["sin"], wts["qn"], wts["kn"], wts["wal"], wts["bal"]]
        specs = [row(D), vec, vec, _full((1, D)), tab, tab, _full((1, LANES)), _full((1, LANES)),
                 _full(wts["wal"].shape), _full(wts["bal"].shape)]
        out_w = (ATT_W, ATT_KV_W, ATT_KV_W, GLA_K_W, GLA_K_W, GLA_V_W, GLA_V_W, D, D)
    else:
        names = ("wk", "wv", "wgk", "wgv", "wlr")
        ins = [x, sh, sc, norm1, wts["kn"], wts["wal"], wts["bal"]]
        specs = [row(D), vec, vec, _full((1, D)), _full((1, LANES)),
                 _full(wts["wal"].shape), _full(wts["bal"].shape)]
        out_w = (ATT_KV_W, ATT_KV_W, GLA_K_W, GLA_V_W)
    ins += [wts[n] for n in names]
    specs += [_full(wts[n].shape) for n in names]
    out_shape = [jax.ShapeDtypeStruct((B, T, w), BF16) for w in out_w]
    out_shape.append(jax.ShapeDtypeStruct((B, T, 2 * GLA_K_W), F32))
    out_specs = [row(w) for w in out_w] + [row(2 * GLA_K_W)]
    return pl.pallas_call(
        functools.partial(_inproj_kernel, rope=rope, full=full),
        out_shape=out_shape, grid=grid, in_specs=specs, out_specs=out_specs,
        compiler_params=_cparams(("parallel", "arbitrary")),
        name="inproj_full" if full else "inproj_ctx",
    )(*ins)


def _attn_kernel(*refs, seq, nsb):
    sink_ref, band_ref, q_ref = refs[:3]
    kblocks = refs[3:nsb + 5]
    kx_ref = refs[nsb + 5]
    vblocks = refs[nsb + 6:2 * nsb + 8]
    vx_ref, o_ref = refs[2 * nsb + 8:]
    n = pl.program_id(1)
    blk = ATT_BLOCK
    nb = seq // blk
    nslab = ATT_W // LANES
    rows = nslab * blk
    lane = lax.broadcasted_iota(jnp.int32, (blk, LANES), 1)
    lo = lane < HEAD_DIM
    hrow = lax.broadcasted_iota(jnp.int32, (rows, 1), 0) // blk
    band = band_ref[...]
    for sb in range(nsb):
        kcat = jnp.concatenate([r[...] for r in kblocks[sb:sb + 3]] + [kx_ref[...]], axis=0)
        vcat = jnp.concatenate([r[...] for r in vblocks[sb:sb + 3]] + [vx_ref[...]], axis=0)
        first = nsb * n + sb - 1
        q = q_ref[sb * blk:(sb + 1) * blk, :]
        outs = []
        for kv in range(ATT_KV_HEADS):
            keep = lo if kv == 0 else jnp.logical_not(lo)
            qs = jnp.concatenate([jnp.where(keep, q[:, m * LANES:(m + 1) * LANES], jnp.zeros((blk, LANES), BF16))
                                  for m in range(nslab)], axis=0)
            s = lax.dot_general(qs, kcat, (((1,), (1,)), ((), ())), preferred_element_type=F32)
            s = jnp.concatenate([jnp.where(first >= 0, s[:, :blk] + band[:, :blk], NEG), s[:, blk:2 * blk],
                                 jnp.where(first + 2 < nb, s[:, 2 * blk:3 * blk] + band[:, 2 * blk:3 * blk], NEG),
                                 s[:, 3 * blk:]], axis=1)
            snk = jnp.zeros((rows, 1), F32)
            for m in range(nslab):
                snk = jnp.where(hrow == m, sink_ref[kv * ATT_GROUP + m], snk)
            mx = jnp.maximum(jnp.max(s, axis=-1, keepdims=True), snk)
            p = jnp.exp(s - mx)
            den = jnp.sum(p, axis=-1, keepdims=True) + jnp.exp(snk - mx)
            outs.append(jnp.dot(p.astype(BF16), vcat, preferred_element_type=F32) / den)
        for m in range(nslab):
            o_ref[sb * blk:(sb + 1) * blk, m * LANES:(m + 1) * LANES] = jnp.where(
                lo, outs[0][m * blk:(m + 1) * blk], outs[1][m * blk:(m + 1) * blk]).astype(BF16)


def _attention(sink, aq, ak, av, cak, cav):
    B, T, _ = aq.shape
    lc = cak.shape[1]
    blk = ATT_BLOCK
    nb = T // blk
    nsb = 8 if nb % 8 == 0 else (4 if nb % 4 == 0 else 2)
    assert nb % nsb == 0
    kvspec = lambda off: pl.BlockSpec((None, blk, ATT_KV_W),
                                      lambda b, n: (b, jnp.clip(nsb * n + off, 0, nb - 1), 0))
    cspec = pl.BlockSpec((None, lc, ATT_KV_W), lambda b, n: (b, 0, 0))
    qspec = pl.BlockSpec((None, nsb * blk, ATT_W), lambda b, n: (b, n, 0))
    kvs = [kvspec(off) for off in range(-1, nsb + 1)]
    rows = (ATT_W // LANES) * blk
    qi = np.arange(rows)[:, None] % blk
    kj = np.arange(3 * blk + lc)[None, :]
    band = jnp.asarray(np.where((np.abs(kj - blk - qi) <= WINDOW) | (kj >= 3 * blk), 0.0, NEG), F32)
    return pl.pallas_call(
        functools.partial(_attn_kernel, seq=T, nsb=nsb),
        out_shape=jax.ShapeDtypeStruct((B, T, ATT_W), BF16),
        grid=(B, nb // nsb),
        in_specs=[pl.BlockSpec(memory_space=pltpu.SMEM), _full(band.shape), qspec] + kvs + [cspec] + kvs + [cspec],
        out_specs=qspec,
        compiler_params=_cparams(("parallel", "arbitrary")),
        name="attn",
    )(sink, band, aq, *([ak] * (nsb + 2)), cak, *([av] * (nsb + 2)), cav)


def _gla_constants():
    C, sub, L = GLA_C, GLA_SUB, GLA_LEVELS
    i = np.arange(C)[:, None]
    t = np.arange(C)[None, :]
    tabs = [t <= i]
    rowq, same = [], []
    for l in range(L):
        s = C >> l
        mid = (i // s) * s + s // 2
        rowq.append(np.broadcast_to(i >= mid, (C, C)))
        same.append((i // s) == (t // s))
    shifts, dmask, dvalid = [], [t == i], []
    for d in range(1, sub):
        ok = (i % sub) >= d
        shifts.append(ok & (t == i - d))
        dmask.append(ok & (t == i - d))
        dvalid.append(np.broadcast_to(ok, (C, C)))
    flip = lambda a: a[::-1, ::-1]
    tile = lambda a: np.tile(a, (1, GLA_HEADS))

    def both(xs, lanes):
        f = (lambda a: tile(a)) if lanes else (lambda a: a)
        return np.stack([np.concatenate([f(a) for a in xs], 0),
                         np.concatenate([f(flip(a)) for a in xs], 0)]).astype(np.float32)

    hk = np.arange(GLA_K_W) // GLA_DK
    hv = np.arange(GLA_V_W) // GLA_DV
    ind = (hk[:, None] == hk[None, :]).astype(np.float32)
    bdv = (hk[:, None] == hv[None, :]).astype(np.float32)
    return (both(tabs, False), both(shifts, False), both(rowq, True), both(same, True),
            both(dmask, True), ind, bdv, np.ascontiguousarray(bdv.T), both(dvalid, True))


def _gla_chunk(q_b, k_b, v_b, la, cst, d):
    tri_ref, shm_ref, lv_ref, sm_ref, dm_ref, ind_ref, bdv_ref, hm_ref, dv_ref = cst
    C = GLA_C
    kw = GLA_K_W
    q = q_b.astype(F32)
    k = k_b.astype(F32)
    hi = la.astype(BF16)
    r1 = la - hi.astype(F32)
    mid = r1.astype(BF16)
    lo = (r1 - mid.astype(F32)).astype(BF16)
    b3 = jnp.dot(tri_ref[d], jnp.concatenate([hi, mid, lo], axis=1), preferred_element_type=F32)
    b = b3[:, :kw] + b3[:, kw:2 * kw] + b3[:, 2 * kw:]
    last = b[C - 1:C] if d == 0 else b[0:1]

    qt = (q * jnp.exp(b)).astype(BF16)
    kt = (k * jnp.exp(last - b)).astype(BF16)
    gamma = jnp.exp(last)

    ind = ind_ref[...]
    a = None
    for l in range(GLA_LEVELS):
        s = C >> l
        off = s // 2 - 1 if d == 0 else s // 2
        bref = jnp.concatenate([jnp.broadcast_to(b[st + off:st + off + 1], (s, kw)) for st in range(0, C, s)], axis=0)
        rq = lv_ref[d, l * C:(l + 1) * C, :]
        el = jnp.exp((b - bref) * (2.0 * rq - 1.0))
        qh = (q * (el * rq)).astype(BF16)
        kh = (k * (el * (1.0 - rq))).astype(BF16)
        bdk = jnp.concatenate([kh] * GLA_HEADS, axis=0) * ind
        al = lax.dot_general(qh, bdk, (((1,), (1,)), ((), ())), preferred_element_type=F32)
        a = al if l == 0 else a + al * sm_ref[d, l * C:(l + 1) * C, :]
    ksh = jnp.dot(shm_ref[d], k_b, preferred_element_type=F32)
    ps = [q * k]
    for j in range(1, GLA_SUB):
        bsh = pltpu.roll(b, j if d == 0 else C - j, 0)
        ej = jnp.exp((b - bsh) * dv_ref[d, (j - 1) * C:j * C, :])
        ps.append(q * ksh[(j - 1) * C:j * C] * ej)
    w = jnp.dot(jnp.concatenate(ps, axis=0).astype(BF16), ind, preferred_element_type=F32)
    for j in range(GLA_SUB):
        a = a + w[j * C:(j + 1) * C] * dm_ref[d, j * C:(j + 1) * C, :]

    bdv = jnp.concatenate([v_b] * GLA_HEADS, axis=0) * bdv_ref[...]
    o_intra = jnp.dot(a.astype(BF16), bdv, preferred_element_type=F32)
    upd = lax.dot_general(v_b, kt, (((0,), (0,)), ((), ())), preferred_element_type=F32) * hm_ref[...]
    return o_intra, qt, upd, gamma


def _gla_kernel(qf_ref, kf_ref, vf_ref, laf_ref, qb_ref, kb_ref, vb_ref, lab_ref,
                ckf_ref, cvf_ref, claf_ref, ckb_ref, cvb_ref, clab_ref,
                tri_ref, shm_ref, lv_ref, sm_ref, dm_ref, ind_ref, bdv_ref, hm_ref, dv_ref,
                of_ref, ob_ref, st_ref, *, n_ctx_steps):
    s = pl.program_id(1)
    C = GLA_C
    cst = (tri_ref, shm_ref, lv_ref, sm_ref, dm_ref, ind_ref, bdv_ref, hm_ref, dv_ref)

    @pl.when(s == 0)
    def _():
        st_ref[...] = jnp.zeros_like(st_ref)

    is_ctx = s < n_ctx_steps
    dirs = ((0, qf_ref, kf_ref, vf_ref, laf_ref, ckf_ref, cvf_ref, claf_ref, of_ref),
            (1, qb_ref, kb_ref, vb_ref, lab_ref, ckb_ref, cvb_ref, clab_ref, ob_ref))
    nbat = qf_ref.shape[0]
    states = [[st_ref[bi, 0], st_ref[bi, 1]] for bi in range(nbat)]
    for idx in range(2):
        for bi in range(nbat):
            for d, q_ref, k_ref, v_ref, la_ref, ck_ref, cv_ref, cla_ref, o_ref in dirs:
                c = idx if d == 0 else 1 - idx
                rows = slice(c * C, (c + 1) * C)
                k_b = jnp.where(is_ctx, ck_ref[bi, rows, :], k_ref[bi, rows, :])
                v_b = jnp.where(is_ctx, cv_ref[bi, rows, :], v_ref[bi, rows, :])
                la = jnp.where(is_ctx, cla_ref[bi, rows, :], la_ref[bi, rows, :])
                o_intra, qt, upd, gamma = _gla_chunk(q_ref[bi, rows, :], k_b, v_b, la, cst, d)
                st = states[bi][d]
                o_ref[bi, rows, :] = (o_intra + lax.dot_general(qt, st.astype(BF16), (((1,), (1,)), ((), ())),
                                                                preferred_element_type=F32)).astype(o_ref.dtype)
                states[bi][d] = st * gamma + upd
    for bi in range(nbat):
        st_ref[bi, 0] = states[bi][0]
        st_ref[bi, 1] = states[bi][1]


def _gla(gq, gk, gv, la, cgk, cgv, cla):
    B, T, _ = gq.shape
    lc = cgk.shape[1]
    R = 2 * GLA_C
    assert lc % R == 0 and T % R == 0
    n_ctx, n_lat = lc // R, T // R
    consts = _gla_constants()
    tri, shm = jnp.asarray(consts[0], BF16), jnp.asarray(consts[1], BF16)
    lv, sm, dm = [jnp.asarray(c) for c in consts[2:5]]
    ind, bdv = jnp.asarray(consts[5], BF16), jnp.asarray(consts[6], BF16)
    hm, dv = jnp.asarray(consts[7]), jnp.asarray(consts[8])

    def lat(s, d):
        j = jnp.maximum(s - n_ctx, 0)
        return j if d == 0 else n_lat - 1 - j

    def ctx(s, d):
        j = jnp.minimum(s, n_ctx - 1)
        return j if d == 0 else n_ctx - 1 - j

    nbat = 4 if B % 4 == 0 else (2 if B % 2 == 0 else 1)
    lspec = lambda w, d, c=0: pl.BlockSpec((nbat, R, w), lambda b, s: (b, lat(s, d), c))
    cspec = lambda w, d, c=0: pl.BlockSpec((nbat, R, w), lambda b, s: (b, ctx(s, d), c))
    lat_specs = lambda d: [lspec(GLA_K_W, d), lspec(GLA_K_W, d), lspec(GLA_V_W, d), lspec(GLA_K_W, d, d)]
    ctx_specs = lambda d: [cspec(GLA_K_W, d), cspec(GLA_V_W, d), cspec(GLA_K_W, d, d)]
    cs = [tri, shm, lv, sm, dm, ind, bdv, hm, dv]
    return pl.pallas_call(
        functools.partial(_gla_kernel, n_ctx_steps=n_ctx),
        out_shape=[jax.ShapeDtypeStruct((B, T, GLA_V_W), BF16)] * 2,
        grid=(B // nbat, n_ctx + n_lat),
        in_specs=lat_specs(0) + lat_specs(1) + ctx_specs(0) + ctx_specs(1) + [_full(c.shape) for c in cs],
        out_specs=[lspec(GLA_V_W, 0), lspec(GLA_V_W, 1)],
        scratch_shapes=[pltpu.VMEM((nbat, 2, GLA_V_W, GLA_K_W), F32)],
        compiler_params=_cparams(("parallel", "arbitrary")),
        name="gla",
    )(gq, gk, gv, la, gq, gk, gv, la, cgk, cgv, cla, cgk, cgv, cla, *cs)


def _merge_kernel(x_ref, at_ref, of_ref, ob_ref, gr_ref, ga_ref, gg_ref, gt1_ref, sc2_ref, sh2_ref,
                  n2_ref, gn_ref, ltri_ref, wba_ref, wbg_ref, wo_ref, wrh_ref, wrl_ref, br_ref,
                  xn_ref, h2_ref, ti_ref, rk_ref, tw_ref, cnt_ref, carry_ref):
    tm = ltri_ref.shape[0]
    nsplit = x_ref.shape[0] // tm

    @pl.when((pl.program_id(0) == 0) & (pl.program_id(1) == 0))
    def _():
        carry_ref[...] = jnp.zeros_like(carry_ref)

    lane = lax.broadcasted_iota(jnp.int32, (tm, LANES), 1).astype(F32)
    carry = carry_ref[...]
    for part in range(nsplit):
        r = slice(part * tm, (part + 1) * tm)
        go = of_ref[r, :].astype(F32) + ob_ref[r, :].astype(F32)
        parts = []
        for h in range(GLA_HEADS):
            gh = go[:, h * GLA_DV:(h + 1) * GLA_DV]
            ms = jnp.mean(gh * gh, axis=-1, keepdims=True)
            parts.append(gh * lax.rsqrt(ms + EPS))
        o = jnp.concatenate(parts, axis=1) * gn_ref[...] * gr_ref[r, :].astype(F32)
        ya = jnp.dot(at_ref[r, :], wba_ref[...], preferred_element_type=F32)
        yg = jnp.dot(o.astype(BF16), wbg_ref[...], preferred_element_type=F32)
        y = ga_ref[r, :].astype(F32) * ya + gg_ref[r, :].astype(F32) * yg
        z = jnp.dot(y.astype(BF16), wo_ref[...], preferred_element_type=F32)
        xn = x_ref[r, :] + gt1_ref[...] * z
        xn_ref[r, :] = xn
        ms = jnp.mean(xn * xn, axis=-1, keepdims=True)
        h2 = (xn * lax.rsqrt(ms + EPS) * n2_ref[...]) * (1.0 + sc2_ref[...]) + sh2_ref[...]
        hh = h2.astype(BF16)
        hl = (h2 - hh.astype(F32)).astype(BF16)
        h2_ref[r, :] = h2
        logits = (jnp.dot(hh, wrh_ref[...], preferred_element_type=F32)
                  + jnp.dot(hl, wrh_ref[...], preferred_element_type=F32)
                  + jnp.dot(hh, wrl_ref[...], preferred_element_type=F32)) + br_ref[...]
        vals, idxs = [], []
        l = logits
        for _ in range(TOP_K):
            m = jnp.max(l, axis=-1, keepdims=True)
            ix = jnp.min(jnp.where(l == m, lane, float(LANES)), axis=-1, keepdims=True)
            vals.append(m)
            idxs.append(ix)
            l = jnp.where(lane == ix, -3.0e38, l)
        ex = [jnp.exp(v - vals[0]) for v in vals]
        den = ex[0] + ex[1] + ex[2] + ex[3]
        mh = jnp.zeros((tm, LANES), F32)
        for j in range(TOP_K):
            mh = mh + jnp.where(lane == idxs[j], 1.0, 0.0)
        pc = jnp.dot(ltri_ref[...], mh.astype(BF16), preferred_element_type=F32) + carry
        ti = jnp.zeros((tm, LANES), F32)
        rk = jnp.zeros((tm, LANES), F32)
        tw = jnp.zeros((tm, LANES), F32)
        for j in range(TOP_K):
            rj = jnp.sum(jnp.where(lane == idxs[j], pc, 0.0), axis=-1, keepdims=True)
            ti = jnp.where(lane == float(j), idxs[j], ti)
            rk = jnp.where(lane == float(j), rj, rk)
            tw = jnp.where(lane == float(j), ex[j] / den, tw)
        ti_ref[r, :] = ti.astype(jnp.int32)
        rk_ref[r, :] = rk.astype(jnp.int32)
        tw_ref[r, :] = tw
        carry = carry + jnp.sum(mh, axis=0, keepdims=True)
    carry_ref[...] = carry
    cnt_ref[...] = carry.astype(jnp.int32)


def _merge(x, attn_o, gla_o, gr, ga, gg, gt1, sc2, sh2, norm2, wts, *, tm):
    B, T, D = x.shape
    row = lambda w: pl.BlockSpec((None, tm, w), lambda b, t: (b, t, 0))
    vec = pl.BlockSpec((None, 1, D), lambda b, t: (b, 0, 0))
    names = ("wba", "wbg", "wo", "wrh", "wrl", "br")
    tp = tm
    ltri = jnp.asarray(np.tril(np.ones((tp, tp), np.float32), -1), BF16)
    return pl.pallas_call(
        _merge_kernel,
        out_shape=[jax.ShapeDtypeStruct((B, T, D), F32), jax.ShapeDtypeStruct((B, T, D), F32),
                   jax.ShapeDtypeStruct((B, T, LANES), jnp.int32), jax.ShapeDtypeStruct((B, T, LANES), jnp.int32),
                   jax.ShapeDtypeStruct((B, T, LANES), F32), jax.ShapeDtypeStruct((1, LANES), jnp.int32)],
        grid=(B, T // tm),
        in_specs=[row(D), row(ATT_W), row(GLA_V_W), row(GLA_V_W),
                  row(GLA_V_W), row(D), row(D), vec, vec, vec,
                  _full((1, D)), _full((1, GLA_V_W)), _full((tp, tp))] + [_full(wts[n].shape) for n in names],
        out_specs=[row(D), row(D), row(LANES), row(LANES), row(LANES), _full((1, LANES))],
        scratch_shapes=[pltpu.VMEM((1, LANES), F32)],
        compiler_params=_cparams(("arbitrary", "arbitrary")),
        name="merge",
    )(x, attn_o, gla_o[0], gla_o[1], gr, ga, gg, gt1, sc2, sh2, norm2, wts["gn"], ltri, *[wts[n] for n in names])


def _dest_kernel(ti_ref, rk_ref, tw_ref, ps_ref, o_ref, w_ref):
    tm = ti_ref.shape[0]
    lane = lax.broadcasted_iota(jnp.int32, (tm, LANES), 1)
    ti = ti_ref[...]
    ps = ps_ref[...].astype(F32)
    out = jnp.where(lane < TOP_K, rk_ref[...], 0).astype(F32)
    for k in range(TOP_K):
        start = jnp.sum(jnp.where(lane == ti[:, k:k + 1], ps, 0.0), axis=-1, keepdims=True)
        out = out + jnp.where(lane == k, start, 0.0)
    o_ref[...] = jnp.transpose(out)[0:8, :].astype(jnp.int32)
    w_ref[...] = jnp.transpose(tw_ref[...])[0:8, :]


def _dest(ti, rk, tw, pad_start, *, tm):
    n = ti.shape[0]
    ps = jnp.zeros((1, LANES), jnp.int32).at[0, :N_EXPERTS].set(pad_start)
    row = pl.BlockSpec((tm, LANES), lambda i: (i, 0))
    col = pl.BlockSpec((8, tm), lambda i: (0, i))
    return pl.pallas_call(
        _dest_kernel,
        out_shape=[jax.ShapeDtypeStruct((8, n), jnp.int32), jax.ShapeDtypeStruct((8, n), F32)],
        grid=(n // tm,), in_specs=[row, row, row, _full((1, LANES))], out_specs=[col, col],
        compiler_params=_cparams(("arbitrary",)), name="dest",
    )(ti, rk, tw, ps)


def _dispatch_kernel(d_ref, zs_ref, nu_ref, h_ref, xs_ref, buf, zbuf, isem, sem, zsem,
                     *, tb, nsteps):
    s = pl.program_id(0)
    nblk = xs_ref.shape[0] // MOE_STEP
    slot = s % 3
    nxt = (s + 1) % 3

    def loads(step, sl):
        r0 = pl.multiple_of(step * tb, tb)
        return [pltpu.make_async_copy(h_ref.at[pl.ds(r0, tb), pl.ds(j * LANES, LANES)], buf.at[sl, :, j, :],
                                      isem.at[sl]) for j in range(ROW_TILE)]

    def wait_rows(sl):
        for _ in range(TOP_K):
            pltpu.make_async_copy(buf.at[sl], xs_ref.at[pl.ds(0, tb)], sem.at[sl]).wait()

    @pl.when(s == 0)
    def _():
        zbuf[...] = jnp.zeros_like(zbuf)

        def zstart(e, c):
            z0 = pl.multiple_of(zs_ref[e], MOE_STEP)
            pltpu.make_async_copy(zbuf, xs_ref.at[pl.ds(z0, MOE_STEP)], zsem).start()
            return c

        def zwait(e, c):
            pltpu.make_async_copy(zbuf, xs_ref.at[pl.ds(0, MOE_STEP)], zsem).wait()
            return c

        lax.fori_loop(0, N_EXPERTS, zstart, 0)
        lax.fori_loop(0, N_EXPERTS, zwait, 0)

        def tstart(j, c):
            pltpu.make_async_copy(zbuf, xs_ref.at[pl.ds(pl.multiple_of(j * MOE_STEP, MOE_STEP), MOE_STEP)],
                                  zsem).start()
            return c

        lax.fori_loop(nu_ref[0], nblk, tstart, 0)
        lax.fori_loop(nu_ref[0], nblk, zwait, 0)
        for c in loads(0, 0):
            c.start()

    for c in loads(s, slot):
        c.wait()

    @pl.when(s + 1 < nsteps)
    def _():
        @pl.when(s >= 2)
        def _():
            wait_rows(nxt)
        for c in loads(s + 1, nxt):
            c.start()

    def issue(r, c):
        for k in range(TOP_K):
            d = d_ref[0, r * TOP_K + k]
            pltpu.make_async_copy(buf.at[slot, r], xs_ref.at[d], sem.at[slot]).start(priority=k % 2)
        return c

    lax.fori_loop(0, tb, issue, 0, unroll=8)

    @pl.when(s == nsteps - 1)
    def _():
        wait_rows(slot)
        if nsteps >= 2:
            wait_rows((s + 2) % 3)
        if nsteps >= 3:
            wait_rows(nxt)


def _dispatch(dest4, zstart, n_used, h2, cap, *, tb):
    n, D = h2.shape
    nsteps = n // tb
    idx = pl.BlockSpec((None, 1, tb * TOP_K), lambda s: (s, 0, 0), memory_space=pltpu.SMEM)
    smem = pl.BlockSpec(memory_space=pltpu.SMEM)
    anyspec = pl.BlockSpec(memory_space=pl.ANY)
    return pl.pallas_call(
        functools.partial(_dispatch_kernel, tb=tb, nsteps=nsteps),
        out_shape=jax.ShapeDtypeStruct((cap, ROW_TILE, LANES), F32),
        grid=(nsteps,),
        in_specs=[idx, smem, smem, anyspec],
        out_specs=anyspec,
        scratch_shapes=[pltpu.VMEM((3, tb, ROW_TILE, LANES), F32), pltpu.VMEM((MOE_STEP, ROW_TILE, LANES), F32),
                        pltpu.SemaphoreType.DMA((3,)), pltpu.SemaphoreType.DMA((3,)), pltpu.SemaphoreType.DMA(())],
        compiler_params=_cparams(("arbitrary",)),
        name="dispatch",
    )(dest4, zstart, n_used, h2)


def _expert_kernel(be_ref, nv_ref, fs_ref, nx_ref, pr_ref, xs_ref, w1_ref, b1_ref, w2_ref, b2_ref, ys_ref,
                   w1b, w2b, w1f, w2f, xin, yout, isem, osem, wsem, *, nsteps):
    i = pl.program_id(0)
    slot = i % 2

    def wloads(ex, sl):
        return [pltpu.make_async_copy(w1_ref.at[ex], w1f.at[sl], wsem.at[sl]),
                pltpu.make_async_copy(w2_ref.at[ex], w2f.at[sl], wsem.at[sl])]

    def loads(step, sl):
        r0 = pl.multiple_of(step * MOE_STEP, MOE_STEP)
        return [pltpu.make_async_copy(xs_ref.at[pl.ds(r0, MOE_STEP), j, :],
                                      xin.at[sl, :, pl.ds(j * LANES, LANES)], isem.at[sl]) for j in range(ROW_TILE)]

    def stores(step, sl):
        r0 = pl.multiple_of(step * MOE_STEP, MOE_STEP)
        return [pltpu.make_async_copy(yout.at[sl, :, pl.ds(j * LANES, LANES)],
                                      ys_ref.at[pl.ds(r0, MOE_STEP), j, :], osem.at[sl]) for j in range(ROW_TILE)]

    @pl.when(i == 0)
    def _():
        for c in loads(0, 0):
            c.start()
        for c in wloads(be_ref[0], 0):
            c.start()

    @pl.when(fs_ref[i] == 1)
    def _():
        par = pr_ref[i]
        for c in wloads(be_ref[i], par):
            c.wait()
        w1b[...] = w1f[par].astype(BF16)
        w2b[...] = w2f[par].astype(BF16)

        @pl.when(nx_ref[i] >= 0)
        def _():
            for c in wloads(nx_ref[i], 1 - par):
                c.start(priority=1)

    for c in loads(i, slot):
        c.wait()

    @pl.when(i + 1 < nsteps)
    def _():
        for c in loads(i + 1, 1 - slot):
            c.start()

    @pl.when(i >= 2)
    def _():
        for c in stores(i - 2, slot):
            c.wait()

    def mlp(rows):
        xb = xin[slot, 0:rows, :].astype(BF16)
        y = jnp.zeros((rows, D_MODEL), F32)
        fh = D_FF // 2
        for h in range(2):
            g = jnp.dot(xb, w1b[:, h * fh:(h + 1) * fh], preferred_element_type=F32) + b1_ref[:, h * fh:(h + 1) * fh]
            u = (jnp.dot(xb, w1b[:, D_FF + h * fh:D_FF + (h + 1) * fh], preferred_element_type=F32)
                 + b1_ref[:, D_FF + h * fh:D_FF + (h + 1) * fh])
            gate = jnp.minimum(g, SWIGLU_LIMIT)
            up = jnp.clip(u, -SWIGLU_LIMIT, SWIGLU_LIMIT)
            act = gate * (1.0 / (1.0 + jnp.exp(-SWIGLU_ALPHA * gate))) * (up + 1.0)
            y = y + jnp.dot(act.astype(BF16), w2b[h * fh:(h + 1) * fh, :], preferred_element_type=F32)
        yout[slot, 0:rows, :] = y + b2_ref[...]

    nv = nv_ref[i]
    quarter = MOE_STEP // 4
    for j in range(1, 5):
        rows = j * quarter

        @pl.when((nv > rows - quarter) & (nv <= rows))
        def _():
            mlp(rows)
            if rows < MOE_STEP:
                yout[slot, rows:, :] = jnp.zeros((MOE_STEP - rows, D_MODEL), F32)

    @pl.when(nv == 0)
    def _():
        yout[slot] = jnp.zeros((MOE_STEP, D_MODEL), F32)

    for c in stores(i, slot):
        c.start()

    @pl.when(i == nsteps - 1)
    def _():
        for c in stores(i, slot):
            c.wait()
        if nsteps >= 2:
            for c in stores(i - 1, 1 - slot):
                c.wait()


def _experts(blk_e, nv, first, nxt_e, parity, xs, w1, b1, w2, b2):
    cap = xs.shape[0]
    n_blk = cap // MOE_STEP
    ne = w1.shape[0]
    anyspec = pl.BlockSpec(memory_space=pl.ANY)
    bias = lambda w: pl.BlockSpec((None, 1, w), lambda i, be, *_: (be[i], 0, 0))
    gs = pltpu.PrefetchScalarGridSpec(
        num_scalar_prefetch=5, grid=(n_blk,),
        in_specs=[anyspec, anyspec, bias(2 * D_FF), anyspec, bias(D_MODEL)],
        out_specs=anyspec,
        scratch_shapes=[pltpu.VMEM((D_MODEL, 2 * D_FF), BF16), pltpu.VMEM((D_FF, D_MODEL), BF16),
                        pltpu.VMEM((2, D_MODEL, 2 * D_FF), F32), pltpu.VMEM((2, D_FF, D_MODEL), F32),
                        pltpu.VMEM((2, MOE_STEP, D_MODEL), F32), pltpu.VMEM((2, MOE_STEP, D_MODEL), F32),
                        pltpu.SemaphoreType.DMA((2,)), pltpu.SemaphoreType.DMA((2,)),
                        pltpu.SemaphoreType.DMA((2,))])
    return pl.pallas_call(
        functools.partial(_expert_kernel, nsteps=n_blk), grid_spec=gs,
        out_shape=jax.ShapeDtypeStruct((cap, ROW_TILE, LANES), F32),
        compiler_params=_cparams(("arbitrary",)),
        name="experts",
    )(blk_e, nv, first, nxt_e, parity, xs, w1, b1.reshape(ne, 1, 2 * D_FF), w2, b2.reshape(ne, 1, D_MODEL))


def _combine_kernel(dc_ref, dn_ref, tw_ref, gt2_ref, xn_ref, ys_ref, o_ref,
                    gbuf, xt, ot, gsem, xsem, osem, *, tb, nsteps):
    s = pl.program_id(0)
    slot = s % 2
    other = 1 - slot

    def xloads(step, sl):
        r0 = pl.multiple_of(step * tb, tb)
        return [pltpu.make_async_copy(xn_ref.at[pl.ds(r0, tb), pl.ds(j * LANES, LANES)], xt.at[sl, :, j, :],
                                      xsem.at[sl]) for j in range(ROW_TILE)]

    def ostores(step, sl):
        r0 = pl.multiple_of(step * tb, tb)
        return [pltpu.make_async_copy(ot.at[sl, :, j, :], o_ref.at[pl.ds(r0, tb), pl.ds(j * LANES, LANES)],
                                      osem.at[sl]) for j in range(ROW_TILE)]

    def gather_row(d_ref, sl, r):
        for k in range(TOP_K):
            d = d_ref[0, r * TOP_K + k]
            pltpu.make_async_copy(ys_ref.at[d], gbuf.at[sl, k, r], gsem.at[sl]).start(priority=k % 2)

    def wait_gathers(sl):
        for k in range(TOP_K):
            pltpu.make_async_copy(ys_ref.at[pl.ds(0, tb)], gbuf.at[sl, k], gsem.at[sl]).wait()

    @pl.when(s == 0)
    def _():
        def issue(r, c):
            gather_row(dc_ref, 0, r)
            return c

        lax.fori_loop(0, tb, issue, 0, unroll=8)
        for c in xloads(0, 0):
            c.start()

    wait_gathers(slot)
    for c in xloads(s, slot):
        c.wait()

    @pl.when(s >= 2)
    def _():
        for c in ostores(s - 2, slot):
            c.wait()

    @pl.when(s + 1 < nsteps)
    def _():
        def issue(r, c):
            gather_row(dn_ref, other, r)
            return c

        lax.fori_loop(0, tb, issue, 0, unroll=8)
        for c in xloads(s + 1, other):
            c.start()

    g2 = gt2_ref[...]

    def wsum(r, c):
        acc = tw_ref[0, r * TOP_K] * gbuf[slot, 0, r]
        for k in range(1, TOP_K):
            acc = acc + tw_ref[0, r * TOP_K + k] * gbuf[slot, k, r]
        ot[slot, r] = xt[slot, r] + g2 * acc
        return c

    lax.fori_loop(0, tb, wsum, 0, unroll=8)
    for c in ostores(s, slot):
        c.start()

    @pl.when(s == nsteps - 1)
    def _():
        for c in ostores(s, slot):
            c.wait()
        if nsteps >= 2:
            for c in ostores(s - 1, other):
                c.wait()


def _combine(dest4, tw4, gt2t, xn, ys, *, tb, seq):
    n, D = xn.shape
    nsteps = n // tb
    cur = lambda s: (s, 0, 0)
    nxt = lambda s: (jnp.minimum(s + 1, nsteps - 1), 0, 0)
    idx = lambda f: pl.BlockSpec((None, 1, tb * TOP_K), f, memory_space=pltpu.SMEM)
    anyspec = pl.BlockSpec(memory_space=pl.ANY)
    tile = (tb, ROW_TILE, LANES)
    return pl.pallas_call(
        functools.partial(_combine_kernel, tb=tb, nsteps=nsteps),
        out_shape=jax.ShapeDtypeStruct((n, D), F32),
        grid=(nsteps,),
        in_specs=[idx(cur), idx(nxt), idx(cur),
                  pl.BlockSpec((None, ROW_TILE, LANES), lambda s: ((s * tb) // seq, 0, 0)), anyspec, anyspec],
        out_specs=anyspec,
        scratch_shapes=[pltpu.VMEM((2, TOP_K) + tile, F32), pltpu.VMEM((2,) + tile, F32),
                        pltpu.VMEM((2,) + tile, F32), pltpu.SemaphoreType.DMA((2,)),
                        pltpu.SemaphoreType.DMA((2,)), pltpu.SemaphoreType.DMA((2,))],
        compiler_params=_cparams(("arbitrary",)),
        name="combine",
    )(dest4, dest4, tw4, gt2t, xn, ys)


def _rope_tables(T):
    rows = T // GRID_W
    row = jnp.repeat(jnp.arange(rows, dtype=F32), GRID_W)
    col = jnp.tile(jnp.arange(GRID_W, dtype=F32), rows)
    inv = ROPE_BASE ** (-jnp.arange(0, AXIS_ROT, 2, dtype=F32) / AXIS_ROT)
    ang_r, ang_c = row[:, None] * inv, col[:, None] * inv
    m = AXIS_ROT // 2
    ang = jnp.concatenate([ang_r, ang_r, ang_c, ang_c], axis=1)
    sign = jnp.tile(jnp.concatenate([-jnp.ones((m,), F32), jnp.ones((m,), F32)]), 2)
    cos = jnp.tile(jnp.cos(ang), (1, LANES // HEAD_DIM))
    sin = jnp.tile(jnp.sin(ang) * sign, (1, LANES // HEAD_DIM))
    return cos, sin


def _head_perm():
    order = []
    for m in range(ATT_GROUP):
        for kv in range(ATT_KV_HEADS):
            h = kv * ATT_GROUP + m
            order.extend(range(h * HEAD_DIM, (h + 1) * HEAD_DIM))
    return np.asarray(order)


def kernel(x, c, ctx, c_ctx, w_mod, b_mod, norm1, norm2, w_in, q_norm, k_norm, attn_sink,
           w_alpha_f, b_alpha_f, w_alpha_b, b_alpha_b, gla_norm, w_branch_attn, w_branch_gla,
           w_out, w_router, b_router, w_exp_in, b_exp_in, w_exp_out, b_exp_out):
    B, T, D = x.shape
    depth = w_mod.shape[0]
    assert depth == 1, "single-layer kernel: the context stream update only feeds later layers"
    l = 0
    perm = _head_perm()

    rows = ((B + 1 + 7) // 8) * 8
    c_all = jnp.zeros((rows, D), F32).at[:B].set(c).at[B].set(c_ctx)
    mod = _modulation(c_all, w_mod[l], b_mod[l])
    sh1, sc1, gt1, sh2, sc2, gt2 = [mod[:B, j * D:(j + 1) * D].reshape(B, 1, D) for j in range(6)]
    csh1, csc1 = [jnp.broadcast_to(mod[B, j * D:(j + 1) * D].reshape(1, 1, D), (B, 1, D)) for j in range(2)]

    offs = np.concatenate([[0], np.cumsum(IN_SPLITS)])
    cols = lambda j: w_in[l][:, offs[j]:offs[j + 1]]
    wal = jnp.zeros((2 * GLA_RANK, 2 * GLA_K_W), F32)
    wal = wal.at[:GLA_RANK, :GLA_K_W].set(w_alpha_f[l]).at[GLA_RANK:, GLA_K_W:].set(w_alpha_b[l])
    pw = {
        "wq": cols(0)[:, perm].astype(BF16), "wk": cols(1).astype(BF16), "wv": cols(2).astype(BF16),
        "wgq": cols(3).astype(BF16), "wgk": cols(4).astype(BF16), "wgv": cols(5).astype(BF16),
        "wgr": cols(6).astype(BF16), "wga": cols(9).astype(BF16), "wgg": cols(10).astype(BF16),
        "wlr": jnp.concatenate([cols(7), cols(8)], axis=1).astype(BF16),
        "qn": jnp.tile(q_norm[l], LANES // HEAD_DIM).reshape(1, LANES),
        "kn": jnp.tile(k_norm[l], LANES // HEAD_DIM).reshape(1, LANES),
        "wal": wal.astype(BF16),
        "bal": jnp.concatenate([b_alpha_f[l], b_alpha_b[l]]).reshape(1, 2 * GLA_K_W),
    }
    cos, sin = _rope_tables(T)
    n1 = norm1[l].reshape(1, D)
    tm = min(512, T)
    aq, ak, av, gq, gk, gv, gr, ga, gg, la = _inproj(
        x, sh1, sc1, n1, {"cos": cos, "sin": sin}, pw, rope=True, full=True, tm=tm)
    cak, cav, cgk, cgv, cla = _inproj(
        ctx, csh1, csc1, n1, None, pw, rope=False, full=False, tm=min(256, ctx.shape[1]))

    attn_o = _attention(attn_sink[l], aq, ak, av, cak, cav)
    gla_o = _gla(gq, gk, gv, la, cgk, cgv, cla)

    wr = jnp.zeros((D, LANES), F32).at[:, :N_EXPERTS].set(w_router[l])
    wrh = wr.astype(BF16)
    mw = {
        "gn": jnp.tile(gla_norm[l], GLA_HEADS).reshape(1, GLA_V_W),
        "wba": w_branch_attn[l][perm, :].astype(BF16), "wbg": w_branch_gla[l].astype(BF16),
        "wo": w_out[l].astype(BF16), "wrh": wrh, "wrl": (wr - wrh.astype(F32)).astype(BF16),
        "br": jnp.full((1, LANES), NEG, F32).at[0, :N_EXPERTS].set(b_router[l]),
    }
    xn, h2, ti, rk, tw, cnt = _merge(x, attn_o, gla_o, gr, ga, gg, gt1, sc2, sh2, norm2[l].reshape(1, D), mw, tm=tm)

    n = B * T
    counts = cnt[0, :N_EXPERTS]
    padded = (counts + MOE_STEP - 1) // MOE_STEP * MOE_STEP
    pad_end = jnp.cumsum(padded)
    pad_start = (pad_end - padded).astype(jnp.int32)
    zstart = jnp.maximum(pad_end - MOE_STEP, 0).astype(jnp.int32)
    cap = (n * TOP_K + N_EXPERTS * (MOE_STEP - 1)) // MOE_STEP * MOE_STEP
    n_blk = cap // MOE_STEP
    row0 = jnp.arange(n_blk, dtype=jnp.int32) * MOE_STEP
    blk_e = jnp.minimum(jnp.sum((pad_end[None, :] <= row0[:, None]).astype(jnp.int32), axis=1), N_EXPERTS - 1)
    onehot = (blk_e[:, None] == jnp.arange(N_EXPERTS, dtype=jnp.int32)[None, :]).astype(jnp.int32)
    valid_end = jnp.sum(onehot * (pad_start + counts)[None, :], axis=1)
    nv = jnp.clip(valid_end - row0, 0, MOE_STEP).astype(jnp.int32)
    n_used = (pad_end[-1] // MOE_STEP).astype(jnp.int32).reshape(1)
    changed = jnp.concatenate([jnp.ones((1,), bool), blk_e[1:] != blk_e[:-1]])
    first = (changed & (nv > 0)).astype(jnp.int32)
    parity = ((jnp.cumsum(first) - 1) % 2).astype(jnp.int32)
    eid = jnp.arange(N_EXPERTS, dtype=jnp.int32)
    later = (eid[None, :] > eid[:, None]) & (counts[None, :] > 0)
    nxt_of = jnp.where(jnp.any(later, axis=1), jnp.argmax(later, axis=1), -1).astype(jnp.int32)
    nxt_e = jnp.sum(onehot * nxt_of[None, :], axis=1).astype(jnp.int32)

    tb = min(512, T)
    dest_t, tw_t = _dest(ti.reshape(n, LANES), rk.reshape(n, LANES), tw.reshape(n, LANES), pad_start,
                         tm=min(2048, n))
    per_block = lambda a: a[:TOP_K].reshape(TOP_K, n // tb, tb).transpose(1, 2, 0).reshape(n // tb, 1, tb * TOP_K)
    dest4, tw4 = per_block(dest_t), per_block(tw_t)
    xs = _dispatch(dest4, zstart, n_used, h2.reshape(n, D), cap, tb=tb)
    ys = _experts(blk_e, nv, first, nxt_e, parity, xs, w_exp_in[l], b_exp_in[l], w_exp_out[l], b_exp_out[l])
    out = _combine(dest4, tw4, gt2.reshape(B, ROW_TILE, LANES), xn.reshape(n, D), ys, tb=tb, seq=T)
    return out.reshape(B, T, D)
```
